```python
import jax, jax.numpy as jnp
from jax import lax
import numpy as np

D_MODEL = 2048
BATCH = 1
SEQ = 8192
DEPTH = 1

N_MEM = 256
D_MIX = D_MODEL
ATTN_WIDTH = D_MIX // 2
SGU_WIDTH = D_MIX - ATTN_WIDTH
ATTN_HEAD_DIM = 64
N_ATTN_HEADS = ATTN_WIDTH // ATTN_HEAD_DIM
DILATED_BRANCHES = ((128, 1), (512, 4), (2048, 16))
BLOCK = 128
SGU_CHUNK = 128
SGU_GROUP_DIM = 128
N_SGU_GROUPS = SGU_WIDTH // SGU_GROUP_DIM
N_MEM_HEADS = 4
MEM_HEAD_DIM = D_MODEL // N_MEM_HEADS
D_FF = ((8 * D_MODEL // 3 + 255) // 256) * 256
D_IN = 3 * ATTN_WIDTH + 2 * SGU_WIDTH
ROPE_THETA = 10000.0
EPS = 1e-6
FFN_RES_SCALE = 0.5

kernel_name = "hybrid_dilated_attn_sgu_macaron_layer"


def _rmsnorm(x, g):
    x32 = x.astype(jnp.float32)
    y = x32 * lax.rsqrt(jnp.mean(x32 * x32, axis=-1, keepdims=True) + EPS)
    return (y * g.astype(jnp.float32)).astype(x.dtype)


def _layernorm(x, g, b):
    x32 = x.astype(jnp.float32)
    mu = jnp.mean(x32, axis=-1, keepdims=True)
    xc = x32 - mu
    y = xc * lax.rsqrt(jnp.mean(xc * xc, axis=-1, keepdims=True) + EPS)
    return (y * g.astype(jnp.float32) + b.astype(jnp.float32)).astype(x.dtype)


def _swiglu(x, w_gate, w_up, w_down):
    return (jax.nn.silu(x @ w_gate) * (x @ w_up)) @ w_down


def _rope(t, positions):
    e = t.shape[-1]
    inv_freq = ROPE_THETA ** (-jnp.arange(0, e, 2, dtype=jnp.float32) / e)
    ang = positions.astype(jnp.float32)[..., None] * inv_freq
    cos = jnp.cos(ang)[:, :, None, :]
    sin = jnp.sin(ang)[:, :, None, :]
    t32 = t.astype(jnp.float32)
    t1, t2 = t32[..., : e // 2], t32[..., e // 2:]
    return jnp.concatenate([t1 * cos - t2 * sin, t1 * sin + t2 * cos], axis=-1).astype(t.dtype)


def _dilated_branch(q, k, v, window, dilation):
    b, s, h, e = q.shape
    steps = window // dilation
    sub_len = s // dilation
    n_blk = -(-sub_len // BLOCK)
    pad = n_blk * BLOCK - sub_len

    def split(t):
        return t.reshape(b, sub_len, dilation, h, e).transpose(0, 2, 1, 3, 4)

    qs = jnp.pad(split(q), ((0, 0), (0, 0), (0, pad), (0, 0), (0, 0)))
    kv_pad = ((0, 0), (0, 0), (BLOCK, pad), (0, 0), (0, 0))
    ks = jnp.pad(split(k), kv_pad).reshape(b, dilation, n_blk + 1, BLOCK, h, e)
    vs = jnp.pad(split(v), kv_pad).reshape(b, dilation, n_blk + 1, BLOCK, h, e)
    qb = qs.reshape(b, dilation, n_blk, BLOCK, h, e)
    kb = jnp.concatenate([ks[:, :, :-1], ks[:, :, 1:]], axis=3)
    vb = jnp.concatenate([vs[:, :, :-1], vs[:, :, 1:]], axis=3)

    scores = jnp.einsum('brnqhe,brnkhe->brnhqk', qb, kb).astype(jnp.float32) * (e ** -0.5)
    qi = jnp.arange(BLOCK)[:, None]
    kj = jnp.arange(2 * BLOCK)[None, :]
    diff = qi + BLOCK - kj
    band = (diff >= 0) & (diff <= steps)
    blk = jnp.arange(n_blk)[:, None, None]
    valid = band[None] & ((blk > 0) | (kj[None] >= BLOCK))
    scores = jnp.where(valid[None, None, :, None], scores, -jnp.inf)
    m = jnp.max(scores, axis=-1, keepdims=True)
    p = jnp.exp(scores - m)
    l = jnp.sum(p, axis=-1, keepdims=True)
    o = jnp.einsum('brnhqk,brnkhe->brnqhe', (p / l).astype(v.dtype), vb)
    lse = (m + jnp.log(l))[..., 0]

    o = o.reshape(b, dilation, n_blk * BLOCK, h, e)[:, :, :sub_len]
    o = o.transpose(0, 2, 1, 3, 4).reshape(b, s, h, e)
    lse = lse.transpose(0, 1, 2, 4, 3).reshape(b, dilation, n_blk * BLOCK, h)[:, :, :sub_len]
    lse = lse.transpose(0, 2, 1, 3).reshape(b, s, h)
    return o, lse


def _dilated_attention(q, k, v):
    outs, lses = [], []
    for window, dilation in DILATED_BRANCHES:
        o, lse = _dilated_branch(q, k, v, window, dilation)
        outs.append(o)
        lses.append(lse)
    wts = jax.nn.softmax(jnp.stack(lses, axis=0), axis=0)
    return jnp.einsum('nbsh,nbshe->bshe', wts.astype(v.dtype), jnp.stack(outs, axis=0))


def _spatial_gating(u, gv, ln_g, ln_b, w_s, b_s):
    b, s, _ = gv.shape
    u = jax.nn.gelu(u)
    gv = _layernorm(jax.nn.gelu(gv), ln_g, ln_b)
    nc = s // SGU_CHUNK
    vc = gv.reshape(b, nc, SGU_CHUNK, N_SGU_GROUPS, SGU_GROUP_DIM)
    causal = jnp.tril(jnp.ones((SGU_CHUNK, SGU_CHUNK), dtype=bool))
    w = jnp.where(causal[None], w_s, 0.0).astype(gv.dtype)
    sv = jnp.einsum('gij,bcjgd->bcigd', w, vc) + b_s.T.astype(gv.dtype)[None, None, :, :, None]
    return u * sv.reshape(b, s, SGU_WIDTH)


def _token_mixing(h, positions, norm_g, w_in, ln_g, ln_b, w_s, b_s, attn_g, sgu_g, w_out):
    b, s, _ = h.shape
    xn = _rmsnorm(h, norm_g)
    proj = xn @ w_in
    q, k, v, u, gv = jnp.split(
        proj, [ATTN_WIDTH, 2 * ATTN_WIDTH, 3 * ATTN_WIDTH, 3 * ATTN_WIDTH + SGU_WIDTH], axis=-1)
    q = _rope(q.reshape(b, s, N_ATTN_HEADS, ATTN_HEAD_DIM), positions)
    k = _rope(k.reshape(b, s, N_ATTN_HEADS, ATTN_HEAD_DIM), positions)
    v = v.reshape(b, s, N_ATTN_HEADS, ATTN_HEAD_DIM)
    attn = _dilated_attention(q, k, v).reshape(b, s, ATTN_WIDTH)
    sgu = _spatial_gating(u, gv, ln_g, ln_b, w_s, b_s)
    mixed = jnp.concatenate([_rmsnorm(attn, attn_g), _rmsnorm(sgu, sgu_g)], axis=-1)
    return mixed @ w_out


def _memory_cross_attention(h, mem, g_q, g_kv, w_q, w_k, w_v, w_o):
    b, s, _ = h.shape
    n_mem = mem.shape[1]
    xq = _rmsnorm(h, g_q)
    mk = _rmsnorm(mem, g_kv)
    q = (xq @ w_q).reshape(b, s, N_MEM_HEADS, MEM_HEAD_DIM)
    k = (mk @ w_k).reshape(b, n_mem, N_MEM_HEADS, MEM_HEAD_DIM)
    v = (mk @ w_v).reshape(b, n_mem, N_MEM_HEADS, MEM_HEAD_DIM)
    scores = jnp.einsum('bshe,bmhe->bhsm', q, k).astype(jnp.float32) * (MEM_HEAD_DIM ** -0.5)
    p = jax.nn.softmax(scores, axis=-1).astype(v.dtype)
    o = jnp.einsum('bhsm,bmhe->bshe', p, v).reshape(b, s, D_MODEL)
    return o @ w_o


def setup_inputs(seed: int = 0) -> dict:
    key = jax.random.key(seed)
    ks = jax.random.split(key, 32)
    f32 = jnp.float32
    L = DEPTH

    def w(k, shape, fan_in):
        return jax.random.normal(k, shape, f32) * fan_in ** -0.5

    def gain(k, shape):
        return 1.0 + 0.01 * jax.random.normal(k, shape, f32)

    return {
        "x": jax.random.normal(ks[0], (BATCH, SEQ, D_MODEL), f32),
        "mem": jax.random.normal(ks[1], (BATCH, N_MEM, D_MODEL), f32),
        "positions": jnp.broadcast_to(jnp.arange(SEQ, dtype=jnp.int32), (BATCH, SEQ)),
        "ffn1_norm": gain(ks[2], (L, D_MODEL)),
        "ffn1_w_gate": w(ks[3], (L, D_MODEL, D_FF), D_MODEL),
        "ffn1_w_up": w(ks[4], (L, D_MODEL, D_FF), D_MODEL),
        "ffn1_w_down": w(ks[5], (L, D_FF, D_MODEL), D_FF),
        "mix_norm": gain(ks[6], (L, D_MODEL)),
        "w_in": w(ks[7], (L, D_MODEL, D_IN), D_MODEL),
        "sgu_ln_gain": gain(ks[8], (L, SGU_WIDTH)),
        "sgu_ln_bias": 0.01 * jax.random.normal(ks[9], (L, SGU_WIDTH), f32),
        "sgu_w_s": w(ks[10], (L, N_SGU_GROUPS, SGU_CHUNK, SGU_CHUNK), SGU_CHUNK),
        "sgu_b_s": 1.0 + 0.1 * jax.random.normal(ks[11], (L, N_SGU_GROUPS, SGU_CHUNK), f32),
        "attn_out_gain": gain(ks[12], (L, ATTN_WIDTH)),
        "sgu_out_gain": gain(ks[13], (L, SGU_WIDTH)),
        "w_out": w(ks[14], (L, D_MIX, D_MODEL), D_MIX),
        "mem_q_norm": gain(ks[15], (L, D_MODEL)),
        "mem_kv_norm": gain(ks[16], (L, D_MODEL)),
        "mem_w_q": w(ks[17], (L, D_MODEL, D_MODEL), D_MODEL),
        "mem_w_k": w(ks[18], (L, D_MODEL, D_MODEL), D_MODEL),
        "mem_w_v": w(ks[19], (L, D_MODEL, D_MODEL), D_MODEL),
        "mem_w_o": w(ks[20], (L, D_MODEL, D_MODEL), D_MODEL),
        "ffn2_norm": gain(ks[21], (L, D_MODEL)),
        "ffn2_w_gate": w(ks[22], (L, D_MODEL, D_FF), D_MODEL),
        "ffn2_w_up": w(ks[23], (L, D_MODEL, D_FF), D_MODEL),
        "ffn2_w_down": w(ks[24], (L, D_FF, D_MODEL), D_FF),
        "final_norm": gain(ks[25], (D_MODEL,)),
    }


def reference(x, mem, positions,
              ffn1_norm, ffn1_w_gate, ffn1_w_up, ffn1_w_down,
              mix_norm, w_in, sgu_ln_gain, sgu_ln_bias, sgu_w_s, sgu_b_s,
              attn_out_gain, sgu_out_gain, w_out,
              mem_q_norm, mem_kv_norm, mem_w_q, mem_w_k, mem_w_v, mem_w_o,
              ffn2_norm, ffn2_w_gate, ffn2_w_up, ffn2_w_down,
              final_norm):
    h = x
    for layer in range(DEPTH):
        h = h + FFN_RES_SCALE * _swiglu(_rmsnorm(h, ffn1_norm[layer]),
                                        ffn1_w_gate[layer], ffn1_w_up[layer], ffn1_w_down[layer])
        h = h + _token_mixing(h, positions, mix_norm[layer], w_in[layer],
                              sgu_ln_gain[layer], sgu_ln_bias[layer], sgu_w_s[layer], sgu_b_s[layer],
                              attn_out_gain[layer], sgu_out_gain[layer], w_out[layer])
        h = h + _memory_cross_attention(h, mem, mem_q_norm[layer], mem_kv_norm[layer],
                                        mem_w_q[layer], mem_w_k[layer], mem_w_v[layer], mem_w_o[layer])
        h = h + FFN_RES_SCALE * _swiglu(_rmsnorm(h, ffn2_norm[layer]),
                                        ffn2_w_gate[layer], ffn2_w_up[layer], ffn2_w_down[layer])
    return _rmsnorm(h, final_norm)
```

```python
import functools

import jax
import jax.numpy as jnp
import numpy as np
from jax.experimental import pallas as pl
from jax.experimental.pallas import tpu as pltpu

F32 = jnp.float32
BF16 = jnp.bfloat16

EPS = 1e-6
ROPE_THETA = 10000.0
FFN_RES_SCALE = 0.5
HEAD_DIM = 64
BLOCK = 128
DILATED_BRANCHES = ((128, 1), (512, 4), (2048, 16))
SGU_CHUNK = 128
SGU_GROUP_DIM = 128
N_MEM_HEADS = 4
LANES = 128

VMEM_LIMIT = 56 * 1024 * 1024


def _params(n_axes, vmem=VMEM_LIMIT):
    return pltpu.CompilerParams(
        dimension_semantics=("arbitrary",) * n_axes, vmem_limit_bytes=vmem)


def _rms(x, g):
    return x * jax.lax.rsqrt(jnp.mean(x * x, axis=-1, keepdims=True) + EPS) * g


def _gelu(x):
    c = np.sqrt(2.0 / np.pi).astype(np.float32)
    return 0.5 * x * (1.0 + jnp.tanh(c * (x + 0.044715 * (x * x * x))))


def _rope_table_kernel(pos_ref, freq_ref, cos_ref, sin_ref):
    ang = pos_ref[...].astype(F32) * freq_ref[...]
    lane = jax.lax.broadcasted_iota(jnp.int32, ang.shape, 1)
    sign = jnp.where(lane % HEAD_DIM < HEAD_DIM // 2, -1.0, 1.0)
    cos_ref[...] = jnp.cos(ang)
    sin_ref[...] = jnp.sin(ang) * sign


def _rope_tables(positions, seq):
    tm = 1024
    half = HEAD_DIM // 2
    inv_freq = ROPE_THETA ** (-jnp.arange(0, HEAD_DIM, 2, dtype=F32) / HEAD_DIM)
    freq_lane = jnp.tile(inv_freq, LANES // half).reshape(1, LANES)
    pos = positions.reshape(seq, 1)
    return pl.pallas_call(
        _rope_table_kernel,
        grid=(seq // tm,),
        in_specs=[pl.BlockSpec((tm, 1), lambda i: (i, 0)),
                  pl.BlockSpec((1, LANES), lambda i: (0, 0))],
        out_specs=[pl.BlockSpec((tm, LANES), lambda i: (i, 0)),
                   pl.BlockSpec((tm, LANES), lambda i: (i, 0))],
        out_shape=[jax.ShapeDtypeStruct((seq, LANES), F32)] * 2,
        compiler_params=_params(1),
        name="rope_tables",
    )(pos, freq_lane)


def _ffn_kernel(*refs, final):
    if final:
        x_ref, g_ref, wg_ref, wu_ref, wd_ref, fg_ref, o_ref, xn_ref = refs
    else:
        x_ref, g_ref, wg_ref, wu_ref, wd_ref, o_ref, xn_ref = refs
    f = pl.program_id(1)

    @pl.when(f == 0)
    def _():
        x = x_ref[...]
        xn_ref[...] = _rms(x, g_ref[...]).astype(BF16)
        o_ref[...] = x

    xn = xn_ref[...]
    gate = jnp.dot(xn, wg_ref[...], preferred_element_type=F32)
    up = jnp.dot(xn, wu_ref[...], preferred_element_type=F32)
    a = (gate * (FFN_RES_SCALE / (1.0 + jnp.exp(-gate)))) * up
    o_ref[...] += jnp.dot(a.astype(BF16), wd_ref[...], preferred_element_type=F32)

    if final:
        @pl.when(f == pl.num_programs(1) - 1)
        def _():
            o_ref[...] = _rms(o_ref[...], fg_ref[...])


def _ffn(x, g, wg, wu, wd, final_g=None, *, tm=512, tf=512):
    seq, d = x.shape
    dff = wg.shape[1]
    final = final_g is not None
    in_specs = [
        pl.BlockSpec((tm, d), lambda m, f: (m, 0)),
        pl.BlockSpec((1, d), lambda m, f: (0, 0)),
        pl.BlockSpec((d, tf), lambda m, f: (0, f)),
        pl.BlockSpec((d, tf), lambda m, f: (0, f)),
        pl.BlockSpec((tf, d), lambda m, f: (f, 0)),
    ]
    args = [x, g.reshape(1, d), wg, wu, wd]
    if final:
        in_specs.append(pl.BlockSpec((1, d), lambda m, f: (0, 0)))
        args.append(final_g.reshape(1, d))
    return pl.pallas_call(
        functools.partial(_ffn_kernel, final=final),
        grid=(seq // tm, dff // tf),
        in_specs=in_specs,
        out_specs=pl.BlockSpec((tm, d), lambda m, f: (m, 0)),
        out_shape=jax.ShapeDtypeStruct((seq, d), F32),
        scratch_shapes=[pltpu.VMEM((tm, d), BF16)],
        compiler_params=_params(2),
        name="ffn_final" if final else "ffn",
    )(*args)


def _proj_kernel(h_ref, g_ref, w_ref, cos_ref, sin_ref, lng_ref, lnb_ref, o_ref, xn_ref,
                 *, q_scale):
    n = pl.program_id(1)

    @pl.when(n == 0)
    def _():
        xn_ref[...] = _rms(h_ref[...], g_ref[...]).astype(BF16)

    y = jnp.dot(xn_ref[...], w_ref[...], preferred_element_type=F32)
    width = y.shape[1]

    def rope(scale):
        cos = cos_ref[...]
        sin = sin_ref[...]
        lane = jax.lax.broadcasted_iota(jnp.int32, cos.shape, 1)
        first = lane % HEAD_DIM < HEAD_DIM // 2
        for j in range(width // LANES):
            blk = y[:, j * LANES:(j + 1) * LANES]
            partner = jnp.where(first,
                                pltpu.roll(blk, LANES - HEAD_DIM // 2, 1),
                                pltpu.roll(blk, HEAD_DIM // 2, 1))
            out = blk * cos + partner * sin
            if scale != 1.0:
                out = out * scale
            o_ref[:, j * LANES:(j + 1) * LANES] = out.astype(o_ref.dtype)

    @pl.when(n == 0)
    def _():
        rope(q_scale)

    @pl.when(n == 1)
    def _():
        rope(1.0)

    @pl.when(n == 2)
    def _():
        o_ref[...] = y.astype(o_ref.dtype)

    @pl.when(n == 3)
    def _():
        o_ref[...] = _gelu(y).astype(o_ref.dtype)

    @pl.when(n == 4)
    def _():
        t = _gelu(y)
        mu = jnp.mean(t, axis=-1, keepdims=True)
        tc = t - mu
        t = tc * jax.lax.rsqrt(jnp.mean(tc * tc, axis=-1, keepdims=True) + EPS)
        o_ref[...] = (t * lng_ref[...] + lnb_ref[...]).astype(o_ref.dtype)


def _proj(h, g, w_in, cos, sin, ln_g, ln_b, *, width, tm=512):
    seq, d = h.shape
    d_in = w_in.shape[1]
    assert d_in == 5 * width
    return pl.pallas_call(
        functools.partial(_proj_kernel, q_scale=HEAD_DIM ** -0.5),
        grid=(seq // tm, d_in // width),
        in_specs=[
            pl.BlockSpec((tm, d), lambda m, n: (m, 0)),
            pl.BlockSpec((1, d), lambda m, n: (0, 0)),
            pl.BlockSpec((d, width), lambda m, n: (0, n)),
            pl.BlockSpec((tm, LANES), lambda m, n: (m, 0)),
            pl.BlockSpec((tm, LANES), lambda m, n: (m, 0)),
            pl.BlockSpec((1, width), lambda m, n: (0, 0)),
            pl.BlockSpec((1, width), lambda m, n: (0, 0)),
        ],
        out_specs=pl.BlockSpec((tm, width), lambda m, n: (m, n)),
        out_shape=jax.ShapeDtypeStruct((seq, d_in), BF16),
        scratch_shapes=[pltpu.VMEM((tm, d), BF16)],
        compiler_params=_params(2),
        name="proj",
    )(h, g.reshape(1, d), w_in, cos, sin, ln_g.reshape(1, width), ln_b.reshape(1, width))


def _attn_kernel(q_ref, kc_ref, kp_ref, vc_ref, vp_ref, o_ref, lse_ref, *, steps):
    n = pl.program_id(1)
    width = q_ref.shape[1]
    qi = jax.lax.broadcasted_iota(jnp.int32, (BLOCK, 2 * BLOCK), 0)
    kj = jax.lax.broadcasted_iota(jnp.int32, (BLOCK, 2 * BLOCK), 1)
    diff = qi + BLOCK - kj
    valid = (diff >= 0) & (diff <= steps) & ((n > 0) | (kj >= BLOCK))
    lane = jax.lax.broadcasted_iota(jnp.int32, (BLOCK, LANES), 1)
    low = lane < HEAD_DIM
    lse_tile = jnp.zeros((BLOCK, LANES), F32)
    for p in range(width // LANES):
        sl = slice(p * LANES, (p + 1) * LANES)
        q = q_ref[:, sl]
        k = jnp.concatenate([kp_ref[:, sl], kc_ref[:, sl]], axis=0)
        v = jnp.concatenate([vp_ref[:, sl], vc_ref[:, sl]], axis=0)
        outs = []
        for hh in range(2):
            qm = jnp.where(low if hh == 0 else ~low, q, jnp.zeros_like(q))
            s = jax.lax.dot_general(qm, k, (((1,), (1,)), ((), ())),
                                    preferred_element_type=F32)
            s = jnp.where(valid, s, -jnp.inf)
            m = jnp.max(s, axis=-1, keepdims=True)
            e = jnp.exp(s - m)
            l = jnp.sum(e, axis=-1, keepdims=True)
            o = jnp.dot(e.astype(BF16), v, preferred_element_type=F32)
            outs.append(o / l)
            lse_tile = jnp.where(lane == 2 * p + hh, m + jnp.log(l), lse_tile)
        o_ref[:, sl] = jnp.where(low, outs[0], outs[1])
    lse_ref[...] = lse_tile


def _attn_branch(proj, *, window, dilation, width):
    seq, d_in = proj.shape
    steps = window // dilation
    sub_len = seq // dilation
    n_blk = sub_len // BLOCK
    n_seg = d_in // width
    view = proj.reshape(sub_len, dilation * d_in)

    def seg(i, prev):
        if prev:
            return pl.BlockSpec((BLOCK, width),
                                lambda r, n: (jnp.maximum(n - 1, 0), r * n_seg + i))
        return pl.BlockSpec((BLOCK, width), lambda r, n: (n, r * n_seg + i))

    o, lse = pl.pallas_call(
        functools.partial(_attn_kernel, steps=steps),
        grid=(dilation, n_blk),
        in_specs=[seg(0, False), seg(1, False), seg(1, True), seg(2, False), seg(2, True)],
        out_specs=[pl.BlockSpec((BLOCK, width), lambda r, n: (n, r)),
                   pl.BlockSpec((BLOCK, LANES), lambda r, n: (n, r))],
        out_shape=[jax.ShapeDtypeStruct((sub_len, dilation * width), F32),
                   jax.ShapeDtypeStruct((sub_len, dilation * LANES), F32)],
        compiler_params=_params(2),
        name=f"attn_d{dilation}",
    )(view, view, view, view, view)
    return o.reshape(seq, width), lse.reshape(seq, LANES)


def _mix_kernel(h_ref, o1_ref, o2_ref, o3_ref, l1_ref, l2_ref, l3_ref, u_ref, gv_ref,
                ws_ref, bs_ref, ag_ref, sg_ref, wo_ref, out_ref, mixed_ref):
    tm = h_ref.shape[0]
    width = u_ref.shape[1]
    l1, l2, l3 = l1_ref[...], l2_ref[...], l3_ref[...]
    mx = jnp.maximum(jnp.maximum(l1, l2), l3)
    e1, e2, e3 = jnp.exp(l1 - mx), jnp.exp(l2 - mx), jnp.exp(l3 - mx)
    inv = 1.0 / (e1 + e2 + e3)
    wts = (e1 * inv, e2 * inv, e3 * inv)
    lane = jax.lax.broadcasted_iota(jnp.int32, (tm, LANES), 1)
    low = lane < HEAD_DIM
    o_refs = (o1_ref, o2_ref, o3_ref)
    for p in range(width // LANES):
        sl = slice(p * LANES, (p + 1) * LANES)
        acc = jnp.zeros((tm, LANES), F32)
        for w, o_ref in zip(wts, o_refs):
            wp = jnp.where(low, w[:, 2 * p:2 * p + 1], w[:, 2 * p + 1:2 * p + 2])
            acc = acc + wp * o_ref[:, sl]
        out_ref[:, sl] = acc
    attn = out_ref[:, :width]
    mixed_ref[:, :width] = _rms(attn, ag_ref[...]).astype(BF16)

    ci = jax.lax.broadcasted_iota(jnp.int32, (SGU_CHUNK, SGU_CHUNK), 0)
    cj = jax.lax.broadcasted_iota(jnp.int32, (SGU_CHUNK, SGU_CHUNK), 1)
    causal = cj <= ci
    for g in range(width // SGU_GROUP_DIM):
        gs = slice(g * SGU_GROUP_DIM, (g + 1) * SGU_GROUP_DIM)
        w = jnp.where(causal, ws_ref[g], 0.0).astype(BF16)
        b = bs_ref[:, g:g + 1]
        for c in range(tm // SGU_CHUNK):
            rs = slice(c * SGU_CHUNK, (c + 1) * SGU_CHUNK)
            sv = jnp.dot(w, gv_ref[rs, gs], preferred_element_type=F32) + b
            out_ref[rs, gs] = u_ref[rs, gs].astype(F32) * sv
    sgu = out_ref[:, :width]
    mixed_ref[:, width:] = _rms(sgu, sg_ref[...]).astype(BF16)

    out_ref[...] = h_ref[...] + jnp.dot(mixed_ref[...], wo_ref[...],
                                        preferred_element_type=F32)


def _mix(h, outs, lses, proj, w_s, b_s, attn_g, sgu_g, w_out, *, width, tm=512):
    seq, d = h.shape
    n_grp = w_s.shape[0]
    row = lambda m: (m, 0)
    const2 = lambda m: (0, 0)
    return pl.pallas_call(
        _mix_kernel,
        grid=(seq // tm,),
        in_specs=[pl.BlockSpec((tm, d), row)]
        + [pl.BlockSpec((tm, width), row)] * 3
        + [pl.BlockSpec((tm, LANES), row)] * 3
        + [pl.BlockSpec((tm, width), lambda m: (m, 3)),
           pl.BlockSpec((tm, width), lambda m: (m, 4)),
           pl.BlockSpec((n_grp, SGU_CHUNK, SGU_CHUNK), lambda m: (0, 0, 0)),
           pl.BlockSpec((SGU_CHUNK, n_grp), const2),
           pl.BlockSpec((1, width), const2),
           pl.BlockSpec((1, width), const2),
           pl.BlockSpec((2 * width, d), const2)],
        out_specs=pl.BlockSpec((tm, d), row),
        out_shape=jax.ShapeDtypeStruct((seq, d), F32),
        scratch_shapes=[pltpu.VMEM((tm, 2 * width), BF16)],
        compiler_params=_params(1),
        name="mix_out",
    )(h, *outs, *lses, proj, proj, w_s, b_s.T, attn_g.reshape(1, width),
      sgu_g.reshape(1, width), w_out)


def _mem_kv_kernel(mem_ref, g_ref, wk_ref, wv_ref, k_ref, v_ref):
    mk = _rms(mem_ref[...], g_ref[...]).astype(BF16)
    k_ref[...] = jnp.dot(mk, wk_ref[...], preferred_element_type=F32).astype(BF16)
    v_ref[...] = jnp.dot(mk, wv_ref[...], preferred_element_type=F32).astype(BF16)


def _mem_kv(mem, g, w_k, w_v, *, tn=512):
    n_mem, d = mem.shape
    return pl.pallas_call(
        _mem_kv_kernel,
        grid=(d // tn,),
        in_specs=[pl.BlockSpec((n_mem, d), lambda n: (0, 0)),
                  pl.BlockSpec((1, d), lambda n: (0, 0)),
                  pl.BlockSpec((d, tn), lambda n: (0, n)),
                  pl.BlockSpec((d, tn), lambda n: (0, n))],
        out_specs=[pl.BlockSpec((n_mem, tn), lambda n: (0, n))] * 2,
        out_shape=[jax.ShapeDtypeStruct((n_mem, d), BF16)] * 2,
        compiler_params=_params(1),
        name="mem_kv",
    )(mem, g.reshape(1, d), w_k, w_v)


def _cross_kernel(h_ref, g_ref, wq_ref, k_ref, v_ref, wo_ref, out_ref, q_ref, o_ref):
    d = h_ref.shape[1]
    hd = d // N_MEM_HEADS
    h = h_ref[...]
    xq = _rms(h, g_ref[...]).astype(BF16)
    q_ref[...] = jnp.dot(xq, wq_ref[...], preferred_element_type=F32).astype(BF16)
    for i in range(N_MEM_HEADS):
        sl = slice(i * hd, (i + 1) * hd)
        s = jax.lax.dot_general(q_ref[:, sl], k_ref[:, sl], (((1,), (1,)), ((), ())),
                                preferred_element_type=F32) * (hd ** -0.5)
        m = jnp.max(s, axis=-1, keepdims=True)
        e = jnp.exp(s - m)
        p = e / jnp.sum(e, axis=-1, keepdims=True)
        o_ref[:, sl] = jnp.dot(p.astype(BF16), v_ref[:, sl],
                               preferred_element_type=F32).astype(BF16)
    out_ref[...] = h + jnp.dot(o_ref[...], wo_ref[...], preferred_element_type=F32)


def _cross(h, g, w_q, k, v, w_o, *, tm=512):
    seq, d = h.shape
    n_mem = k.shape[0]
    const = lambda m: (0, 0)
    return pl.pallas_call(
        _cross_kernel,
        grid=(seq // tm,),
        in_specs=[pl.BlockSpec((tm, d), lambda m: (m, 0)),
                  pl.BlockSpec((1, d), const),
                  pl.BlockSpec((d, d), const),
                  pl.BlockSpec((n_mem, d), const),
                  pl.BlockSpec((n_mem, d), const),
                  pl.BlockSpec((d, d), const)],
        out_specs=pl.BlockSpec((tm, d), lambda m: (m, 0)),
        out_shape=jax.ShapeDtypeStruct((seq, d), F32),
        scratch_shapes=[pltpu.VMEM((tm, d), BF16), pltpu.VMEM((tm, d), BF16)],
        compiler_params=_params(1),
        name="mem_cross",
    )(h, g.reshape(1, d), w_q, k, v, w_o)


def kernel(x, mem, positions, ffn1_norm, ffn1_w_gate, ffn1_w_up, ffn1_w_down, mix_norm, w_in,
           sgu_ln_gain, sgu_ln_bias, sgu_w_s, sgu_b_s, attn_out_gain, sgu_out_gain, w_out,
           mem_q_norm, mem_kv_norm, mem_w_q, mem_w_k, mem_w_v, mem_w_o, ffn2_norm,
           ffn2_w_gate, ffn2_w_up, ffn2_w_down, final_norm):
    batch, seq, d = x.shape
    depth = ffn1_norm.shape[0]
    width = attn_out_gain.shape[1]
    assert batch == 1 and mem.shape[0] == 1
    bf = lambda w: w.astype(BF16)

    cos, sin = _rope_tables(positions, seq)
    h = x.reshape(seq, d)
    mem2 = mem.reshape(mem.shape[1], d)
    for layer in range(depth):
        is_last = layer == depth - 1
        h = _ffn(h, ffn1_norm[layer], bf(ffn1_w_gate[layer]), bf(ffn1_w_up[layer]),
                 bf(ffn1_w_down[layer]))
        proj = _proj(h, mix_norm[layer], bf(w_in[layer]), cos, sin,
                     sgu_ln_gain[layer], sgu_ln_bias[layer], width=width)
        outs, lses = [], []
        for window, dilation in DILATED_BRANCHES:
            o, lse = _attn_branch(proj, window=window, dilation=dilation, width=width)
            outs.append(o)
            lses.append(lse)
        h = _mix(h, outs, lses, proj, sgu_w_s[layer], sgu_b_s[layer], attn_out_gain[layer],
                 sgu_out_gain[layer], bf(w_out[layer]), width=width)
        k, v = _mem_kv(mem2, mem_kv_norm[layer], bf(mem_w_k[layer]), bf(mem_w_v[layer]))
        h = _cross(h, mem_q_norm[layer], bf(mem_w_q[layer]), k, v, bf(mem_w_o[layer]))
        h = _ffn(h, ffn2_norm[layer], bf(ffn2_w_gate[layer]), bf(ffn2_w_up[layer]),
                 bf(ffn2_w_down[layer]), final_norm if is_last else None)
    return h.reshape(batch, seq, d)
```

```python
import functools

import jax
import jax.numpy as jnp
import numpy as np
from jax.experimental import pallas as pl
from jax.experimental.pallas import tpu as pltpu

F32 = jnp.float32
BF16 = jnp.bfloat16

EPS = 1e-6
ROPE_THETA = 10000.0
FFN_RES_SCALE = 0.5
HEAD_DIM = 64
BLOCK = 128
DILATED_BRANCHES = ((128, 1), (512, 4), (2048, 16))
SGU_CHUNK = 128
SGU_GROUP_DIM = 128
N_MEM_HEADS = 4
LANES = 128
UNIT_UNROLL = 4

VMEM_LIMIT = 56 * 1024 * 1024


def _params(n_axes, vmem=VMEM_LIMIT):
    return pltpu.CompilerParams(
        dimension_semantics=("arbitrary",) * n_axes, vmem_limit_bytes=vmem)


def _rms(x, g):
    return x * jax.lax.rsqrt(jnp.mean(x * x, axis=-1, keepdims=True) + EPS) * g


def _gelu(x):
    c = np.sqrt(2.0 / np.pi).astype(np.float32)
    return 0.5 * x * (1.0 + jnp.tanh(c * (x + 0.044715 * (x * x * x))))


def _rope_table_kernel(pos_ref, freq_ref, cos_ref, sin_ref):
    ang = pos_ref[...].astype(F32) * freq_ref[...]
    lane = jax.lax.broadcasted_iota(jnp.int32, ang.shape, 1)
    sign = jnp.where(lane % HEAD_DIM < HEAD_DIM // 2, -1.0, 1.0)
    cos_ref[...] = jnp.cos(ang)
    sin_ref[...] = jnp.sin(ang) * sign


def _rope_tables(positions, seq):
    tm = 1024
    half = HEAD_DIM // 2
    inv_freq = ROPE_THETA ** (-jnp.arange(0, HEAD_DIM, 2, dtype=F32) / HEAD_DIM)
    freq_lane = jnp.tile(inv_freq, LANES // half).reshape(1, LANES)
    pos = positions.reshape(seq, 1)
    return pl.pallas_call(
        _rope_table_kernel,
        grid=(seq // tm,),
        in_specs=[pl.BlockSpec((tm, 1), lambda i: (i, 0)),
                  pl.BlockSpec((1, LANES), lambda i: (0, 0))],
        out_specs=[pl.BlockSpec((tm, LANES), lambda i: (i, 0)),
                   pl.BlockSpec((tm, LANES), lambda i: (i, 0))],
        out_shape=[jax.ShapeDtypeStruct((seq, LANES), F32)] * 2,
        compiler_params=_params(1),
        name="rope_tables",
    )(pos, freq_lane)


def _ffn_kernel(*refs, final):
    if final:
        x_ref, g_ref, wg_ref, wu_ref, wd_ref, fg_ref, o_ref, xn_ref = refs
    else:
        x_ref, g_ref, wg_ref, wu_ref, wd_ref, o_ref, xn_ref = refs
    f = pl.program_id(1)

    @pl.when(f == 0)
    def _():
        x = x_ref[...]
        xn_ref[...] = _rms(x, g_ref[...]).astype(BF16)
        o_ref[...] = x

    xn = xn_ref[...]
    gate = jnp.dot(xn, wg_ref[...], preferred_element_type=F32)
    up = jnp.dot(xn, wu_ref[...], preferred_element_type=F32)
    a = (gate * (FFN_RES_SCALE / (1.0 + jnp.exp(-gate)))) * up
    o_ref[...] += jnp.dot(a.astype(BF16), wd_ref[...], preferred_element_type=F32)

    if final:
        @pl.when(f == pl.num_programs(1) - 1)
        def _():
            o_ref[...] = _rms(o_ref[...], fg_ref[...])


def _ffn(x, g, wg, wu, wd, final_g=None, *, tm=512, tf=512):
    seq, d = x.shape
    dff = wg.shape[1]
    final = final_g is not None
    in_specs = [
        pl.BlockSpec((tm, d), lambda m, f: (m, 0)),
        pl.BlockSpec((1, d), lambda m, f: (0, 0)),
        pl.BlockSpec((d, tf), lambda m, f: (0, f)),
        pl.BlockSpec((d, tf), lambda m, f: (0, f)),
        pl.BlockSpec((tf, d), lambda m, f: (f, 0)),
    ]
    args = [x, g.reshape(1, d), wg, wu, wd]
    if final:
        in_specs.append(pl.BlockSpec((1, d), lambda m, f: (0, 0)))
        args.append(final_g.reshape(1, d))
    return pl.pallas_call(
        functools.partial(_ffn_kernel, final=final),
        grid=(seq // tm, dff // tf),
        in_specs=in_specs,
        out_specs=pl.BlockSpec((tm, d), lambda m, f: (m, 0)),
        out_shape=jax.ShapeDtypeStruct((seq, d), F32),
        scratch_shapes=[pltpu.VMEM((tm, d), BF16)],
        compiler_params=_params(2),
        name="ffn_final" if final else "ffn",
    )(*args)


def _proj_kernel(h_ref, g_ref, w_ref, cos_ref, sin_ref, lng_ref, lnb_ref, o_ref, xn_ref,
                 *, q_scale):
    n = pl.program_id(1)

    @pl.when(n == 0)
    def _():
        xn_ref[...] = _rms(h_ref[...], g_ref[...]).astype(BF16)

    y = jnp.dot(xn_ref[...], w_ref[...], preferred_element_type=F32)
    width = y.shape[1]

    def rope(scale):
        cos = cos_ref[...]
        sin = sin_ref[...]
        lane = jax.lax.broadcasted_iota(jnp.int32, cos.shape, 1)
        first = lane % HEAD_DIM < HEAD_DIM // 2
        for j in range(width // LANES):
            blk = y[:, j * LANES:(j + 1) * LANES]
            partner = jnp.where(first,
                                pltpu.roll(blk, LANES - HEAD_DIM // 2, 1),
                                pltpu.roll(blk, HEAD_DIM // 2, 1))
            out = blk * cos + partner * sin
            if scale != 1.0:
                out = out * scale
            o_ref[:, j * LANES:(j + 1) * LANES] = out.astype(o_ref.dtype)

    @pl.when(n == 0)
    def _():
        rope(q_scale)

    @pl.when(n == 1)
    def _():
        rope(1.0)

    @pl.when(n == 2)
    def _():
        o_ref[...] = y.astype(o_ref.dtype)

    @pl.when(n == 3)
    def _():
        o_ref[...] = _gelu(y).astype(o_ref.dtype)

    @pl.when(n == 4)
    def _():
        t = _gelu(y)
        mu = jnp.mean(t, axis=-1, keepdims=True)
        tc = t - mu
        t = tc * jax.lax.rsqrt(jnp.mean(tc * tc, axis=-1, keepdims=True) + EPS)
        o_ref[...] = (t * lng_ref[...] + lnb_ref[...]).astype(o_ref.dtype)


def _proj(h, g, w_in, cos, sin, ln_g, ln_b, *, width, tm=512):
    seq, d = h.shape
    d_in = w_in.shape[1]
    assert d_in == 5 * width
    return pl.pallas_call(
        functools.partial(_proj_kernel, q_scale=HEAD_DIM ** -0.5),
        grid=(seq // tm, d_in // width),
        in_specs=[
            pl.BlockSpec((tm, d), lambda m, n: (m, 0)),
            pl.BlockSpec((1, d), lambda m, n: (0, 0)),
            pl.BlockSpec((d, width), lambda m, n: (0, n)),
            pl.BlockSpec((tm, LANES), lambda m, n: (m, 0)),
            pl.BlockSpec((tm, LANES), lambda m, n: (m, 0)),
            pl.BlockSpec((1, width), lambda m, n: (0, 0)),
            pl.BlockSpec((1, width), lambda m, n: (0, 0)),
        ],
        out_specs=pl.BlockSpec((tm, width), lambda m, n: (m, n)),
        out_shape=jax.ShapeDtypeStruct((seq, d_in), BF16),
        scratch_shapes=[pltpu.VMEM((tm, d), BF16)],
        compiler_params=_params(2),
        name="proj",
    )(h, g.reshape(1, d), w_in, cos, sin, ln_g.reshape(1, width), ln_b.reshape(1, width))


def _attn_kernel(q_ref, kc_ref, kp_ref, vc_ref, vp_ref, o_ref,
                 stage, q4, k4, v4, kb, vb, m4, l4, acc4, m_s, l_s, acc_s, *, dil):
    t = pl.program_id(0)
    sup = q_ref.shape[0]
    sub = sup // dil
    n_units = sup // BLOCK

    def split(dst, n_rows):
        part = n_rows // dil
        for b in range(dil):
            dst[b * part:(b + 1) * part, :] = stage[pl.ds(b, part, stride=dil), :]

    stage[:sup] = kp_ref[...].astype(F32)
    stage[sup:] = kc_ref[...].astype(F32)
    split(k4, 2 * sup)
    stage[:sup] = vp_ref[...].astype(F32)
    stage[sup:] = vc_ref[...].astype(F32)
    split(v4, 2 * sup)
    stage[:sup] = q_ref[...].astype(F32)
    split(q4, sup)
    kb[:BLOCK] = kp_ref[sup - BLOCK:, :]
    kb[BLOCK:] = kc_ref[...]
    vb[:BLOCK] = vp_ref[sup - BLOCK:, :]
    vb[BLOCK:] = vc_ref[...]

    qi = jax.lax.broadcasted_iota(jnp.int32, (BLOCK, 2 * BLOCK), 0)
    kj = jax.lax.broadcasted_iota(jnp.int32, (BLOCK, 2 * BLOCK), 1)
    diff = qi + BLOCK - kj
    band = (diff >= 0) & (diff <= BLOCK)
    in_cur = kj >= BLOCK
    low = jax.lax.broadcasted_iota(jnp.int32, (BLOCK, LANES), 1) < HEAD_DIM

    def head_pair(q, k, v, has_prev):
        valid = band & (in_cur | has_prev)
        ms, ls, os = [], [], []
        for hh in range(2):
            qm = jnp.where(low if hh == 0 else ~low, q, jnp.zeros_like(q))
            s = jax.lax.dot_general(qm, k, (((1,), (1,)), ((), ())),
                                    preferred_element_type=F32)
            s = jnp.where(valid, s, -jnp.inf)
            m = jnp.max(s, axis=-1, keepdims=True)
            e = jnp.exp(s - m)
            ms.append(m)
            ls.append(jnp.sum(e, axis=-1, keepdims=True))
            os.append(jnp.dot(e.astype(BF16), v, preferred_element_type=F32))
        return (jnp.where(low, ms[0], ms[1]), jnp.where(low, ls[0], ls[1]),
                jnp.where(low, os[0], os[1]))

    def merge(refs, rows, stats):
        m_ref, l_ref, acc_ref = refs
        m_u, l_u, o_u = stats
        m_old = m_ref[rows, :]
        m_new = jnp.maximum(m_old, m_u)
        a_old = jnp.exp(m_old - m_new)
        a_u = jnp.exp(m_u - m_new)
        m_ref[rows, :] = m_new
        l_ref[rows, :] = a_old * l_ref[rows, :] + a_u * l_u
        acc_ref[rows, :] = a_old * acc_ref[rows, :] + a_u * o_u

    def mid_unit(u, carry):
        r = u // (sub // BLOCK)
        n = u % (sub // BLOCK)
        rows = pl.ds(pl.multiple_of(r * sub + n * BLOCK, BLOCK), BLOCK)
        kv_rows = pl.ds(pl.multiple_of(r * 2 * sub + sub + (n - 1) * BLOCK, BLOCK), 2 * BLOCK)
        m_u, l_u, o_u = head_pair(q4[rows, :].astype(BF16), k4[kv_rows, :].astype(BF16),
                                  v4[kv_rows, :].astype(BF16), (t > 0) | (n > 0))
        m4[rows, :] = m_u
        l4[rows, :] = l_u
        acc4[rows, :] = o_u
        return carry

    def wide_unit(r, carry):
        a = r // dil
        b = r % dil
        rows = pl.ds(b * sub + a, BLOCK, stride=dil)
        kv_rows = pl.ds(b * 2 * sub + a, 2 * BLOCK, stride=dil)
        stats = head_pair(q4[rows, :].astype(BF16), k4[kv_rows, :].astype(BF16),
                          v4[kv_rows, :].astype(BF16), t > 0)
        merge((m4, l4, acc4), rows, stats)
        return carry

    def near_unit(n, carry):
        rows = pl.ds(pl.multiple_of(n * BLOCK, BLOCK), BLOCK)
        kv_rows = pl.ds(pl.multiple_of(n * BLOCK, BLOCK), 2 * BLOCK)
        stats = head_pair(q_ref[rows, :], kb[kv_rows, :], vb[kv_rows, :], (t > 0) | (n > 0))
        merge((m_s, l_s, acc_s), rows, stats)
        return carry

    jax.lax.fori_loop(0, n_units, mid_unit, 0, unroll=UNIT_UNROLL)
    jax.lax.fori_loop(0, n_units, wide_unit, 0, unroll=UNIT_UNROLL)
    for b in range(dil):
        src = slice(b * sub, (b + 1) * sub)
        dst = pl.ds(b, sub, stride=dil)
        m_s[dst, :] = m4[src, :]
        l_s[dst, :] = l4[src, :]
        acc_s[dst, :] = acc4[src, :]
    jax.lax.fori_loop(0, n_units, near_unit, 0, unroll=UNIT_UNROLL)

    o_ref[...] = (acc_s[...] / l_s[...]).astype(o_ref.dtype)


def _attention(proj, *, width):
    seq, d_in = proj.shape
    dilations = tuple(d for _, d in DILATED_BRANCHES)
    steps = {w // d for w, d in DILATED_BRANCHES}
    assert steps == {BLOCK}, "every branch must span exactly one previous block"
    dil = dilations[1]
    assert dilations == (1, dil, dil * dil)
    sup = BLOCK * dil * dil
    seg = width // LANES
    cur = lambda i: pl.BlockSpec((sup, LANES), lambda t, p: (t, i * seg + p))
    prev = lambda i: pl.BlockSpec((sup, LANES),
                                  lambda t, p: (jnp.maximum(t - 1, 0), i * seg + p))
    f32_rows = lambda n: pltpu.VMEM((n, LANES), F32)
    return pl.pallas_call(
        functools.partial(_attn_kernel, dil=dil),
        grid=(seq // sup, seg),
        in_specs=[cur(0), cur(1), prev(1), cur(2), prev(2)],
        out_specs=pl.BlockSpec((sup, LANES), lambda t, p: (t, p)),
        out_shape=jax.ShapeDtypeStruct((seq, width), BF16),
        scratch_shapes=[f32_rows(2 * sup),
                        f32_rows(sup), f32_rows(2 * sup), f32_rows(2 * sup),
                        pltpu.VMEM((sup + BLOCK, LANES), BF16),
                        pltpu.VMEM((sup + BLOCK, LANES), BF16),
                        f32_rows(sup), f32_rows(sup), f32_rows(sup),
                        f32_rows(sup), f32_rows(sup), f32_rows(sup)],
        compiler_params=_params(2),
        name="dilated_attn",
    )(proj, proj, proj, proj, proj)


def _mix_kernel(h_ref, attn_ref, u_ref, gv_ref, ws_ref, bs_ref, ag_ref, sg_ref, wo_ref,
                out_ref, mixed_ref):
    tm = h_ref.shape[0]
    width = u_ref.shape[1]
    mixed_ref[:, :width] = _rms(attn_ref[...].astype(F32), ag_ref[...]).astype(BF16)

    ci = jax.lax.broadcasted_iota(jnp.int32, (SGU_CHUNK, SGU_CHUNK), 0)
    cj = jax.lax.broadcasted_iota(jnp.int32, (SGU_CHUNK, SGU_CHUNK), 1)
    causal = cj <= ci
    for g in range(width // SGU_GROUP_DIM):
        gs = slice(g * SGU_GROUP_DIM, (g + 1) * SGU_GROUP_DIM)
        w = jnp.where(causal, ws_ref[g], 0.0).astype(BF16)
        b = bs_ref[:, g:g + 1]
        for c in range(tm // SGU_CHUNK):
            rs = slice(c * SGU_CHUNK, (c + 1) * SGU_CHUNK)
            sv = jnp.dot(w, gv_ref[rs, gs], preferred_element_type=F32) + b
            out_ref[rs, gs] = u_ref[rs, gs].astype(F32) * sv
    sgu = out_ref[:, :width]
    mixed_ref[:, width:] = _rms(sgu, sg_ref[...]).astype(BF16)

    out_ref[...] = h_ref[...] + jnp.dot(mixed_ref[...], wo_ref[...],
                                        preferred_element_type=F32)


def _mix(h, attn, proj, w_s, b_s, attn_g, sgu_g, w_out, *, width, tm=512):
    seq, d = h.shape
    n_grp = w_s.shape[0]
    row = lambda m: (m, 0)
    const2 = lambda m: (0, 0)
    return pl.pallas_call(
        _mix_kernel,
        grid=(seq // tm,),
        in_specs=[pl.BlockSpec((tm, d), row),
                  pl.BlockSpec((tm, width), row)]
        + [pl.BlockSpec((tm, width), lambda m: (m, 3)),
           pl.BlockSpec((tm, width), lambda m: (m, 4)),
           pl.BlockSpec((n_grp, SGU_CHUNK, SGU_CHUNK), lambda m: (0, 0, 0)),
           pl.BlockSpec((SGU_CHUNK, n_grp), const2),
           pl.BlockSpec((1, width), const2),
           pl.BlockSpec((1, width), const2),
           pl.BlockSpec((2 * width, d), const2)],
        out_specs=pl.BlockSpec((tm, d), row),
        out_shape=jax.ShapeDtypeStruct((seq, d), F32),
        scratch_shapes=[pltpu.VMEM((tm, 2 * width), BF16)],
        compiler_params=_params(1),
        name="mix_out",
    )(h, attn, proj, proj, w_s, b_s.T, attn_g.reshape(1, width),
      sgu_g.reshape(1, width), w_out)


def _mem_kv_kernel(mem_ref, g_ref, wk_ref, wv_ref, k_ref, v_ref):
    mk = _rms(mem_ref[...], g_ref[...]).astype(BF16)
    k_ref[...] = jnp.dot(mk, wk_ref[...], preferred_element_type=F32).astype(BF16)
    v_ref[...] = jnp.dot(mk, wv_ref[...], preferred_element_type=F32).astype(BF16)


def _mem_kv(mem, g, w_k, w_v, *, tn=512):
    n_mem, d = mem.shape
    return pl.pallas_call(
        _mem_kv_kernel,
        grid=(d // tn,),
        in_specs=[pl.BlockSpec((n_mem, d), lambda n: (0, 0)),
                  pl.BlockSpec((1, d), lambda n: (0, 0)),
                  pl.BlockSpec((d, tn), lambda n: (0, n)),
                  pl.BlockSpec((d, tn), lambda n: (0, n))],
        out_specs=[pl.BlockSpec((n_mem, tn), lambda n: (0, n))] * 2,
        out_shape=[jax.ShapeDtypeStruct((n_mem, d), BF16)] * 2,
        compiler_params=_params(1),
        name="mem_kv",
    )(mem, g.reshape(1, d), w_k, w_v)


def _cross_kernel(h_ref, g_ref, wq_ref, k_ref, v_ref, wo_ref, out_ref, q_ref, o_ref):
    d = h_ref.shape[1]
    hd = d // N_MEM_HEADS
    h = h_ref[...]
    xq = _rms(h, g_ref[...]).astype(BF16)
    q_ref[...] = jnp.dot(xq, wq_ref[...], preferred_element_type=F32).astype(BF16)
    for i in range(N_MEM_HEADS):
        sl = slice(i * hd, (i + 1) * hd)
        s = jax.lax.dot_general(q_ref[:, sl], k_ref[:, sl], (((1,), (1,)), ((), ())),
                                preferred_element_type=F32) * (hd ** -0.5)
        m = jnp.max(s, axis=-1, keepdims=True)
        e = jnp.exp(s - m)
        p = e / jnp.sum(e, axis=-1, keepdims=True)
        o_ref[:, sl] = jnp.dot(p.astype(BF16), v_ref[:, sl],
                               preferred_element_type=F32).astype(BF16)
    out_ref[...] = h + jnp.dot(o_ref[...], wo_ref[...], preferred_element_type=F32)


def _cross(h, g, w_q, k, v, w_o, *, tm=512):
    seq, d = h.shape
    n_mem = k.shape[0]
    const = lambda m: (0, 0)
    return pl.pallas_call(
        _cross_kernel,
        grid=(seq // tm,),
        in_specs=[pl.BlockSpec((tm, d), lambda m: (m, 0)),
                  pl.BlockSpec((1, d), const),
                  pl.BlockSpec((d, d), const),
                  pl.BlockSpec((n_mem, d), const),
                  pl.BlockSpec((n_mem, d), const),
                  pl.BlockSpec((d, d), const)],
        out_specs=pl.BlockSpec((tm, d), lambda m: (m, 0)),
        out_shape=jax.ShapeDtypeStruct((seq, d), F32),
        scratch_shapes=[pltpu.VMEM((tm, d), BF16), pltpu.VMEM((tm, d), BF16)],
        compiler_params=_params(1),
        name="mem_cross",
    )(h, g.reshape(1, d), w_q, k, v, w_o)


def kernel(x, mem, positions, ffn1_norm, ffn1_w_gate, ffn1_w_up, ffn1_w_down, mix_norm, w_in,
           sgu_ln_gain, sgu_ln_bias, sgu_w_s, sgu_b_s, attn_out_gain, sgu_out_gain, w_out,
           mem_q_norm, mem_kv_norm, mem_w_q, mem_w_k, mem_w_v, mem_w_o, ffn2_norm,
           ffn2_w_gate, ffn2_w_up, ffn2_w_down, final_norm):
    batch, seq, d = x.shape
    depth = ffn1_norm.shape[0]
    width = attn_out_gain.shape[1]
    assert batch == 1 and mem.shape[0] == 1
    bf = lambda w: w.astype(BF16)

    cos, sin = _rope_tables(positions, seq)
    h = x.reshape(seq, d)
    mem2 = mem.reshape(mem.shape[1], d)
    for layer in range(depth):
        is_last = layer == depth - 1
        h = _ffn(h, ffn1_norm[layer], bf(ffn1_w_gate[layer]), bf(ffn1_w_up[layer]),
                 bf(ffn1_w_down[layer]))
        proj = _proj(h, mix_norm[layer], bf(w_in[layer]), cos, sin,
                     sgu_ln_gain[layer], sgu_ln_bias[layer], width=width)
        attn = _attention(proj, width=width)
        h = _mix(h, attn, proj, sgu_w_s[layer], sgu_b_s[layer], attn_out_gain[layer],
                 sgu_out_gain[layer], bf(w_out[layer]), width=width)
        k, v = _mem_kv(mem2, mem_kv_norm[layer], bf(mem_w_k[layer]), bf(mem_w_v[layer]))
        h = _cross(h, mem_q_norm[layer], bf(mem_w_q[layer]), k, v, bf(mem_w_o[layer]))
        h = _ffn(h, ffn2_norm[layer], bf(ffn2_w_gate[layer]), bf(ffn2_w_up[layer]),
                 bf(ffn2_w_down[layer]), final_norm if is_last else None)
    return h.reshape(batch, seq, d)
```

```python
import functools

import jax
import jax.numpy as jnp
import numpy as np
from jax.experimental import pallas as pl
from jax.experimental.pallas import tpu as pltpu

F32 = jnp.float32
BF16 = jnp.bfloat16

EPS = 1e-6
ROPE_THETA = 10000.0
FFN_RES_SCALE = 0.5
HEAD_DIM = 64
BLOCK = 128
DILATED_BRANCHES = ((128, 1), (512, 4), (2048, 16))
SGU_CHUNK = 128
SGU_GROUP_DIM = 128
N_MEM_HEADS = 4
LANES = 128
UNIT_UNROLL = 4
CAST_ROWS = 256

VMEM_LIMIT = 56 * 1024 * 1024


def _params(n_axes, vmem=VMEM_LIMIT):
    return pltpu.CompilerParams(
        dimension_semantics=("arbitrary",) * n_axes, vmem_limit_bytes=vmem)


def _resident(shape):
    return pl.BlockSpec(shape, lambda *_: (0,) * len(shape), pipeline_mode=pl.Buffered(1))


def _rms(x, g):
    return x * jax.lax.rsqrt(jnp.mean(x * x, axis=-1, keepdims=True) + EPS) * g


def _gelu(x):
    c = np.sqrt(2.0 / np.pi).astype(np.float32)
    return 0.5 * x * (1.0 + jnp.tanh(c * (x + 0.044715 * (x * x * x))))


def _cast_to_bf16(src_ref, dst_ref):
    rows = src_ref.shape[0]
    chunk = min(CAST_ROWS, rows)

    def body(i, carry):
        sl = pl.ds(pl.multiple_of(i * chunk, chunk), chunk)
        dst_ref[sl, :] = src_ref[sl, :].astype(BF16)
        return carry

    jax.lax.fori_loop(0, rows // chunk, body, 0)


def _rope_table_kernel(pos_ref, freq_ref, cos_ref, sin_ref):
    ang = pos_ref[...].astype(F32) * freq_ref[...]
    lane = jax.lax.broadcasted_iota(jnp.int32, ang.shape, 1)
    sign = jnp.where(lane % HEAD_DIM < HEAD_DIM // 2, -1.0, 1.0)
    cos_ref[...] = jnp.cos(ang)
    sin_ref[...] = jnp.sin(ang) * sign


def _rope_tables(positions, seq):
    tm = 1024
    half = HEAD_DIM // 2
    inv_freq = ROPE_THETA ** (-jnp.arange(0, HEAD_DIM, 2, dtype=F32) / HEAD_DIM)
    freq_lane = jnp.tile(inv_freq, LANES // half).reshape(1, LANES)
    pos = positions.reshape(seq, 1)
    return pl.pallas_call(
        _rope_table_kernel,
        grid=(seq // tm,),
        in_specs=[pl.BlockSpec((tm, 1), lambda i: (i, 0)),
                  pl.BlockSpec((1, LANES), lambda i: (0, 0))],
        out_specs=[pl.BlockSpec((tm, LANES), lambda i: (i, 0)),
                   pl.BlockSpec((tm, LANES), lambda i: (i, 0))],
        out_shape=[jax.ShapeDtypeStruct((seq, LANES), F32)] * 2,
        compiler_params=_params(1),
        name="rope_tables",
    )(pos, freq_lane)


def _norm_kernel(x_ref, g_ref, o_ref):
    o_ref[...] = _rms(x_ref[...], g_ref[...]).astype(BF16)


def _first_norm(x, g, *, tm=1024):
    seq, d = x.shape
    return pl.pallas_call(
        _norm_kernel,
        grid=(seq // tm,),
        in_specs=[pl.BlockSpec((tm, d), lambda m: (m, 0)),
                  pl.BlockSpec((1, d), lambda m: (0, 0))],
        out_specs=pl.BlockSpec((tm, d), lambda m: (m, 0)),
        out_shape=jax.ShapeDtypeStruct((seq, d), BF16),
        compiler_params=_params(1),
        name="first_norm",
    )(x, g.reshape(1, d))


def _ffn_up_kernel(xn_ref, wg_ref, wu_ref, wd_ref, a_ref, wdb_ref, wgb, wub):
    @pl.when(pl.program_id(1) == 0)
    def _():
        _cast_to_bf16(wg_ref, wgb)
        _cast_to_bf16(wu_ref, wub)
        _cast_to_bf16(wd_ref, wdb_ref)

    xn = xn_ref[...]
    gate = jnp.dot(xn, wgb[...], preferred_element_type=F32)
    up = jnp.dot(xn, wub[...], preferred_element_type=F32)
    a_ref[...] = ((gate * (FFN_RES_SCALE / (1.0 + jnp.exp(-gate)))) * up).astype(BF16)


def _ffn_up(xn, wg, wu, wd, *, tm=1024, tf=512):
    seq, d = xn.shape
    dff = wg.shape[1]
    return pl.pallas_call(
        _ffn_up_kernel,
        grid=(dff // tf, seq // tm),
        in_specs=[pl.BlockSpec((tm, d), lambda f, m: (m, 0)),
                  pl.BlockSpec((d, tf), lambda f, m: (0, f)),
                  pl.BlockSpec((d, tf), lambda f, m: (0, f)),
                  pl.BlockSpec((tf, d), lambda f, m: (f, 0))],
        out_specs=[pl.BlockSpec((tm, tf), lambda f, m: (m, f)),
                   pl.BlockSpec((tf, d), lambda f, m: (f, 0))],
        out_shape=[jax.ShapeDtypeStruct((seq, dff), BF16),
                   jax.ShapeDtypeStruct((dff, d), BF16)],
        scratch_shapes=[pltpu.VMEM((d, tf), BF16), pltpu.VMEM((d, tf), BF16)],
        compiler_params=_params(2),
        name="ffn_up",
    )(xn, wg, wu, wd)


def _ffn_down_kernel(a_ref, x_ref, wd_ref, g_ref, *out_refs, final):
    h = x_ref[...] + jnp.dot(a_ref[...], wd_ref[...], preferred_element_type=F32)
    if final:
        out_refs[0][...] = _rms(h, g_ref[...])
    else:
        out_refs[0][...] = h
        out_refs[1][...] = _rms(h, g_ref[...]).astype(BF16)


def _ffn_down(a, x, wd_bf, g, *, final, tm=256):
    seq, d = x.shape
    dff = a.shape[1]
    row = pl.BlockSpec((tm, d), lambda m: (m, 0))
    out_specs = [row] if final else [row, row]
    out_shape = [jax.ShapeDtypeStruct((seq, d), F32)]
    if not final:
        out_shape.append(jax.ShapeDtypeStruct((seq, d), BF16))
    return pl.pallas_call(
        functools.partial(_ffn_down_kernel, final=final),
        grid=(seq // tm,),
        in_specs=[pl.BlockSpec((tm, dff), lambda m: (m, 0)), row,
                  _resident((dff, d)), pl.BlockSpec((1, d), lambda m: (0, 0))],
        out_specs=out_specs,
        out_shape=out_shape,
        compiler_params=_params(1),
        name="ffn_down_final" if final else "ffn_down",
    )(a, x, wd_bf, g.reshape(1, d))


def _proj_kernel(xn_ref, w_ref, cos_ref, sin_ref, lng_ref, lnb_ref, o_ref, wb, *, q_scale):
    n = pl.program_id(0)

    @pl.when(pl.program_id(1) == 0)
    def _():
        _cast_to_bf16(w_ref, wb)

    y = jnp.dot(xn_ref[...], wb[...], preferred_element_type=F32)
    width = y.shape[1]

    def rope(scale):
        cos = cos_ref[...]
        sin = sin_ref[...]
        lane = jax.lax.broadcasted_iota(jnp.int32, cos.shape, 1)
        first = lane % HEAD_DIM < HEAD_DIM // 2
        for j in range(width // LANES):
            blk = y[:, j * LANES:(j + 1) * LANES]
            partner = jnp.where(first,
                                pltpu.roll(blk, LANES - HEAD_DIM // 2, 1),
                                pltpu.roll(blk, HEAD_DIM // 2, 1))
            out = blk * cos + partner * sin
            if scale != 1.0:
                out = out * scale
            o_ref[:, j * LANES:(j + 1) * LANES] = out.astype(o_ref.dtype)

    @pl.when(n == 0)
    def _():
        rope(q_scale)

    @pl.when(n == 1)
    def _():
        rope(1.0)

    @pl.when(n == 2)
    def _():
        o_ref[...] = y.astype(o_ref.dtype)

    @pl.when(n == 3)
    def _():
        o_ref[...] = _gelu(y).astype(o_ref.dtype)

    @pl.when(n == 4)
    def _():
        t = _gelu(y)
        mu = jnp.mean(t, axis=-1, keepdims=True)
        tc = t - mu
        t = tc * jax.lax.rsqrt(jnp.mean(tc * tc, axis=-1, keepdims=True) + EPS)
        o_ref[...] = (t * lng_ref[...] + lnb_ref[...]).astype(o_ref.dtype)


def _proj(xn, w_in, cos, sin, ln_g, ln_b, *, width, tm=1024):
    seq, d = xn.shape
    d_in = w_in.shape[1]
    assert d_in == 5 * width
    return pl.pallas_call(
        functools.partial(_proj_kernel, q_scale=HEAD_DIM ** -0.5),
        grid=(d_in // width, seq // tm),
        in_specs=[
            pl.BlockSpec((tm, d), lambda n, m: (m, 0)),
            pl.BlockSpec((d, width), lambda n, m: (0, n)),
            pl.BlockSpec((tm, LANES), lambda n, m: (m, 0)),
            pl.BlockSpec((tm, LANES), lambda n, m: (m, 0)),
            pl.BlockSpec((1, width), lambda n, m: (0, 0)),
            pl.BlockSpec((1, width), lambda n, m: (0, 0)),
        ],
        out_specs=pl.BlockSpec((tm, width), lambda n, m: (m, n)),
        out_shape=jax.ShapeDtypeStruct((seq, d_in), BF16),
        scratch_shapes=[pltpu.VMEM((d, width), BF16)],
        compiler_params=_params(2),
        name="proj",
    )(xn, w_in, cos, sin, ln_g.reshape(1, width), ln_b.reshape(1, width))


def _attn_kernel(q_ref, kc_ref, kp_ref, vc_ref, vp_ref, o_ref,
                 stage, q4, k4, v4, kb, vb, m4, l4, acc4, m_s, l_s, acc_s, *, dil):
    t = pl.program_id(0)
    sup = q_ref.shape[0]
    sub = sup // dil
    n_units = sup // BLOCK

    def split(dst, n_rows):
        part = n_rows // dil
        for b in range(dil):
            dst[b * part:(b + 1) * part, :] = stage[pl.ds(b, part, stride=dil), :]

    stage[:sup] = kp_ref[...].astype(F32)
    stage[sup:] = kc_ref[...].astype(F32)
    split(k4, 2 * sup)
    stage[:sup] = vp_ref[...].astype(F32)
    stage[sup:] = vc_ref[...].astype(F32)
    split(v4, 2 * sup)
    stage[:sup] = q_ref[...].astype(F32)
    split(q4, sup)
    kb[:BLOCK] = kp_ref[sup - BLOCK:, :]
    kb[BLOCK:] = kc_ref[...]
    vb[:BLOCK] = vp_ref[sup - BLOCK:, :]
    vb[BLOCK:] = vc_ref[...]

    qi = jax.lax.broadcasted_iota(jnp.int32, (BLOCK, 2 * BLOCK), 0)
    kj = jax.lax.broadcasted_iota(jnp.int32, (BLOCK, 2 * BLOCK), 1)
    diff = qi + BLOCK - kj
    band = (diff >= 0) & (diff <= BLOCK)
    in_cur = kj >= BLOCK
    low = jax.lax.broadcasted_iota(jnp.int32, (BLOCK, LANES), 1) < HEAD_DIM

    def head_pair(q, k, v, has_prev):
        valid = band & (in_cur | has_prev)
        ms, ls, os = [], [], []
        for hh in range(2):
            qm = jnp.where(low if hh == 0 else ~low, q, jnp.zeros_like(q))
            s = jax.lax.dot_general(qm, k, (((1,), (1,)), ((), ())),
                                    preferred_element_type=F32)
            s = jnp.where(valid, s, -jnp.inf)
            m = jnp.max(s, axis=-1, keepdims=True)
            e = jnp.exp(s - m)
            ms.append(m)
            ls.append(jnp.sum(e, axis=-1, keepdims=True))
            os.append(jnp.dot(e.astype(BF16), v, preferred_element_type=F32))
        return (jnp.where(low, ms[0], ms[1]), jnp.where(low, ls[0], ls[1]),
                jnp.where(low, os[0], os[1]))

    def merge(refs, rows, stats):
        m_ref, l_ref, acc_ref = refs
        m_u, l_u, o_u = stats
        m_old = m_ref[rows, :]
        m_new = jnp.maximum(m_old, m_u)
        a_old = jnp.exp(m_old - m_new)
        a_u = jnp.exp(m_u - m_new)
        m_ref[rows, :] = m_new
        l_ref[rows, :] = a_old * l_ref[rows, :] + a_u * l_u
        acc_ref[rows, :] = a_old * acc_ref[rows, :] + a_u * o_u

    def mid_unit(u, carry):
        r = u // (sub // BLOCK)
        n = u % (sub // BLOCK)
        rows = pl.ds(pl.multiple_of(r * sub + n * BLOCK, BLOCK), BLOCK)
        kv_rows = pl.ds(pl.multiple_of(r * 2 * sub + sub + (n - 1) * BLOCK, BLOCK), 2 * BLOCK)
        m_u, l_u, o_u = head_pair(q4[rows, :].astype(BF16), k4[kv_rows, :].astype(BF16),
                                  v4[kv_rows, :].astype(BF16), (t > 0) | (n > 0))
        m4[rows, :] = m_u
        l4[rows, :] = l_u
        acc4[rows, :] = o_u
        return carry

    def wide_unit(r, carry):
        a = r // dil
        b = r % dil
        rows = pl.ds(b * sub + a, BLOCK, stride=dil)
        kv_rows = pl.ds(b * 2 * sub + a, 2 * BLOCK, stride=dil)
        stats = head_pair(q4[rows, :].astype(BF16), k4[kv_rows, :].astype(BF16),
                          v4[kv_rows, :].astype(BF16), t > 0)
        merge((m4, l4, acc4), rows, stats)
        return carry

    def near_unit(n, carry):
        rows = pl.ds(pl.multiple_of(n * BLOCK, BLOCK), BLOCK)
        kv_rows = pl.ds(pl.multiple_of(n * BLOCK, BLOCK), 2 * BLOCK)
        stats = head_pair(q_ref[rows, :], kb[kv_rows, :], vb[kv_rows, :], (t > 0) | (n > 0))
        merge((m_s, l_s, acc_s), rows, stats)
        return carry

    jax.lax.fori_loop(0, n_units, mid_unit, 0, unroll=UNIT_UNROLL)
    jax.lax.fori_loop(0, n_units, wide_unit, 0, unroll=UNIT_UNROLL)
    for b in range(dil):
        src = slice(b * sub, (b + 1) * sub)
        dst = pl.ds(b, sub, stride=dil)
        m_s[dst, :] = m4[src, :]
        l_s[dst, :] = l4[src, :]
        acc_s[dst, :] = acc4[src, :]
    jax.lax.fori_loop(0, n_units, near_unit, 0, unroll=UNIT_UNROLL)

    o_ref[...] = (acc_s[...] / l_s[...]).astype(o_ref.dtype)


def _attention(proj, *, width):
    seq, d_in = proj.shape
    dilations = tuple(d for _, d in DILATED_BRANCHES)
    steps = {w // d for w, d in DILATED_BRANCHES}
    assert steps == {BLOCK}, "every branch must span exactly one previous block"
    dil = dilations[1]
    assert dilations == (1, dil, dil * dil)
    sup = BLOCK * dil * dil
    seg = width // LANES
    cur = lambda i: pl.BlockSpec((sup, LANES), lambda t, p: (t, i * seg + p))
    prev = lambda i: pl.BlockSpec((sup, LANES),
                                  lambda t, p: (jnp.maximum(t - 1, 0), i * seg + p))
    f32_rows = lambda n: pltpu.VMEM((n, LANES), F32)
    return pl.pallas_call(
        functools.partial(_attn_kernel, dil=dil),
        grid=(seq // sup, seg),
        in_specs=[cur(0), cur(1), prev(1), cur(2), prev(2)],
        out_specs=pl.BlockSpec((sup, LANES), lambda t, p: (t, p)),
        out_shape=jax.ShapeDtypeStruct((seq, width), BF16),
        scratch_shapes=[f32_rows(2 * sup),
                        f32_rows(sup), f32_rows(2 * sup), f32_rows(2 * sup),
                        pltpu.VMEM((sup + BLOCK, LANES), BF16),
                        pltpu.VMEM((sup + BLOCK, LANES), BF16),
                        f32_rows(sup), f32_rows(sup), f32_rows(sup),
                        f32_rows(sup), f32_rows(sup), f32_rows(sup)],
        compiler_params=_params(2),
        name="dilated_attn",
    )(proj, proj, proj, proj, proj)


def _mix_kernel(h_ref, attn_ref, u_ref, gv_ref, ws_ref, bs_ref, ag_ref, sg_ref, wo_ref, ng_ref,
                out_ref, xn_ref, mixed_ref, wob):
    tm = h_ref.shape[0]
    width = u_ref.shape[1]

    @pl.when(pl.program_id(0) == 0)
    def _():
        _cast_to_bf16(wo_ref, wob)

    mixed_ref[:, :width] = _rms(attn_ref[...].astype(F32), ag_ref[...]).astype(BF16)

    ci = jax.lax.broadcasted_iota(jnp.int32, (SGU_CHUNK, SGU_CHUNK), 0)
    cj = jax.lax.broadcasted_iota(jnp.int32, (SGU_CHUNK, SGU_CHUNK), 1)
    causal = cj <= ci
    for g in range(width // SGU_GROUP_DIM):
        gs = slice(g * SGU_GROUP_DIM, (g + 1) * SGU_GROUP_DIM)
        w = jnp.where(causal, ws_ref[g], 0.0).astype(BF16)
        b = bs_ref[:, g:g + 1]
        for c in range(tm // SGU_CHUNK):
            rs = slice(c * SGU_CHUNK, (c + 1) * SGU_CHUNK)
            sv = jnp.dot(w, gv_ref[rs, gs], preferred_element_type=F32) + b
            out_ref[rs, gs] = u_ref[rs, gs].astype(F32) * sv
    sgu = out_ref[:, :width]
    mixed_ref[:, width:] = _rms(sgu, sg_ref[...]).astype(BF16)

    h = h_ref[...] + jnp.dot(mixed_ref[...], wob[...], preferred_element_type=F32)
    out_ref[...] = h
    xn_ref[...] = _rms(h, ng_ref[...]).astype(BF16)


def _mix(h, attn, proj, w_s, b_s, attn_g, sgu_g, w_out, next_g, *, width, tm=256):
    seq, d = h.shape
    n_grp = w_s.shape[0]
    row = lambda m: (m, 0)
    const2 = lambda m: (0, 0)
    return pl.pallas_call(
        _mix_kernel,
        grid=(seq // tm,),
        in_specs=[pl.BlockSpec((tm, d), row),
                  pl.BlockSpec((tm, width), row),
                  pl.BlockSpec((tm, width), lambda m: (m, 3)),
                  pl.BlockSpec((tm, width), lambda m: (m, 4)),
                  pl.BlockSpec((n_grp, SGU_CHUNK, SGU_CHUNK), lambda m: (0, 0, 0)),
                  pl.BlockSpec((SGU_CHUNK, n_grp), const2),
                  pl.BlockSpec((1, width), const2),
                  pl.BlockSpec((1, width), const2),
                  _resident((2 * width, d)),
                  pl.BlockSpec((1, d), const2)],
        out_specs=[pl.BlockSpec((tm, d), row), pl.BlockSpec((tm, d), row)],
        out_shape=[jax.ShapeDtypeStruct((seq, d), F32), jax.ShapeDtypeStruct((seq, d), BF16)],
        scratch_shapes=[pltpu.VMEM((tm, 2 * width), BF16), pltpu.VMEM((2 * width, d), BF16)],
        compiler_params=_params(1),
        name="mix_out",
    )(h, attn, proj, proj, w_s, b_s.T, attn_g.reshape(1, width), sgu_g.reshape(1, width),
      w_out, next_g.reshape(1, d))


def _mem_kv_kernel(mem_ref, g_ref, wk_ref, wv_ref, k_ref, v_ref):
    mk = _rms(mem_ref[...], g_ref[...]).astype(BF16)
    k_ref[...] = jnp.dot(mk, wk_ref[...].astype(BF16), preferred_element_type=F32).astype(BF16)
    v_ref[...] = jnp.dot(mk, wv_ref[...].astype(BF16), preferred_element_type=F32).astype(BF16)


def _mem_kv(mem, g, w_k, w_v, *, tn=512):
    n_mem, d = mem.shape
    return pl.pallas_call(
        _mem_kv_kernel,
        grid=(d // tn,),
        in_specs=[pl.BlockSpec((n_mem, d), lambda n: (0, 0)),
                  pl.BlockSpec((1, d), lambda n: (0, 0)),
                  pl.BlockSpec((d, tn), lambda n: (0, n)),
                  pl.BlockSpec((d, tn), lambda n: (0, n))],
        out_specs=[pl.BlockSpec((n_mem, tn), lambda n: (0, n))] * 2,
        out_shape=[jax.ShapeDtypeStruct((n_mem, d), BF16)] * 2,
        compiler_params=_params(1),
        name="mem_kv",
    )(mem, g.reshape(1, d), w_k, w_v)


def _mem_attn_kernel(xq_ref, wq_ref, k_ref, v_ref, o_ref, wqb, q_ref):
    @pl.when(pl.program_id(0) == 0)
    def _():
        _cast_to_bf16(wq_ref, wqb)

    hd = xq_ref.shape[1] // N_MEM_HEADS
    q_ref[...] = jnp.dot(xq_ref[...], wqb[...], preferred_element_type=F32).astype(BF16)
    for i in range(N_MEM_HEADS):
        sl = slice(i * hd, (i + 1) * hd)
        s = jax.lax.dot_general(q_ref[:, sl], k_ref[:, sl], (((1,), (1,)), ((), ())),
                                preferred_element_type=F32) * (hd ** -0.5)
        m = jnp.max(s, axis=-1, keepdims=True)
        e = jnp.exp(s - m)
        p = e / jnp.sum(e, axis=-1, keepdims=True)
        o_ref[:, sl] = jnp.dot(p.astype(BF16), v_ref[:, sl],
                               preferred_element_type=F32).astype(BF16)


def _mem_attn(xq, w_q, k, v, *, tm=512):
    seq, d = xq.shape
    n_mem = k.shape[0]
    const = lambda m: (0, 0)
    return pl.pallas_call(
        _mem_attn_kernel,
        grid=(seq // tm,),
        in_specs=[pl.BlockSpec((tm, d), lambda m: (m, 0)),
                  _resident((d, d)),
                  pl.BlockSpec((n_mem, d), const),
                  pl.BlockSpec((n_mem, d), const)],
        out_specs=pl.BlockSpec((tm, d), lambda m: (m, 0)),
        out_shape=jax.ShapeDtypeStruct((seq, d), BF16),
        scratch_shapes=[pltpu.VMEM((d, d), BF16), pltpu.VMEM((tm, d), BF16)],
        compiler_params=_params(1),
        name="mem_attn",
    )(xq, w_q, k, v)


def _mem_out_kernel(h_ref, o_ref, wo_ref, ng_ref, out_ref, xn_ref, wob):
    @pl.when(pl.program_id(0) == 0)
    def _():
        _cast_to_bf16(wo_ref, wob)

    h = h_ref[...] + jnp.dot(o_ref[...], wob[...], preferred_element_type=F32)
    out_ref[...] = h
    xn_ref[...] = _rms(h, ng_ref[...]).astype(BF16)


def _mem_out(h, o, w_o, next_g, *, tm=512):
    seq, d = h.shape
    row = pl.BlockSpec((tm, d), lambda m: (m, 0))
    return pl.pallas_call(
        _mem_out_kernel,
        grid=(seq // tm,),
        in_specs=[row, row, _resident((d, d)), pl.BlockSpec((1, d), lambda m: (0, 0))],
        out_specs=[row, row],
        out_shape=[jax.ShapeDtypeStruct((seq, d), F32), jax.ShapeDtypeStruct((seq, d), BF16)],
        scratch_shapes=[pltpu.VMEM((d, d), BF16)],
        compiler_params=_params(1),
        name="mem_out",
    )(h, o, w_o, next_g.reshape(1, d))


def kernel(x, mem, positions, ffn1_norm, ffn1_w_gate, ffn1_w_up, ffn1_w_down, mix_norm, w_in,
           sgu_ln_gain, sgu_ln_bias, sgu_w_s, sgu_b_s, attn_out_gain, sgu_out_gain, w_out,
           mem_q_norm, mem_kv_norm, mem_w_q, mem_w_k, mem_w_v, mem_w_o, ffn2_norm,
           ffn2_w_gate, ffn2_w_up, ffn2_w_down, final_norm):
    batch, seq, d = x.shape
    depth = ffn1_norm.shape[0]
    width = attn_out_gain.shape[1]
    assert batch == 1 and mem.shape[0] == 1 and depth == 1
    layer = 0

    cos, sin = _rope_tables(positions, seq)
    h = x.reshape(seq, d)
    mem2 = mem.reshape(mem.shape[1], d)

    xn = _first_norm(h, ffn1_norm[layer])
    a, wd_bf = _ffn_up(xn, ffn1_w_gate[layer], ffn1_w_up[layer], ffn1_w_down[layer])
    h, xn = _ffn_down(a, h, wd_bf, mix_norm[layer], final=False)
    proj = _proj(xn, w_in[layer], cos, sin, sgu_ln_gain[layer], sgu_ln_bias[layer], width=width)
    attn = _attention(proj, width=width)
    h, xn = _mix(h, attn, proj, sgu_w_s[layer], sgu_b_s[layer], attn_out_gain[layer],
                 sgu_out_gain[layer], w_out[layer], mem_q_norm[layer], width=width)
    k, v = _mem_kv(mem2, mem_kv_norm[layer], mem_w_k[layer], mem_w_v[layer])
    o = _mem_attn(xn, mem_w_q[layer], k, v)
    h, xn = _mem_out(h, o, mem_w_o[layer], ffn2_norm[layer])
    a, wd_bf = _ffn_up(xn, ffn2_w_gate[layer], ffn2_w_up[layer], ffn2_w_down[layer])
    out = _ffn_down(a, h, wd_bf, final_norm, final=True)[0]
    return out.reshape(batch, seq, d)
```

```python
import functools

import jax
import jax.numpy as jnp
import numpy as np
from jax.experimental import pallas as pl
from jax.experimental.pallas import tpu as pltpu

F32 = jnp.float32
BF16 = jnp.bfloat16

EPS = 1e-6
ROPE_THETA = 10000.0
FFN_RES_SCALE = 0.5
HEAD_DIM = 64
BLOCK = 128
DILATED_BRANCHES = ((128, 1), (512, 4), (2048, 16))
SGU_CHUNK = 128
SGU_GROUP_DIM = 128
N_MEM_HEADS = 4
LANES = 128
MXU_COLS = 256
UNIT_UNROLL = 16
CAST_ROWS = 256

VMEM_LIMIT = 56 * 1024 * 1024


def _params(n_axes, vmem=VMEM_LIMIT):
    return pltpu.CompilerParams(
        dimension_semantics=("arbitrary",) * n_axes, vmem_limit_bytes=vmem)


def _resident(shape):
    return pl.BlockSpec(shape, lambda *_: (0,) * len(shape), pipeline_mode=pl.Buffered(1))


def _rms(x, g):
    return x * jax.lax.rsqrt(jnp.mean(x * x, axis=-1, keepdims=True) + EPS) * g


def _gelu(x):
    c = np.sqrt(2.0 / np.pi).astype(np.float32)
    return 0.5 * x * (1.0 + jnp.tanh(c * (x + 0.044715 * (x * x * x))))


def _cast_to_bf16(src_ref, dst_ref):
    rows = src_ref.shape[0]
    chunk = min(CAST_ROWS, rows)

    def body(i, carry):
        sl = pl.ds(pl.multiple_of(i * chunk, chunk), chunk)
        dst_ref[sl, :] = src_ref[sl, :].astype(BF16)
        return carry

    jax.lax.fori_loop(0, rows // chunk, body, 0)


def _rope_table_kernel(pos_ref, freq_ref, cos_ref, sin_ref):
    ang = pos_ref[...].astype(F32) * freq_ref[...]
    lane = jax.lax.broadcasted_iota(jnp.int32, ang.shape, 1)
    sign = jnp.where(lane % HEAD_DIM < HEAD_DIM // 2, -1.0, 1.0)
    cos_ref[...] = jnp.cos(ang)
    sin_ref[...] = jnp.sin(ang) * sign


def _rope_tables(positions, seq):
    tm = 1024
    half = HEAD_DIM // 2
    inv_freq = ROPE_THETA ** (-jnp.arange(0, HEAD_DIM, 2, dtype=F32) / HEAD_DIM)
    freq_lane = jnp.tile(inv_freq, LANES // half).reshape(1, LANES)
    pos = positions.reshape(seq, 1)
    return pl.pallas_call(
        _rope_table_kernel,
        grid=(seq // tm,),
        in_specs=[pl.BlockSpec((tm, 1), lambda i: (i, 0)),
                  pl.BlockSpec((1, LANES), lambda i: (0, 0))],
        out_specs=[pl.BlockSpec((tm, LANES), lambda i: (i, 0)),
                   pl.BlockSpec((tm, LANES), lambda i: (i, 0))],
        out_shape=[jax.ShapeDtypeStruct((seq, LANES), F32)] * 2,
        compiler_params=_params(1),
        name="rope_tables",
    )(pos, freq_lane)


def _norm_kernel(x_ref, g_ref, o_ref):
    o_ref[...] = _rms(x_ref[...], g_ref[...]).astype(BF16)


def _first_norm(x, g, *, tm=1024):
    seq, d = x.shape
    return pl.pallas_call(
        _norm_kernel,
        grid=(seq // tm,),
        in_specs=[pl.BlockSpec((tm, d), lambda m: (m, 0)),
                  pl.BlockSpec((1, d), lambda m: (0, 0))],
        out_specs=pl.BlockSpec((tm, d), lambda m: (m, 0)),
        out_shape=jax.ShapeDtypeStruct((seq, d), BF16),
        compiler_params=_params(1),
        name="first_norm",
    )(x, g.reshape(1, d))


def _ffn_up_kernel(xn_ref, wg_ref, wu_ref, wd_ref, a_ref, wdb_ref, wgb, wub):
    @pl.when(pl.program_id(1) == 0)
    def _():
        _cast_to_bf16(wg_ref, wgb)
        _cast_to_bf16(wu_ref, wub)
        _cast_to_bf16(wd_ref, wdb_ref)

    xn = xn_ref[...]
    gate = jnp.dot(xn, wgb[...], preferred_element_type=F32)
    up = jnp.dot(xn, wub[...], preferred_element_type=F32)
    a_ref[...] = ((gate * (FFN_RES_SCALE / (1.0 + jnp.exp(-gate)))) * up).astype(BF16)


def _ffn_up(xn, wg, wu, wd, *, tm=1024, tf=512):
    seq, d = xn.shape
    dff = wg.shape[1]
    return pl.pallas_call(
        _ffn_up_kernel,
        grid=(dff // tf, seq // tm),
        in_specs=[pl.BlockSpec((tm, d), lambda f, m: (m, 0)),
                  pl.BlockSpec((d, tf), lambda f, m: (0, f)),
                  pl.BlockSpec((d, tf), lambda f, m: (0, f)),
                  pl.BlockSpec((tf, d), lambda f, m: (f, 0))],
        out_specs=[pl.BlockSpec((tm, tf), lambda f, m: (m, f)),
                   pl.BlockSpec((tf, d), lambda f, m: (f, 0))],
        out_shape=[jax.ShapeDtypeStruct((seq, dff), BF16),
                   jax.ShapeDtypeStruct((dff, d), BF16)],
        scratch_shapes=[pltpu.VMEM((d, tf), BF16), pltpu.VMEM((d, tf), BF16)],
        compiler_params=_params(2),
        name="ffn_up",
    )(xn, wg, wu, wd)


def _ffn_down_kernel(a_ref, x_ref, wd_ref, g_ref, *out_refs, final):
    h = x_ref[...] + jnp.dot(a_ref[...], wd_ref[...], preferred_element_type=F32)
    if final:
        out_refs[0][...] = _rms(h, g_ref[...])
    else:
        out_refs[0][...] = h
        out_refs[1][...] = _rms(h, g_ref[...]).astype(BF16)


def _ffn_down(a, x, wd_bf, g, *, final, tm=256):
    seq, d = x.shape
    dff = a.shape[1]
    row = pl.BlockSpec((tm, d), lambda m: (m, 0))
    out_specs = [row] if final else [row, row]
    out_shape = [jax.ShapeDtypeStruct((seq, d), F32)]
    if not final:
        out_shape.append(jax.ShapeDtypeStruct((seq, d), BF16))
    return pl.pallas_call(
        functools.partial(_ffn_down_kernel, final=final),
        grid=(seq // tm,),
        in_specs=[pl.BlockSpec((tm, dff), lambda m: (m, 0)), row,
                  _resident((dff, d)), pl.BlockSpec((1, d), lambda m: (0, 0))],
        out_specs=out_specs,
        out_shape=out_shape,
        compiler_params=_params(1),
        name="ffn_down_final" if final else "ffn_down",
    )(a, x, wd_bf, g.reshape(1, d))


def _proj_kernel(xn_ref, w_ref, cos_ref, sin_ref, lng_ref, lnb_ref, o_ref, wb, t_ref, *, q_scale):
    n = pl.program_id(0)
    width = o_ref.shape[1]

    @pl.when(pl.program_id(1) == 0)
    def _():
        _cast_to_bf16(w_ref, wb)

    half = xn_ref.shape[0] // 2

    def chunked(epilogue):
        for j in range(width // MXU_COLS):
            c0 = j * MXU_COLS
            for r0 in (0, half):
                rows = slice(r0, r0 + half)
                y = jnp.dot(xn_ref[rows, :], wb[:, c0:c0 + MXU_COLS],
                            preferred_element_type=F32)
                epilogue(y, rows, c0)

    def rope(scale):
        def epilogue(y, rows, c0):
            cos = cos_ref[rows, :]
            sin = sin_ref[rows, :]
            lane = jax.lax.broadcasted_iota(jnp.int32, cos.shape, 1)
            first = lane % HEAD_DIM < HEAD_DIM // 2
            for j in range(MXU_COLS // LANES):
                blk = y[:, j * LANES:(j + 1) * LANES]
                partner = jnp.where(first,
                                    pltpu.roll(blk, LANES - HEAD_DIM // 2, 1),
                                    pltpu.roll(blk, HEAD_DIM // 2, 1))
                out = blk * cos + partner * sin
                if scale != 1.0:
                    out = out * scale
                o_ref[rows, c0 + j * LANES:c0 + (j + 1) * LANES] = out.astype(o_ref.dtype)
        return epilogue

    def store(fn):
        def epilogue(y, rows, c0):
            o_ref[rows, c0:c0 + MXU_COLS] = fn(y).astype(o_ref.dtype)
        return epilogue

    @pl.when(n == 0)
    def _():
        chunked(rope(q_scale))

    @pl.when(n == 1)
    def _():
        chunked(rope(1.0))

    @pl.when(n == 2)
    def _():
        chunked(store(lambda y: y))

    @pl.when(n == 3)
    def _():
        chunked(store(_gelu))

    @pl.when(n == 4)
    def _():
        sums = {0: [], half: []}

        def epilogue(y, rows, c0):
            t = _gelu(y)
            t_ref[rows, c0:c0 + MXU_COLS] = t
            sums[rows.start].append(jnp.sum(t, axis=-1, keepdims=True))

        chunked(epilogue)
        for r0 in (0, half):
            rows = slice(r0, r0 + half)
            mu = sum(sums[r0]) * (1.0 / width)
            tc = t_ref[rows, :] - mu
            t = tc * jax.lax.rsqrt(jnp.mean(tc * tc, axis=-1, keepdims=True) + EPS)
            o_ref[rows, :] = (t * lng_ref[...] + lnb_ref[...]).astype(o_ref.dtype)


def _proj(xn, w_in, cos, sin, ln_g, ln_b, *, width, tm=1024):
    seq, d = xn.shape
    d_in = w_in.shape[1]
    assert d_in == 5 * width
    return pl.pallas_call(
        functools.partial(_proj_kernel, q_scale=HEAD_DIM ** -0.5),
        grid=(d_in // width, seq // tm),
        in_specs=[
            pl.BlockSpec((tm, d), lambda n, m: (m, 0)),
            pl.BlockSpec((d, width), lambda n, m: (0, n)),
            pl.BlockSpec((tm, LANES), lambda n, m: (m, 0)),
            pl.BlockSpec((tm, LANES), lambda n, m: (m, 0)),
            pl.BlockSpec((1, width), lambda n, m: (0, 0)),
            pl.BlockSpec((1, width), lambda n, m: (0, 0)),
        ],
        out_specs=pl.BlockSpec((tm, width), lambda n, m: (m, n)),
        out_shape=jax.ShapeDtypeStruct((seq, d_in), BF16),
        scratch_shapes=[pltpu.VMEM((d, width), BF16), pltpu.VMEM((tm, width), F32)],
        compiler_params=_params(2),
        name="proj",
    )(xn, w_in, cos, sin, ln_g.reshape(1, width), ln_b.reshape(1, width))


def _attn_kernel(q_ref, kc_ref, kp_ref, vc_ref, vp_ref, o_ref,
                 stage, q4, k4, v4, kb, vb, m4, l4, acc4, m_s, l_s, acc_s, *, dil):
    t = pl.program_id(0)
    sup = q_ref.shape[0]
    sub = sup // dil
    n_units = sup // BLOCK

    def split(dst, n_rows):
        part = n_rows // dil
        for b in range(dil):
            dst[b * part:(b + 1) * part, :] = stage[pl.ds(b, part, stride=dil), :]

    stage[:sup] = kp_ref[...].astype(F32)
    stage[sup:] = kc_ref[...].astype(F32)
    split(k4, 2 * sup)
    stage[:sup] = vp_ref[...].astype(F32)
    stage[sup:] = vc_ref[...].astype(F32)
    split(v4, 2 * sup)
    stage[:sup] = q_ref[...].astype(F32)
    split(q4, sup)
    kb[:BLOCK] = kp_ref[sup - BLOCK:, :]
    kb[BLOCK:] = kc_ref[...]
    vb[:BLOCK] = vp_ref[sup - BLOCK:, :]
    vb[BLOCK:] = vc_ref[...]

    qi = jax.lax.broadcasted_iota(jnp.int32, (BLOCK, 2 * BLOCK), 0)
    kj = jax.lax.broadcasted_iota(jnp.int32, (BLOCK, 2 * BLOCK), 1)
    diff = qi + BLOCK - kj
    band = (diff >= 0) & (diff <= BLOCK)
    in_cur = kj >= BLOCK
    low = jax.lax.broadcasted_iota(jnp.int32, (BLOCK, LANES), 1) < HEAD_DIM

    def head_pair(q, k, v, has_prev):
        valid = band & (in_cur | has_prev)
        ms, ls, os = [], [], []
        for hh in range(2):
            qm = jnp.where(low if hh == 0 else ~low, q, jnp.zeros_like(q))
            s = jax.lax.dot_general(qm, k, (((1,), (1,)), ((), ())),
                                    preferred_element_type=F32)
            s = jnp.where(valid, s, -jnp.inf)
            m = jnp.max(s, axis=-1, keepdims=True)
            e = jnp.exp(s - m)
            ms.append(m)
            ls.append(jnp.sum(e, axis=-1, keepdims=True))
            os.append(jnp.dot(e.astype(BF16), v, preferred_element_type=F32))
        return (jnp.where(low, ms[0], ms[1]), jnp.where(low, ls[0], ls[1]),
                jnp.where(low, os[0], os[1]))

    def merge(refs, rows, stats):
        m_ref, l_ref, acc_ref = refs
        m_u, l_u, o_u = stats
        m_old = m_ref[rows, :]
        m_new = jnp.maximum(m_old, m_u)
        a_old = jnp.exp(m_old - m_new)
        a_u = jnp.exp(m_u - m_new)
        m_ref[rows, :] = m_new
        l_ref[rows, :] = a_old * l_ref[rows, :] + a_u * l_u
        acc_ref[rows, :] = a_old * acc_ref[rows, :] + a_u * o_u

    def mid_unit(u, carry):
        r = u // (sub // BLOCK)
        n = u % (sub // BLOCK)
        rows = pl.ds(pl.multiple_of(r * sub + n * BLOCK, BLOCK), BLOCK)
        kv_rows = pl.ds(pl.multiple_of(r * 2 * sub + sub + (n - 1) * BLOCK, BLOCK), 2 * BLOCK)
        m_u, l_u, o_u = head_pair(q4[rows, :].astype(BF16), k4[kv_rows, :].astype(BF16),
                                  v4[kv_rows, :].astype(BF16), (t > 0) | (n > 0))
        m4[rows, :] = m_u
        l4[rows, :] = l_u
        acc4[rows, :] = o_u
        return carry

    def wide_unit(r, carry):
        a = r // dil
        b = r % dil
        rows = pl.ds(b * sub + a, BLOCK, stride=dil)
        kv_rows = pl.ds(b * 2 * sub + a, 2 * BLOCK, stride=dil)
        stats = head_pair(q4[rows, :].astype(BF16), k4[kv_rows, :].astype(BF16),
                          v4[kv_rows, :].astype(BF16), t > 0)
        merge((m4, l4, acc4), rows, stats)
        return carry

    def near_unit(n, carry):
        rows = pl.ds(pl.multiple_of(n * BLOCK, BLOCK), BLOCK)
        kv_rows = pl.ds(pl.multiple_of(n * BLOCK, BLOCK), 2 * BLOCK)
        stats = head_pair(q_ref[rows, :], kb[kv_rows, :], vb[kv_rows, :], (t > 0) | (n > 0))
        merge((m_s, l_s, acc_s), rows, stats)
        return carry

    jax.lax.fori_loop(0, n_units, mid_unit, 0, unroll=UNIT_UNROLL)
    jax.lax.fori_loop(0, n_units, wide_unit, 0, unroll=UNIT_UNROLL)
    for b in range(dil):
        src = slice(b * sub, (b + 1) * sub)
        dst = pl.ds(b, sub, stride=dil)
        m_s[dst, :] = m4[src, :]
        l_s[dst, :] = l4[src, :]
        acc_s[dst, :] = acc4[src, :]
    jax.lax.fori_loop(0, n_units, near_unit, 0, unroll=UNIT_UNROLL)

    o_ref[...] = (acc_s[...] / l_s[...]).astype(o_ref.dtype)


def _attention(proj, *, width):
    seq, d_in = proj.shape
    dilations = tuple(d for _, d in DILATED_BRANCHES)
    steps = {w // d for w, d in DILATED_BRANCHES}
    assert steps == {BLOCK}, "every branch must span exactly one previous block"
    dil = dilations[1]
    assert dilations == (1, dil, dil * dil)
    sup = BLOCK * dil * dil
    seg = width // LANES
    cur = lambda i: pl.BlockSpec((sup, LANES), lambda t, p: (t, i * seg + p))
    prev = lambda i: pl.BlockSpec((sup, LANES),
                                  lambda t, p: (jnp.maximum(t - 1, 0), i * seg + p))
    f32_rows = lambda n: pltpu.VMEM((n, LANES), F32)
    return pl.pallas_call(
        functools.partial(_attn_kernel, dil=dil),
        grid=(seq // sup, seg),
        in_specs=[cur(0), cur(1), prev(1), cur(2), prev(2)],
        out_specs=pl.BlockSpec((sup, LANES), lambda t, p: (t, p)),
        out_shape=jax.ShapeDtypeStruct((seq, width), BF16),
        scratch_shapes=[f32_rows(2 * sup),
                        f32_rows(sup), f32_rows(2 * sup), f32_rows(2 * sup),
                        pltpu.VMEM((sup + BLOCK, LANES), BF16),
                        pltpu.VMEM((sup + BLOCK, LANES), BF16),
                        f32_rows(sup), f32_rows(sup), f32_rows(sup),
                        f32_rows(sup), f32_rows(sup), f32_rows(sup)],
        compiler_params=_params(2),
        name="dilated_attn",
    )(proj, proj, proj, proj, proj)


def _mix_kernel(h_ref, attn_ref, u_ref, gv_ref, ws_ref, bs_ref, ag_ref, sg_ref, wo_ref, ng_ref,
                out_ref, xn_ref, mixed_ref, wob):
    tm = h_ref.shape[0]
    width = u_ref.shape[1]

    @pl.when(pl.program_id(0) == 0)
    def _():
        _cast_to_bf16(wo_ref, wob)

    mixed_ref[:, :width] = _rms(attn_ref[...].astype(F32), ag_ref[...]).astype(BF16)

    ci = jax.lax.broadcasted_iota(jnp.int32, (SGU_CHUNK, SGU_CHUNK), 0)
    cj = jax.lax.broadcasted_iota(jnp.int32, (SGU_CHUNK, SGU_CHUNK), 1)
    causal = cj <= ci
    for g in range(width // SGU_GROUP_DIM):
        gs = slice(g * SGU_GROUP_DIM, (g + 1) * SGU_GROUP_DIM)
        w = jnp.where(causal, ws_ref[g], 0.0).astype(BF16)
        b = bs_ref[:, g:g + 1]
        for c in range(tm // SGU_CHUNK):
            rs = slice(c * SGU_CHUNK, (c + 1) * SGU_CHUNK)
            sv = jnp.dot(w, gv_ref[rs, gs], preferred_element_type=F32) + b
            out_ref[rs, gs] = u_ref[rs, gs].astype(F32) * sv
    sgu = out_ref[:, :width]
    mixed_ref[:, width:] = _rms(sgu, sg_ref[...]).astype(BF16)

    h = h_ref[...] + jnp.dot(mixed_ref[...], wob[...], preferred_element_type=F32)
    out_ref[...] = h
    xn_ref[...] = _rms(h, ng_ref[...]).astype(BF16)


def _mix(h, attn, proj, w_s, b_s, attn_g, sgu_g, w_out, next_g, *, width, tm=256):
    seq, d = h.shape
    n_grp = w_s.shape[0]
    row = lambda m: (m, 0)
    const2 = lambda m: (0, 0)
    return pl.pallas_call(
        _mix_kernel,
        grid=(seq // tm,),
        in_specs=[pl.BlockSpec((tm, d), row),
                  pl.BlockSpec((tm, width), row),
                  pl.BlockSpec((tm, width), lambda m: (m, 3)),
                  pl.BlockSpec((tm, width), lambda m: (m, 4)),
                  pl.BlockSpec((n_grp, SGU_CHUNK, SGU_CHUNK), lambda m: (0, 0, 0)),
                  pl.BlockSpec((SGU_CHUNK, n_grp), const2),
                  pl.BlockSpec((1, width), const2),
                  pl.BlockSpec((1, width), const2),
                  _resident((2 * width, d)),
                  pl.BlockSpec((1, d), const2)],
        out_specs=[pl.BlockSpec((tm, d), row), pl.BlockSpec((tm, d), row)],
        out_shape=[jax.ShapeDtypeStruct((seq, d), F32), jax.ShapeDtypeStruct((seq, d), BF16)],
        scratch_shapes=[pltpu.VMEM((tm, 2 * width), BF16), pltpu.VMEM((2 * width, d), BF16)],
        compiler_params=_params(1),
        name="mix_out",
    )(h, attn, proj, proj, w_s, b_s.T, attn_g.reshape(1, width), sgu_g.reshape(1, width),
      w_out, next_g.reshape(1, d))


def _mem_kv_kernel(mem_ref, g_ref, wk_ref, wv_ref, k_ref, v_ref):
    mk = _rms(mem_ref[...], g_ref[...]).astype(BF16)
    k_ref[...] = jnp.dot(mk, wk_ref[...].astype(BF16), preferred_element_type=F32).astype(BF16)
    v_ref[...] = jnp.dot(mk, wv_ref[...].astype(BF16), preferred_element_type=F32).astype(BF16)


def _mem_kv(mem, g, w_k, w_v, *, tn=512):
    n_mem, d = mem.shape
    return pl.pallas_call(
        _mem_kv_kernel,
        grid=(d // tn,),
        in_specs=[pl.BlockSpec((n_mem, d), lambda n: (0, 0)),
                  pl.BlockSpec((1, d), lambda n: (0, 0)),
                  pl.BlockSpec((d, tn), lambda n: (0, n)),
                  pl.BlockSpec((d, tn), lambda n: (0, n))],
        out_specs=[pl.BlockSpec((n_mem, tn), lambda n: (0, n))] * 2,
        out_shape=[jax.ShapeDtypeStruct((n_mem, d), BF16)] * 2,
        compiler_params=_params(1),
        name="mem_kv",
    )(mem, g.reshape(1, d), w_k, w_v)


def _mem_attn_kernel(xq_ref, wq_ref, k_ref, v_ref, o_ref, wqb, q_ref):
    @pl.when(pl.program_id(0) == 0)
    def _():
        _cast_to_bf16(wq_ref, wqb)

    hd = xq_ref.shape[1] // N_MEM_HEADS
    q_ref[...] = jnp.dot(xq_ref[...], wqb[...], preferred_element_type=F32).astype(BF16)
    for i in range(N_MEM_HEADS):
        sl = slice(i * hd, (i + 1) * hd)
        s = jax.lax.dot_general(q_ref[:, sl], k_ref[:, sl], (((1,), (1,)), ((), ())),
                                preferred_element_type=F32) * (hd ** -0.5)
        m = jnp.max(s, axis=-1, keepdims=True)
        e = jnp.exp(s - m)
        p = e / jnp.sum(e, axis=-1, keepdims=True)
        o_ref[:, sl] = jnp.dot(p.astype(BF16), v_ref[:, sl],
                               preferred_element_type=F32).astype(BF16)


def _mem_attn(xq, w_q, k, v, *, tm=512):
    seq, d = xq.shape
    n_mem = k.shape[0]
    const = lambda m: (0, 0)
    return pl.pallas_call(
        _mem_attn_kernel,
        grid=(seq // tm,),
        in_specs=[pl.BlockSpec((tm, d), lambda m: (m, 0)),
                  _resident((d, d)),
                  pl.BlockSpec((n_mem, d), const),
                  pl.BlockSpec((n_mem, d), const)],
        out_specs=pl.BlockSpec((tm, d), lambda m: (m, 0)),
        out_shape=jax.ShapeDtypeStruct((seq, d), BF16),
        scratch_shapes=[pltpu.VMEM((d, d), BF16), pltpu.VMEM((tm, d), BF16)],
        compiler_params=_params(1),
        name="mem_attn",
    )(xq, w_q, k, v)


def _mem_out_kernel(h_ref, o_ref, wo_ref, ng_ref, out_ref, xn_ref, wob):
    @pl.when(pl.program_id(0) == 0)
    def _():
        _cast_to_bf16(wo_ref, wob)

    h = h_ref[...] + jnp.dot(o_ref[...], wob[...], preferred_element_type=F32)
    out_ref[...] = h
    xn_ref[...] = _rms(h, ng_ref[...]).astype(BF16)


def _mem_out(h, o, w_o, next_g, *, tm=512):
    seq, d = h.shape
    row = pl.BlockSpec((tm, d), lambda m: (m, 0))
    return pl.pallas_call(
        _mem_out_kernel,
        grid=(seq // tm,),
        in_specs=[row, row, _resident((d, d)), pl.BlockSpec((1, d), lambda m: (0, 0))],
        out_specs=[row, row],
        out_shape=[jax.ShapeDtypeStruct((seq, d), F32), jax.ShapeDtypeStruct((seq, d), BF16)],
        scratch_shapes=[pltpu.VMEM((d, d), BF16)],
        compiler_params=_params(1),
        name="mem_out",
    )(h, o, w_o, next_g.reshape(1, d))


def kernel(x, mem, positions, ffn1_norm, ffn1_w_gate, ffn1_w_up, ffn1_w_down, mix_norm, w_in,
           sgu_ln_gain, sgu_ln_bias, sgu_w_s, sgu_b_s, attn_out_gain, sgu_out_gain, w_out,
           mem_q_norm, mem_kv_norm, mem_w_q, mem_w_k, mem_w_v, mem_w_o, ffn2_norm,
           ffn2_w_gate, ffn2_w_up, ffn2_w_down, final_norm):
    batch, seq, d = x.shape
    depth = ffn1_norm.shape[0]
    width = attn_out_gain.shape[1]
    assert batch == 1 and mem.shape[0] == 1 and depth == 1
    layer = 0

    cos, sin = _rope_tables(positions, seq)
    h = x.reshape(seq, d)
    mem2 = mem.reshape(mem.shape[1], d)

    xn = _first_norm(h, ffn1_norm[layer])
    a, wd_bf = _ffn_up(xn, ffn1_w_gate[layer], ffn1_w_up[layer], ffn1_w_down[layer])
    h, xn = _ffn_down(a, h, wd_bf, mix_norm[layer], final=False)
    proj = _proj(xn, w_in[layer], cos, sin, sgu_ln_gain[layer], sgu_ln_bias[layer], width=width)
    attn = _attention(proj, width=width)
    h, xn = _mix(h, attn, proj, sgu_w_s[layer], sgu_b_s[layer], attn_out_gain[layer],
                 sgu_out_gain[layer], w_out[layer], mem_q_norm[layer], width=width)
    k, v = _mem_kv(mem2, mem_kv_norm[layer], mem_w_k[layer], mem_w_v[layer])
    o = _mem_attn(xn, mem_w_q[layer], k, v)
    h, xn = _mem_out(h, o, mem_w_o[layer], ffn2_norm[layer])
    a, wd_bf = _ffn_up(xn, ffn2_w_gate[layer], ffn2_w_up[layer], ffn2_w_down[layer])
    out = _ffn_down(a, h, wd_bf, final_norm, final=True)[0]
    return out.reshape(batch, seq, d)
```

```python
import functools

import jax
import jax.numpy as jnp
import numpy as np
from jax.experimental import pallas as pl
from jax.experimental.pallas import tpu as pltpu

F32 = jnp.float32
BF16 = jnp.bfloat16

EPS = 1e-6
ROPE_THETA = 10000.0
FFN_RES_SCALE = 0.5
HEAD_DIM = 64
BLOCK = 128
DILATED_BRANCHES = ((128, 1), (512, 4), (2048, 16))
SGU_CHUNK = 128
SGU_GROUP_DIM = 128
N_MEM_HEADS = 4
LANES = 128
MXU_COLS = 256
CAST_ROWS = 256

VMEM_LIMIT = 56 * 1024 * 1024


def _params(n_axes, vmem=VMEM_LIMIT):
    return pltpu.CompilerParams(
        dimension_semantics=("arbitrary",) * n_axes, vmem_limit_bytes=vmem)


def _resident(shape):
    return pl.BlockSpec(shape, lambda *_: (0,) * len(shape), pipeline_mode=pl.Buffered(1))


def _rms(x, g):
    return x * jax.lax.rsqrt(jnp.mean(x * x, axis=-1, keepdims=True) + EPS) * g


def _gelu(x):
    c = np.sqrt(2.0 / np.pi).astype(np.float32)
    return 0.5 * x * (1.0 + jnp.tanh(c * (x + 0.044715 * (x * x * x))))


def _cast_to_bf16(src_ref, dst_ref):
    rows = src_ref.shape[0]
    chunk = min(CAST_ROWS, rows)

    def body(i, carry):
        sl = pl.ds(pl.multiple_of(i * chunk, chunk), chunk)
        dst_ref[sl, :] = src_ref[sl, :].astype(BF16)
        return carry

    jax.lax.fori_loop(0, rows // chunk, body, 0)


def _prep_kernel(pos_ref, freq_ref, x_ref, g_ref, cos_ref, sin_ref, xn_ref):
    ang = pos_ref[...].astype(F32) * freq_ref[...]
    lane = jax.lax.broadcasted_iota(jnp.int32, ang.shape, 1)
    sign = jnp.where(lane % HEAD_DIM < HEAD_DIM // 2, -1.0, 1.0)
    cos_ref[...] = jnp.cos(ang)
    sin_ref[...] = jnp.sin(ang) * sign
    xn_ref[...] = _rms(x_ref[...], g_ref[...]).astype(BF16)


def _prep(positions, x, g, *, tm=1024):
    seq, d = x.shape
    half = HEAD_DIM // 2
    inv_freq = ROPE_THETA ** (-jnp.arange(0, HEAD_DIM, 2, dtype=F32) / HEAD_DIM)
    freq_lane = jnp.tile(inv_freq, LANES // half).reshape(1, LANES)
    pos = positions.reshape(seq, 1)
    row = lambda w: pl.BlockSpec((tm, w), lambda i: (i, 0))
    const = lambda w: pl.BlockSpec((1, w), lambda i: (0, 0))
    return pl.pallas_call(
        _prep_kernel,
        grid=(seq // tm,),
        in_specs=[row(1), const(LANES), row(d), const(d)],
        out_specs=[row(LANES), row(LANES), row(d)],
        out_shape=[jax.ShapeDtypeStruct((seq, LANES), F32)] * 2
        + [jax.ShapeDtypeStruct((seq, d), BF16)],
        compiler_params=_params(1),
        name="prep",
    )(pos, freq_lane, x, g.reshape(1, d))


def _ffn_up_kernel(xn_ref, wg_ref, wu_ref, wd_ref, a_ref, wdb_ref, wgb, wub):
    @pl.when(pl.program_id(1) == 0)
    def _():
        _cast_to_bf16(wg_ref, wgb)
        _cast_to_bf16(wu_ref, wub)
        _cast_to_bf16(wd_ref, wdb_ref)

    xn = xn_ref[...]
    gate = jnp.dot(xn, wgb[...], preferred_element_type=F32)
    up = jnp.dot(xn, wub[...], preferred_element_type=F32)
    a_ref[...] = ((gate * (FFN_RES_SCALE / (1.0 + jnp.exp(-gate)))) * up).astype(BF16)


def _ffn_up(xn, wg, wu, wd, *, tm=1024, tf=512):
    seq, d = xn.shape
    dff = wg.shape[1]
    return pl.pallas_call(
        _ffn_up_kernel,
        grid=(dff // tf, seq // tm),
        in_specs=[pl.BlockSpec((tm, d), lambda f, m: (m, 0)),
                  pl.BlockSpec((d, tf), lambda f, m: (0, f)),
                  pl.BlockSpec((d, tf), lambda f, m: (0, f)),
                  pl.BlockSpec((tf, d), lambda f, m: (f, 0))],
        out_specs=[pl.BlockSpec((tm, tf), lambda f, m: (m, f)),
                   pl.BlockSpec((tf, d), lambda f, m: (f, 0))],
        out_shape=[jax.ShapeDtypeStruct((seq, dff), BF16),
                   jax.ShapeDtypeStruct((dff, d), BF16)],
        scratch_shapes=[pltpu.VMEM((d, tf), BF16), pltpu.VMEM((d, tf), BF16)],
        compiler_params=_params(2),
        name="ffn_up",
    )(xn, wg, wu, wd)


def _ffn_down_kernel(*refs, final, n_cast):
    a_ref, x_ref, wd_ref, g_ref = refs[:4]
    cast_in = refs[4:4 + n_cast]
    out_refs = refs[4 + n_cast:]
    h = x_ref[...] + jnp.dot(a_ref[...], wd_ref[...], preferred_element_type=F32)
    if final:
        out_refs[0][...] = _rms(h, g_ref[...])
    else:
        out_refs[0][...] = h
        out_refs[1][...] = _rms(h, g_ref[...]).astype(BF16)
        for src, dst in zip(cast_in, out_refs[2:]):
            dst[...] = src[...].astype(BF16)


def _ffn_down(a, x, wd_bf, g, cast_weights=(), *, final, tm=256):
    seq, d = x.shape
    dff = a.shape[1]
    steps = seq // tm
    row = pl.BlockSpec((tm, d), lambda m: (m, 0))
    out_specs = [row] if final else [row, row]
    out_shape = [jax.ShapeDtypeStruct((seq, d), F32)]
    if not final:
        out_shape.append(jax.ShapeDtypeStruct((seq, d), BF16))
    cast_specs = []
    for w in cast_weights:
        assert not final and w.shape[0] % steps == 0
        spec = pl.BlockSpec((w.shape[0] // steps, w.shape[1]), lambda m: (m, 0))
        cast_specs.append(spec)
        out_specs.append(spec)
        out_shape.append(jax.ShapeDtypeStruct(w.shape, BF16))
    return pl.pallas_call(
        functools.partial(_ffn_down_kernel, final=final, n_cast=len(cast_weights)),
        grid=(steps,),
        in_specs=[pl.BlockSpec((tm, dff), lambda m: (m, 0)), row,
                  _resident((dff, d)), pl.BlockSpec((1, d), lambda m: (0, 0))] + cast_specs,
        out_specs=out_specs,
        out_shape=out_shape,
        compiler_params=_params(1),
        name="ffn_down_final" if final else "ffn_down",
    )(a, x, wd_bf, g.reshape(1, d), *cast_weights)


def _proj_kernel(xn_ref, w_ref, cos_ref, sin_ref, lng_ref, lnb_ref, o_ref, wb, t_ref, *, q_scale):
    n = pl.program_id(0)
    width = o_ref.shape[1]

    @pl.when(pl.program_id(1) == 0)
    def _():
        _cast_to_bf16(w_ref, wb)

    half = xn_ref.shape[0] // 2

    def chunked(epilogue):
        for j in range(width // MXU_COLS):
            c0 = j * MXU_COLS
            for r0 in (0, half):
                rows = slice(r0, r0 + half)
                y = jnp.dot(xn_ref[rows, :], wb[:, c0:c0 + MXU_COLS],
                            preferred_element_type=F32)
                epilogue(y, rows, c0)

    def rope(scale):
        def epilogue(y, rows, c0):
            cos = cos_ref[rows, :]
            sin = sin_ref[rows, :]
            lane = jax.lax.broadcasted_iota(jnp.int32, cos.shape, 1)
            first = lane % HEAD_DIM < HEAD_DIM // 2
            for j in range(MXU_COLS // LANES):
                blk = y[:, j * LANES:(j + 1) * LANES]
                partner = jnp.where(first,
                                    pltpu.roll(blk, LANES - HEAD_DIM // 2, 1),
                                    pltpu.roll(blk, HEAD_DIM // 2, 1))
                out = blk * cos + partner * sin
                if scale != 1.0:
                    out = out * scale
                o_ref[rows, c0 + j * LANES:c0 + (j + 1) * LANES] = out.astype(o_ref.dtype)
        return epilogue

    def store(fn):
        def epilogue(y, rows, c0):
            o_ref[rows, c0:c0 + MXU_COLS] = fn(y).astype(o_ref.dtype)
        return epilogue

    @pl.when(n == 0)
    def _():
        chunked(rope(q_scale))

    @pl.when(n == 1)
    def _():
        chunked(rope(1.0))

    @pl.when(n == 2)
    def _():
        chunked(store(lambda y: y))

    @pl.when(n == 3)
    def _():
        chunked(store(_gelu))

    @pl.when(n == 4)
    def _():
        sums = {0: [], half: []}

        def epilogue(y, rows, c0):
            t = _gelu(y)
            t_ref[rows, c0:c0 + MXU_COLS] = t
            sums[rows.start].append(jnp.sum(t, axis=-1, keepdims=True))

        chunked(epilogue)
        for r0 in (0, half):
            rows = slice(r0, r0 + half)
            mu = sum(sums[r0]) * (1.0 / width)
            tc = t_ref[rows, :] - mu
            t = tc * jax.lax.rsqrt(jnp.mean(tc * tc, axis=-1, keepdims=True) + EPS)
            o_ref[rows, :] = (t * lng_ref[...] + lnb_ref[...]).astype(o_ref.dtype)


def _proj(xn, w_in, cos, sin, ln_g, ln_b, *, width, tm=1024):
    seq, d = xn.shape
    d_in = w_in.shape[1]
    assert d_in == 5 * width
    return pl.pallas_call(
        functools.partial(_proj_kernel, q_scale=HEAD_DIM ** -0.5 * np.log2(np.e)),
        grid=(d_in // width, seq // tm),
        in_specs=[
            pl.BlockSpec((tm, d), lambda n, m: (m, 0)),
            pl.BlockSpec((d, width), lambda n, m: (0, n)),
            pl.BlockSpec((tm, LANES), lambda n, m: (m, 0)),
            pl.BlockSpec((tm, LANES), lambda n, m: (m, 0)),
            pl.BlockSpec((1, width), lambda n, m: (0, 0)),
            pl.BlockSpec((1, width), lambda n, m: (0, 0)),
        ],
        out_specs=pl.BlockSpec((tm, width), lambda n, m: (m, n)),
        out_shape=jax.ShapeDtypeStruct((seq, d_in), BF16),
        scratch_shapes=[pltpu.VMEM((d, width), BF16), pltpu.VMEM((tm, width), F32)],
        compiler_params=_params(2),
        name="proj",
    )(xn, w_in, cos, sin, ln_g.reshape(1, width), ln_b.reshape(1, width))


def _attn_kernel(q_ref, kc_ref, kp_ref, vc_ref, vp_ref, o_ref,
                 stage, q4, k4, v4, kb, vb, m4, l4, acc4, m_s, l_s, acc_s, bias, *, dil):
    t = pl.program_id(0)
    sup = q_ref.shape[0]
    sub = sup // dil
    n_units = sup // BLOCK

    def split(dst, n_rows):
        part = n_rows // dil
        for b in range(dil):
            dst[b * part:(b + 1) * part, :] = stage[pl.ds(b, part, stride=dil), :]

    stage[:sup] = kp_ref[...].astype(F32)
    stage[sup:] = kc_ref[...].astype(F32)
    split(k4, 2 * sup)
    stage[:sup] = vp_ref[...].astype(F32)
    stage[sup:] = vc_ref[...].astype(F32)
    split(v4, 2 * sup)
    stage[:sup] = q_ref[...].astype(F32)
    split(q4, sup)
    kb[:BLOCK] = kp_ref[sup - BLOCK:, :]
    kb[BLOCK:] = kc_ref[...]
    vb[:BLOCK] = vp_ref[sup - BLOCK:, :]
    vb[BLOCK:] = vc_ref[...]

    qi = jax.lax.broadcasted_iota(jnp.int32, (BLOCK, 2 * BLOCK), 0)
    kj = jax.lax.broadcasted_iota(jnp.int32, (BLOCK, 2 * BLOCK), 1)
    diff = qi + BLOCK - kj
    band = (diff >= 0) & (diff <= BLOCK)
    bias[0] = jnp.where(band & (kj >= BLOCK), 0.0, -jnp.inf)
    bias[1] = jnp.where(band, 0.0, -jnp.inf)
    low = jax.lax.broadcasted_iota(jnp.int32, (BLOCK, LANES), 1) < HEAD_DIM

    def head_pair(q, k, v, has_prev):
        mask = bias[has_prev.astype(jnp.int32)]
        ms, ls, os = [], [], []
        for hh in range(2):
            qm = jnp.where(low if hh == 0 else ~low, q, jnp.zeros_like(q))
            s = jax.lax.dot_general(qm, k, (((1,), (1,)), ((), ())),
                                    preferred_element_type=F32)
            s = s + mask
            m = jnp.max(s, axis=-1, keepdims=True)
            e = jnp.exp2(s - m)
            ms.append(m)
            ls.append(jnp.sum(e, axis=-1, keepdims=True))
            os.append(jnp.dot(e.astype(BF16), v, preferred_element_type=F32))
        return (jnp.where(low, ms[0], ms[1]), jnp.where(low, ls[0], ls[1]),
                jnp.where(low, os[0], os[1]))

    def merge(refs, rows, stats):
        m_ref, l_ref, acc_ref = refs
        m_u, l_u, o_u = stats
        m_old = m_ref[rows, :]
        m_new = jnp.maximum(m_old, m_u)
        a_old = jnp.exp2(m_old - m_new)
        a_u = jnp.exp2(m_u - m_new)
        m_ref[rows, :] = m_new
        l_ref[rows, :] = a_old * l_ref[rows, :] + a_u * l_u
        acc_ref[rows, :] = a_old * acc_ref[rows, :] + a_u * o_u

    def mid_unit(u, carry):
        r = u // (sub // BLOCK)
        n = u % (sub // BLOCK)
        rows = pl.ds(pl.multiple_of(r * sub + n * BLOCK, BLOCK), BLOCK)
        kv_rows = pl.ds(pl.multiple_of(r * 2 * sub + sub + (n - 1) * BLOCK, BLOCK), 2 * BLOCK)
        m_u, l_u, o_u = head_pair(q4[rows, :].astype(BF16), k4[kv_rows, :].astype(BF16),
                                  v4[kv_rows, :].astype(BF16), (t > 0) | (n > 0))
        m4[rows, :] = m_u
        l4[rows, :] = l_u
        acc4[rows, :] = o_u
        return carry

    def wide_unit(r, carry):
        a = r // dil
        b = r % dil
        rows = pl.ds(b * sub + a, BLOCK, stride=dil)
        kv_rows = pl.ds(b * 2 * sub + a, 2 * BLOCK, stride=dil)
        stats = head_pair(q4[rows, :].astype(BF16), k4[kv_rows, :].astype(BF16),
                          v4[kv_rows, :].astype(BF16), t > 0)
        merge((m4, l4, acc4), rows, stats)
        return carry

    def near_unit(n, carry):
        rows = pl.ds(pl.multiple_of(n * BLOCK, BLOCK), BLOCK)
        kv_rows = pl.ds(pl.multiple_of(n * BLOCK, BLOCK), 2 * BLOCK)
        stats = head_pair(q_ref[rows, :], kb[kv_rows, :], vb[kv_rows, :], (t > 0) | (n > 0))
        merge((m_s, l_s, acc_s), rows, stats)
        return carry

    jax.lax.fori_loop(0, n_units, mid_unit, 0, unroll=True)
    jax.lax.fori_loop(0, n_units, wide_unit, 0, unroll=True)
    for b in range(dil):
        src = slice(b * sub, (b + 1) * sub)
        dst = pl.ds(b, sub, stride=dil)
        m_s[dst, :] = m4[src, :]
        l_s[dst, :] = l4[src, :]
        acc_s[dst, :] = acc4[src, :]
    jax.lax.fori_loop(0, n_units, near_unit, 0, unroll=True)

    o_ref[...] = (acc_s[...] / l_s[...]).astype(o_ref.dtype)


def _attention(proj, *, width):
    seq, d_in = proj.shape
    dilations = tuple(d for _, d in DILATED_BRANCHES)
    steps = {w // d for w, d in DILATED_BRANCHES}
    assert steps == {BLOCK}, "every branch must span exactly one previous block"
    dil = dilations[1]
    assert dilations == (1, dil, dil * dil)
    sup = BLOCK * dil * dil
    seg = width // LANES
    cur = lambda i: pl.BlockSpec((sup, LANES), lambda t, p: (t, i * seg + p))
    prev = lambda i: pl.BlockSpec((sup, LANES),
                                  lambda t, p: (jnp.maximum(t - 1, 0), i * seg + p))
    f32_rows = lambda n: pltpu.VMEM((n, LANES), F32)
    return pl.pallas_call(
        functools.partial(_attn_kernel, dil=dil),
        grid=(seq // sup, seg),
        in_specs=[cur(0), cur(1), prev(1), cur(2), prev(2)],
        out_specs=pl.BlockSpec((sup, LANES), lambda t, p: (t, p)),
        out_shape=jax.ShapeDtypeStruct((seq, width), BF16),
        scratch_shapes=[f32_rows(2 * sup),
                        f32_rows(sup), f32_rows(2 * sup), f32_rows(2 * sup),
                        pltpu.VMEM((sup + BLOCK, LANES), BF16),
                        pltpu.VMEM((sup + BLOCK, LANES), BF16),
                        f32_rows(sup), f32_rows(sup), f32_rows(sup),
                        f32_rows(sup), f32_rows(sup), f32_rows(sup),
                        pltpu.VMEM((2, BLOCK, 2 * BLOCK), F32)],
        compiler_params=_params(2),
        name="dilated_attn",
    )(proj, proj, proj, proj, proj)


def _mix_kernel(h_ref, attn_ref, u_ref, gv_ref, ws_ref, bs_ref, ag_ref, sg_ref, wo_ref, ng_ref,
                out_ref, xn_ref, mixed_ref):
    tm = h_ref.shape[0]
    width = u_ref.shape[1]
    mixed_ref[:, :width] = _rms(attn_ref[...].astype(F32), ag_ref[...]).astype(BF16)

    ci = jax.lax.broadcasted_iota(jnp.int32, (SGU_CHUNK, SGU_CHUNK), 0)
    cj = jax.lax.broadcasted_iota(jnp.int32, (SGU_CHUNK, SGU_CHUNK), 1)
    causal = cj <= ci
    for g in range(width // SGU_GROUP_DIM):
        gs = slice(g * SGU_GROUP_DIM, (g + 1) * SGU_GROUP_DIM)
        w = jnp.where(causal, ws_ref[g], 0.0).astype(BF16)
        b = bs_ref[:, g:g + 1]
        for c in range(tm // SGU_CHUNK):
            rs = slice(c * SGU_CHUNK, (c + 1) * SGU_CHUNK)
            sv = jnp.dot(w, gv_ref[rs, gs], preferred_element_type=F32) + b
            out_ref[rs, gs] = u_ref[rs, gs].astype(F32) * sv
    sgu = out_ref[:, :width]
    mixed_ref[:, width:] = _rms(sgu, sg_ref[...]).astype(BF16)

    h = h_ref[...] + jnp.dot(mixed_ref[...], wo_ref[...], preferred_element_type=F32)
    out_ref[...] = h
    xn_ref[...] = _rms(h, ng_ref[...]).astype(BF16)


def _mix(h, attn, proj, w_s, b_s, attn_g, sgu_g, w_out, next_g, *, width, tm=512):
    seq, d = h.shape
    n_grp = w_s.shape[0]
    row = lambda m: (m, 0)
    const2 = lambda m: (0, 0)
    return pl.pallas_call(
        _mix_kernel,
        grid=(seq // tm,),
        in_specs=[pl.BlockSpec((tm, d), row),
                  pl.BlockSpec((tm, width), row),
                  pl.BlockSpec((tm, width), lambda m: (m, 3)),
                  pl.BlockSpec((tm, width), lambda m: (m, 4)),
                  pl.BlockSpec((n_grp, SGU_CHUNK, SGU_CHUNK), lambda m: (0, 0, 0)),
                  pl.BlockSpec((SGU_CHUNK, n_grp), const2),
                  pl.BlockSpec((1, width), const2),
                  pl.BlockSpec((1, width), const2),
                  _resident((2 * width, d)),
                  pl.BlockSpec((1, d), const2)],
        out_specs=[pl.BlockSpec((tm, d), row), pl.BlockSpec((tm, d), row)],
        out_shape=[jax.ShapeDtypeStruct((seq, d), F32), jax.ShapeDtypeStruct((seq, d), BF16)],
        scratch_shapes=[pltpu.VMEM((tm, 2 * width), BF16)],
        compiler_params=_params(1),
        name="mix_out",
    )(h, attn, proj, proj, w_s, b_s.T, attn_g.reshape(1, width), sgu_g.reshape(1, width),
      w_out, next_g.reshape(1, d))


def _mem_kv_kernel(mem_ref, g_ref, wk_ref, wv_ref, k_ref, v_ref):
    mk = _rms(mem_ref[...], g_ref[...]).astype(BF16)
    k_ref[...] = jnp.dot(mk, wk_ref[...].astype(BF16), preferred_element_type=F32).astype(BF16)
    v_ref[...] = jnp.dot(mk, wv_ref[...].astype(BF16), preferred_element_type=F32).astype(BF16)


def _mem_kv(mem, g, w_k, w_v, *, tn=512):
    n_mem, d = mem.shape
    return pl.pallas_call(
        _mem_kv_kernel,
        grid=(d // tn,),
        in_specs=[pl.BlockSpec((n_mem, d), lambda n: (0, 0)),
                  pl.BlockSpec((1, d), lambda n: (0, 0)),
                  pl.BlockSpec((d, tn), lambda n: (0, n)),
                  pl.BlockSpec((d, tn), lambda n: (0, n))],
        out_specs=[pl.BlockSpec((n_mem, tn), lambda n: (0, n))] * 2,
        out_shape=[jax.ShapeDtypeStruct((n_mem, d), BF16)] * 2,
        compiler_params=_params(1),
        name="mem_kv",
    )(mem, g.reshape(1, d), w_k, w_v)


def _mem_attn_kernel(xq_ref, wq_ref, k_ref, v_ref, o_ref, q_ref):
    hd = xq_ref.shape[1] // N_MEM_HEADS
    q_ref[...] = jnp.dot(xq_ref[...], wq_ref[...], preferred_element_type=F32).astype(BF16)
    for i in range(N_MEM_HEADS):
        sl = slice(i * hd, (i + 1) * hd)
        s = jax.lax.dot_general(q_ref[:, sl], k_ref[:, sl], (((1,), (1,)), ((), ())),
                                preferred_element_type=F32) * (hd ** -0.5)
        m = jnp.max(s, axis=-1, keepdims=True)
        e = jnp.exp(s - m)
        p = e / jnp.sum(e, axis=-1, keepdims=True)
        o_ref[:, sl] = jnp.dot(p.astype(BF16), v_ref[:, sl],
                               preferred_element_type=F32).astype(BF16)


def _mem_attn(xq, w_q, k, v, *, tm=512):
    seq, d = xq.shape
    n_mem = k.shape[0]
    const = lambda m: (0, 0)
    return pl.pallas_call(
        _mem_attn_kernel,
        grid=(seq // tm,),
        in_specs=[pl.BlockSpec((tm, d), lambda m: (m, 0)),
                  _resident((d, d)),
                  pl.BlockSpec((n_mem, d), const),
                  pl.BlockSpec((n_mem, d), const)],
        out_specs=pl.BlockSpec((tm, d), lambda m: (m, 0)),
        out_shape=jax.ShapeDtypeStruct((seq, d), BF16),
        scratch_shapes=[pltpu.VMEM((tm, d), BF16)],
        compiler_params=_params(1),
        name="mem_attn",
    )(xq, w_q, k, v)


def _mem_out_kernel(h_ref, o_ref, wo_ref, ng_ref, out_ref, xn_ref):
    h = h_ref[...] + jnp.dot(o_ref[...], wo_ref[...], preferred_element_type=F32)
    out_ref[...] = h
    xn_ref[...] = _rms(h, ng_ref[...]).astype(BF16)


def _mem_out(h, o, w_o, next_g, *, tm=512):
    seq, d = h.shape
    row = pl.BlockSpec((tm, d), lambda m: (m, 0))
    return pl.pallas_call(
        _mem_out_kernel,
        grid=(seq // tm,),
        in_specs=[row, row, _resident((d, d)), pl.BlockSpec((1, d), lambda m: (0, 0))],
        out_specs=[row, row],
        out_shape=[jax.ShapeDtypeStruct((seq, d), F32), jax.ShapeDtypeStruct((seq, d), BF16)],
        compiler_params=_params(1),
        name="mem_out",
    )(h, o, w_o, next_g.reshape(1, d))


def kernel(x, mem, positions, ffn1_norm, ffn1_w_gate, ffn1_w_up, ffn1_w_down, mix_norm, w_in,
           sgu_ln_gain, sgu_ln_bias, sgu_w_s, sgu_b_s, attn_out_gain, sgu_out_gain, w_out,
           mem_q_norm, mem_kv_norm, mem_w_q, mem_w_k, mem_w_v, mem_w_o, ffn2_norm,
           ffn2_w_gate, ffn2_w_up, ffn2_w_down, final_norm):
    batch, seq, d = x.shape
    depth = ffn1_norm.shape[0]
    width = attn_out_gain.shape[1]
    assert batch == 1 and mem.shape[0] == 1 and depth == 1
    layer = 0

    h = x.reshape(seq, d)
    mem2 = mem.reshape(mem.shape[1], d)

    cos, sin, xn = _prep(positions, h, ffn1_norm[layer])
    a, wd_bf = _ffn_up(xn, ffn1_w_gate[layer], ffn1_w_up[layer], ffn1_w_down[layer])
    h, xn, w_out_bf, w_q_bf, w_o_bf = _ffn_down(
        a, h, wd_bf, mix_norm[layer], (w_out[layer], mem_w_q[layer], mem_w_o[layer]), final=False)
    proj = _proj(xn, w_in[layer], cos, sin, sgu_ln_gain[layer], sgu_ln_bias[layer], width=width)
    attn = _attention(proj, width=width)
    h, xn = _mix(h, attn, proj, sgu_w_s[layer], sgu_b_s[layer], attn_out_gain[layer],
                 sgu_out_gain[layer], w_out_bf, mem_q_norm[layer], width=width)
    k, v = _mem_kv(mem2, mem_kv_norm[layer], mem_w_k[layer], mem_w_v[layer])
    o = _mem_attn(xn, w_q_bf, k, v)
    h, xn = _mem_out(h, o, w_o_bf, ffn2_norm[layer])
    a, wd_bf = _ffn_up(xn, ffn2_w_gate[layer], ffn2_w_up[layer], ffn2_w_down[layer])
    out = _ffn_down(a, h, wd_bf, final_norm, final=True)[0]
    return out.reshape(batch, seq, d)
```

```python
import functools

import jax
import jax.numpy as jnp
import numpy as np
from jax.experimental import pallas as pl
from jax.experimental.pallas import tpu as pltpu

F32 = jnp.float32
BF16 = jnp.bfloat16

EPS = 1e-6
ROPE_THETA = 10000.0
FFN_RES_SCALE = 0.5
HEAD_DIM = 64
BLOCK = 128
DILATED_BRANCHES = ((128, 1), (512, 4), (2048, 16))
SGU_CHUNK = 128
SGU_GROUP_DIM = 128
N_MEM_HEADS = 4
LANES = 128
MXU_COLS = 256
CAST_ROWS = 256

VMEM_LIMIT = 56 * 1024 * 1024


def _params(n_axes, vmem=VMEM_LIMIT):
    return pltpu.CompilerParams(
        dimension_semantics=("arbitrary",) * n_axes, vmem_limit_bytes=vmem)


def _resident(shape):
    return pl.BlockSpec(shape, lambda *_: (0,) * len(shape), pipeline_mode=pl.Buffered(1))


def _rms(x, g):
    return x * jax.lax.rsqrt(jnp.mean(x * x, axis=-1, keepdims=True) + EPS) * g


def _gelu(x):
    c = np.sqrt(2.0 / np.pi).astype(np.float32)
    return 0.5 * x * (1.0 + jnp.tanh(c * (x + 0.044715 * (x * x * x))))


def _cast_to_bf16(src_ref, dst_ref):
    rows = src_ref.shape[0]
    chunk = min(CAST_ROWS, rows)

    def body(i, carry):
        sl = pl.ds(pl.multiple_of(i * chunk, chunk), chunk)
        dst_ref[sl, :] = src_ref[sl, :].astype(BF16)
        return carry

    jax.lax.fori_loop(0, rows // chunk, body, 0)


def _prep_kernel(pos_ref, freq_ref, x_ref, g_ref, cos_ref, sin_ref, xn_ref):
    ang = pos_ref[...].astype(F32) * freq_ref[...]
    lane = jax.lax.broadcasted_iota(jnp.int32, ang.shape, 1)
    sign = jnp.where(lane % HEAD_DIM < HEAD_DIM // 2, -1.0, 1.0)
    cos_ref[...] = jnp.cos(ang)
    sin_ref[...] = jnp.sin(ang) * sign
    xn_ref[...] = _rms(x_ref[...], g_ref[...]).astype(BF16)


def _prep(positions, x, g, *, tm=1024):
    seq, d = x.shape
    half = HEAD_DIM // 2
    inv_freq = ROPE_THETA ** (-jnp.arange(0, HEAD_DIM, 2, dtype=F32) / HEAD_DIM)
    freq_lane = jnp.tile(inv_freq, LANES // half).reshape(1, LANES)
    pos = positions.reshape(seq, 1)
    row = lambda w: pl.BlockSpec((tm, w), lambda i: (i, 0))
    const = lambda w: pl.BlockSpec((1, w), lambda i: (0, 0))
    return pl.pallas_call(
        _prep_kernel,
        grid=(seq // tm,),
        in_specs=[row(1), const(LANES), row(d), const(d)],
        out_specs=[row(LANES), row(LANES), row(d)],
        out_shape=[jax.ShapeDtypeStruct((seq, LANES), F32)] * 2
        + [jax.ShapeDtypeStruct((seq, d), BF16)],
        compiler_params=_params(1),
        name="prep",
    )(pos, freq_lane, x, g.reshape(1, d))


def _ffn_up_kernel(xn_ref, wg_ref, wu_ref, wd_ref, a_ref, wdb_ref, wgb, wub):
    @pl.when(pl.program_id(1) == 0)
    def _():
        _cast_to_bf16(wg_ref, wgb)
        _cast_to_bf16(wu_ref, wub)
        _cast_to_bf16(wd_ref, wdb_ref)

    xn = xn_ref[...]
    gate = jnp.dot(xn, wgb[...], preferred_element_type=F32)
    up = jnp.dot(xn, wub[...], preferred_element_type=F32)
    a_ref[...] = ((gate * (FFN_RES_SCALE / (1.0 + jnp.exp(-gate)))) * up).astype(BF16)


def _ffn_up(xn, wg, wu, wd, *, tm=1024, tf=512):
    seq, d = xn.shape
    dff = wg.shape[1]
    return pl.pallas_call(
        _ffn_up_kernel,
        grid=(dff // tf, seq // tm),
        in_specs=[pl.BlockSpec((tm, d), lambda f, m: (m, 0)),
                  pl.BlockSpec((d, tf), lambda f, m: (0, f)),
                  pl.BlockSpec((d, tf), lambda f, m: (0, f)),
                  pl.BlockSpec((tf, d), lambda f, m: (f, 0))],
        out_specs=[pl.BlockSpec((tm, tf), lambda f, m: (m, f)),
                   pl.BlockSpec((tf, d), lambda f, m: (f, 0))],
        out_shape=[jax.ShapeDtypeStruct((seq, dff), BF16),
                   jax.ShapeDtypeStruct((dff, d), BF16)],
        scratch_shapes=[pltpu.VMEM((d, tf), BF16), pltpu.VMEM((d, tf), BF16)],
        compiler_params=_params(2),
        name="ffn_up",
    )(xn, wg, wu, wd)


def _ffn_down_kernel(*refs, final, n_cast):
    a_ref, x_ref, wd_ref, g_ref = refs[:4]
    cast_in = refs[4:4 + n_cast]
    out_refs = refs[4 + n_cast:]
    h = x_ref[...] + jnp.dot(a_ref[...], wd_ref[...], preferred_element_type=F32)
    if final:
        out_refs[0][...] = _rms(h, g_ref[...])
    else:
        out_refs[0][...] = h
        out_refs[1][...] = _rms(h, g_ref[...]).astype(BF16)
        for src, dst in zip(cast_in, out_refs[2:]):
            dst[...] = src[...].astype(BF16)


def _ffn_down(a, x, wd_bf, g, cast_weights=(), *, final, tm=256):
    seq, d = x.shape
    dff = a.shape[1]
    steps = seq // tm
    row = pl.BlockSpec((tm, d), lambda m: (m, 0))
    out_specs = [row] if final else [row, row]
    out_shape = [jax.ShapeDtypeStruct((seq, d), F32)]
    if not final:
        out_shape.append(jax.ShapeDtypeStruct((seq, d), BF16))
    cast_specs = []
    for w in cast_weights:
        assert not final and w.shape[0] % steps == 0
        spec = pl.BlockSpec((w.shape[0] // steps, w.shape[1]), lambda m: (m, 0))
        cast_specs.append(spec)
        out_specs.append(spec)
        out_shape.append(jax.ShapeDtypeStruct(w.shape, BF16))
    return pl.pallas_call(
        functools.partial(_ffn_down_kernel, final=final, n_cast=len(cast_weights)),
        grid=(steps,),
        in_specs=[pl.BlockSpec((tm, dff), lambda m: (m, 0)), row,
                  _resident((dff, d)), pl.BlockSpec((1, d), lambda m: (0, 0))] + cast_specs,
        out_specs=out_specs,
        out_shape=out_shape,
        compiler_params=_params(1),
        name="ffn_down_final" if final else "ffn_down",
    )(a, x, wd_bf, g.reshape(1, d), *cast_weights)


def _proj_kernel(xn_ref, w_ref, cos_ref, sin_ref, lng_ref, lnb_ref, o_ref, wb, t_ref, *, q_scale):
    n = pl.program_id(0)
    width = o_ref.shape[1]

    @pl.when(pl.program_id(1) == 0)
    def _():
        _cast_to_bf16(w_ref, wb)

    half = xn_ref.shape[0] // 2

    def chunked(epilogue):
        for j in range(width // MXU_COLS):
            c0 = j * MXU_COLS
            for r0 in (0, half):
                rows = slice(r0, r0 + half)
                y = jnp.dot(xn_ref[rows, :], wb[:, c0:c0 + MXU_COLS],
                            preferred_element_type=F32)
                epilogue(y, rows, c0)

    def rope(scale):
        def epilogue(y, rows, c0):
            cos = cos_ref[rows, :]
            sin = sin_ref[rows, :]
            lane = jax.lax.broadcasted_iota(jnp.int32, cos.shape, 1)
            first = lane % HEAD_DIM < HEAD_DIM // 2
            for j in range(MXU_COLS // LANES):
                blk = y[:, j * LANES:(j + 1) * LANES]
                partner = jnp.where(first,
                                    pltpu.roll(blk, LANES - HEAD_DIM // 2, 1),
                                    pltpu.roll(blk, HEAD_DIM // 2, 1))
                out = blk * cos + partner * sin
                if scale != 1.0:
                    out = out * scale
                o_ref[rows, c0 + j * LANES:c0 + (j + 1) * LANES] = out.astype(o_ref.dtype)
        return epilogue

    def store(fn):
        def epilogue(y, rows, c0):
            o_ref[rows, c0:c0 + MXU_COLS] = fn(y).astype(o_ref.dtype)
        return epilogue

    @pl.when(n == 0)
    def _():
        chunked(rope(q_scale))

    @pl.when(n == 1)
    def _():
        chunked(rope(1.0))

    @pl.when(n == 2)
    def _():
        chunked(store(lambda y: y))

    @pl.when(n == 3)
    def _():
        chunked(store(_gelu))

    @pl.when(n == 4)
    def _():
        sums = {0: [], half: []}

        def epilogue(y, rows, c0):
            t = _gelu(y)
            t_ref[rows, c0:c0 + MXU_COLS] = t
            sums[rows.start].append(jnp.sum(t, axis=-1, keepdims=True))

        chunked(epilogue)
        for r0 in (0, half):
            rows = slice(r0, r0 + half)
            mu = sum(sums[r0]) * (1.0 / width)
            tc = t_ref[rows, :] - mu
            t = tc * jax.lax.rsqrt(jnp.mean(tc * tc, axis=-1, keepdims=True) + EPS)
            o_ref[rows, :] = (t * lng_ref[...] + lnb_ref[...]).astype(o_ref.dtype)


def _proj(xn, w_in, cos, sin, ln_g, ln_b, *, width, tm=1024):
    seq, d = xn.shape
    d_in = w_in.shape[1]
    assert d_in == 5 * width
    return pl.pallas_call(
        functools.partial(_proj_kernel, q_scale=HEAD_DIM ** -0.5 * np.log2(np.e)),
        grid=(d_in // width, seq // tm),
        in_specs=[
            pl.BlockSpec((tm, d), lambda n, m: (m, 0)),
            pl.BlockSpec((d, width), lambda n, m: (0, n)),
            pl.BlockSpec((tm, LANES), lambda n, m: (m, 0)),
            pl.BlockSpec((tm, LANES), lambda n, m: (m, 0)),
            pl.BlockSpec((1, width), lambda n, m: (0, 0)),
            pl.BlockSpec((1, width), lambda n, m: (0, 0)),
        ],
        out_specs=pl.BlockSpec((tm, width), lambda n, m: (m, n)),
        out_shape=jax.ShapeDtypeStruct((seq, d_in), BF16),
        scratch_shapes=[pltpu.VMEM((d, width), BF16), pltpu.VMEM((tm, width), F32)],
        compiler_params=_params(2),
        name="proj",
    )(xn, w_in, cos, sin, ln_g.reshape(1, width), ln_b.reshape(1, width))


def _attn_kernel(q_ref, kc_ref, kp_ref, vc_ref, vp_ref, o_ref,
                 stage, q4, k4, v4, kb, vb, m4, l4, acc4, m_s, l_s, acc_s, bias, *, dil):
    t = pl.program_id(0)
    sup = q_ref.shape[0]
    sub = sup // dil
    n_units = sup // BLOCK

    def split(dst, n_rows):
        part = n_rows // dil
        for b in range(dil):
            dst[b * part:(b + 1) * part, :] = stage[pl.ds(b, part, stride=dil), :]

    stage[:sup] = kp_ref[...].astype(F32)
    stage[sup:] = kc_ref[...].astype(F32)
    split(k4, 2 * sup)
    stage[:sup] = vp_ref[...].astype(F32)
    stage[sup:] = vc_ref[...].astype(F32)
    split(v4, 2 * sup)
    stage[:sup] = q_ref[...].astype(F32)
    split(q4, sup)
    kb[:BLOCK] = kp_ref[sup - BLOCK:, :]
    kb[BLOCK:] = kc_ref[...]
    vb[:BLOCK] = vp_ref[sup - BLOCK:, :]
    vb[BLOCK:] = vc_ref[...]

    qi = jax.lax.broadcasted_iota(jnp.int32, (BLOCK, 2 * BLOCK), 0)
    kj = jax.lax.broadcasted_iota(jnp.int32, (BLOCK, 2 * BLOCK), 1)
    diff = qi + BLOCK - kj
    band = (diff >= 0) & (diff <= BLOCK)
    bias[0] = jnp.where(band & (kj >= BLOCK), 0.0, -jnp.inf)
    bias[1] = jnp.where(band, 0.0, -jnp.inf)
    low = jax.lax.broadcasted_iota(jnp.int32, (BLOCK, LANES), 1) < HEAD_DIM

    def head_pair(q, k, v, has_prev):
        mask = bias[has_prev.astype(jnp.int32)]
        ms, ls, os = [], [], []
        for hh in range(2):
            qm = jnp.where(low if hh == 0 else ~low, q, jnp.zeros_like(q))
            s = jax.lax.dot_general(qm, k, (((1,), (1,)), ((), ())),
                                    preferred_element_type=F32)
            s = s + mask
            m = jnp.max(s, axis=-1, keepdims=True)
            e = jnp.exp2(s - m)
            ms.append(m)
            ls.append(jnp.sum(e, axis=-1, keepdims=True))
            os.append(jnp.dot(e.astype(BF16), v, preferred_element_type=F32))
        return (jnp.where(low, ms[0], ms[1]), jnp.where(low, ls[0], ls[1]),
                jnp.where(low, os[0], os[1]))

    def merge(refs, rows, stats):
        m_ref, l_ref, acc_ref = refs
        m_u, l_u, o_u = stats
        m_old = m_ref[rows, :]
        m_new = jnp.maximum(m_old, m_u)
        a_old = jnp.exp2(m_old - m_new)
        a_u = jnp.exp2(m_u - m_new)
        m_ref[rows, :] = m_new
        l_ref[rows, :] = a_old * l_ref[rows, :] + a_u * l_u
        acc_ref[rows, :] = a_old * acc_ref[rows, :] + a_u * o_u

    def mid_unit(u, carry):
        r = u // (sub // BLOCK)
        n = u % (sub // BLOCK)
        rows = pl.ds(pl.multiple_of(r * sub + n * BLOCK, BLOCK), BLOCK)
        kv_rows = pl.ds(pl.multiple_of(r * 2 * sub + sub + (n - 1) * BLOCK, BLOCK), 2 * BLOCK)
        m_u, l_u, o_u = head_pair(q4[rows, :].astype(BF16), k4[kv_rows, :].astype(BF16),
                                  v4[kv_rows, :].astype(BF16), (t > 0) | (n > 0))
        m4[rows, :] = m_u
        l4[rows, :] = l_u
        acc4[rows, :] = o_u
        return carry

    def wide_unit(r, carry):
        a = r // dil
        b = r % dil
        rows = pl.ds(b * sub + a, BLOCK, stride=dil)
        kv_rows = pl.ds(b * 2 * sub + a, 2 * BLOCK, stride=dil)
        stats = head_pair(q4[rows, :].astype(BF16), k4[kv_rows, :].astype(BF16),
                          v4[kv_rows, :].astype(BF16), t > 0)
        merge((m4, l4, acc4), rows, stats)
        return carry

    def near_unit(n, carry):
        rows = pl.ds(pl.multiple_of(n * BLOCK, BLOCK), BLOCK)
        kv_rows = pl.ds(pl.multiple_of(n * BLOCK, BLOCK), 2 * BLOCK)
        stats = head_pair(q_ref[rows, :], kb[kv_rows, :], vb[kv_rows, :], (t > 0) | (n > 0))
        merge((m_s, l_s, acc_s), rows, stats)
        return carry

    jax.lax.fori_loop(0, n_units, mid_unit, 0, unroll=True)
    jax.lax.fori_loop(0, n_units, wide_unit, 0, unroll=True)
    for b in range(dil):
        src = slice(b * sub, (b + 1) * sub)
        dst = pl.ds(b, sub, stride=dil)
        m_s[dst, :] = m4[src, :]
        l_s[dst, :] = l4[src, :]
        acc_s[dst, :] = acc4[src, :]
    jax.lax.fori_loop(0, n_units, near_unit, 0, unroll=True)

    o_ref[...] = (acc_s[...] / l_s[...]).astype(o_ref.dtype)


def _attention(proj, *, width):
    seq, d_in = proj.shape
    dilations = tuple(d for _, d in DILATED_BRANCHES)
    steps = {w // d for w, d in DILATED_BRANCHES}
    assert steps == {BLOCK}, "every branch must span exactly one previous block"
    dil = dilations[1]
    assert dilations == (1, dil, dil * dil)
    sup = BLOCK * dil * dil
    seg = width // LANES
    cur = lambda i: pl.BlockSpec((sup, LANES), lambda t, p: (t, i * seg + p))
    prev = lambda i: pl.BlockSpec((sup, LANES),
                                  lambda t, p: (jnp.maximum(t - 1, 0), i * seg + p))
    f32_rows = lambda n: pltpu.VMEM((n, LANES), F32)
    return pl.pallas_call(
        functools.partial(_attn_kernel, dil=dil),
        grid=(seq // sup, seg),
        in_specs=[cur(0), cur(1), prev(1), cur(2), prev(2)],
        out_specs=pl.BlockSpec((sup, LANES), lambda t, p: (t, p)),
        out_shape=jax.ShapeDtypeStruct((seq, width), BF16),
        scratch_shapes=[f32_rows(2 * sup),
                        f32_rows(sup), f32_rows(2 * sup), f32_rows(2 * sup),
                        pltpu.VMEM((sup + BLOCK, LANES), BF16),
                        pltpu.VMEM((sup + BLOCK, LANES), BF16),
                        f32_rows(sup), f32_rows(sup), f32_rows(sup),
                        f32_rows(sup), f32_rows(sup), f32_rows(sup),
                        pltpu.VMEM((2, BLOCK, 2 * BLOCK), F32)],
        compiler_params=_params(2),
        name="dilated_attn",
    )(proj, proj, proj, proj, proj)


def _mix_kernel(h_ref, attn_ref, u_ref, gv_ref, ws_ref, bs_ref, ag_ref, sg_ref, wo_ref, ng_ref,
                out_ref, xn_ref, mixed_ref):
    tm = h_ref.shape[0]
    width = u_ref.shape[1]
    mixed_ref[:, :width] = _rms(attn_ref[...].astype(F32), ag_ref[...]).astype(BF16)

    ci = jax.lax.broadcasted_iota(jnp.int32, (SGU_CHUNK, SGU_CHUNK), 0)
    cj = jax.lax.broadcasted_iota(jnp.int32, (SGU_CHUNK, SGU_CHUNK), 1)
    causal = cj <= ci
    for g in range(width // SGU_GROUP_DIM):
        gs = slice(g * SGU_GROUP_DIM, (g + 1) * SGU_GROUP_DIM)
        w = jnp.where(causal, ws_ref[g], 0.0).astype(BF16)
        b = bs_ref[:, g:g + 1]
        for c in range(tm // SGU_CHUNK):
            rs = slice(c * SGU_CHUNK, (c + 1) * SGU_CHUNK)
            sv = jnp.dot(w, gv_ref[rs, gs], preferred_element_type=F32) + b
            out_ref[rs, gs] = u_ref[rs, gs].astype(F32) * sv
    sgu = out_ref[:, :width]
    mixed_ref[:, width:] = _rms(sgu, sg_ref[...]).astype(BF16)

    h = h_ref[...] + jnp.dot(mixed_ref[...], wo_ref[...], preferred_element_type=F32)
    out_ref[...] = h
    xn_ref[...] = _rms(h, ng_ref[...]).astype(BF16)


def _mix(h, attn, proj, w_s, b_s, attn_g, sgu_g, w_out, next_g, *, width, tm=512):
    seq, d = h.shape
    n_grp = w_s.shape[0]
    row = lambda m: (m, 0)
    const2 = lambda m: (0, 0)
    return pl.pallas_call(
        _mix_kernel,
        grid=(seq // tm,),
        in_specs=[pl.BlockSpec((tm, d), row),
                  pl.BlockSpec((tm, width), row),
                  pl.BlockSpec((tm, width), lambda m: (m, 3)),
                  pl.BlockSpec((tm, width), lambda m: (m, 4)),
                  pl.BlockSpec((n_grp, SGU_CHUNK, SGU_CHUNK), lambda m: (0, 0, 0)),
                  pl.BlockSpec((SGU_CHUNK, n_grp), const2),
                  pl.BlockSpec((1, width), const2),
                  pl.BlockSpec((1, width), const2),
                  _resident((2 * width, d)),
                  pl.BlockSpec((1, d), const2)],
        out_specs=[pl.BlockSpec((tm, d), row), pl.BlockSpec((tm, d), row)],
        out_shape=[jax.ShapeDtypeStruct((seq, d), F32), jax.ShapeDtypeStruct((seq, d), BF16)],
        scratch_shapes=[pltpu.VMEM((tm, 2 * width), BF16)],
        compiler_params=_params(1),
        name="mix_out",
    )(h, attn, proj, proj, w_s, b_s.T, attn_g.reshape(1, width), sgu_g.reshape(1, width),
      w_out, next_g.reshape(1, d))


def _mem_fold_kernel(mem_ref, g_ref, wk_ref, wv_ref, wq_ref, wo_ref, kq_ref, vo_ref):
    hd = wk_ref.shape[1]
    mk = _rms(mem_ref[...], g_ref[...]).astype(BF16)
    k = jnp.dot(mk, wk_ref[...].astype(BF16), preferred_element_type=F32).astype(BF16)
    v = jnp.dot(mk, wv_ref[...].astype(BF16), preferred_element_type=F32).astype(BF16)
    kq = jax.lax.dot_general(wq_ref[...].astype(BF16), k, (((1,), (1,)), ((), ())),
                             preferred_element_type=F32)
    kq_ref[...] = (kq * (hd ** -0.5)).astype(BF16)
    vo_ref[...] = jnp.dot(v, wo_ref[...].astype(BF16), preferred_element_type=F32).astype(BF16)


def _mem_fold(mem, g, w_k, w_v, w_q, w_o):
    n_mem, d = mem.shape
    hd = d // N_MEM_HEADS
    cols = pl.BlockSpec((d, hd), lambda h: (0, h))
    return pl.pallas_call(
        _mem_fold_kernel,
        grid=(N_MEM_HEADS,),
        in_specs=[pl.BlockSpec((n_mem, d), lambda h: (0, 0)),
                  pl.BlockSpec((1, d), lambda h: (0, 0)),
                  cols, cols, cols,
                  pl.BlockSpec((hd, d), lambda h: (h, 0))],
        out_specs=[pl.BlockSpec((d, n_mem), lambda h: (0, h)),
                   pl.BlockSpec((n_mem, d), lambda h: (h, 0))],
        out_shape=[jax.ShapeDtypeStruct((d, N_MEM_HEADS * n_mem), BF16),
                   jax.ShapeDtypeStruct((N_MEM_HEADS * n_mem, d), BF16)],
        compiler_params=_params(1),
        name="mem_fold",
    )(mem, g.reshape(1, d), w_k, w_v, w_q, w_o)


def _mem_cross_kernel(h_ref, xq_ref, kq_ref, vo_ref, ng_ref, out_ref, xn_ref, p_ref):
    n_mem = kq_ref.shape[1] // N_MEM_HEADS
    for i in range(N_MEM_HEADS):
        sl = slice(i * n_mem, (i + 1) * n_mem)
        s = jnp.dot(xq_ref[...], kq_ref[:, sl], preferred_element_type=F32)
        m = jnp.max(s, axis=-1, keepdims=True)
        e = jnp.exp(s - m)
        p_ref[:, sl] = (e / jnp.sum(e, axis=-1, keepdims=True)).astype(BF16)
    h = h_ref[...] + jnp.dot(p_ref[...], vo_ref[...], preferred_element_type=F32)
    out_ref[...] = h
    xn_ref[...] = _rms(h, ng_ref[...]).astype(BF16)


def _mem_cross(h, xq, kq, vo, next_g, *, tm=512):
    seq, d = h.shape
    row = pl.BlockSpec((tm, d), lambda m: (m, 0))
    return pl.pallas_call(
        _mem_cross_kernel,
        grid=(seq // tm,),
        in_specs=[row, row, _resident(kq.shape), _resident(vo.shape),
                  pl.BlockSpec((1, d), lambda m: (0, 0))],
        out_specs=[row, row],
        out_shape=[jax.ShapeDtypeStruct((seq, d), F32), jax.ShapeDtypeStruct((seq, d), BF16)],
        scratch_shapes=[pltpu.VMEM((tm, kq.shape[1]), BF16)],
        compiler_params=_params(1),
        name="mem_cross",
    )(h, xq, kq, vo, next_g.reshape(1, d))


def kernel(x, mem, positions, ffn1_norm, ffn1_w_gate, ffn1_w_up, ffn1_w_down, mix_norm, w_in,
           sgu_ln_gain, sgu_ln_bias, sgu_w_s, sgu_b_s, attn_out_gain, sgu_out_gain, w_out,
           mem_q_norm, mem_kv_norm, mem_w_q, mem_w_k, mem_w_v, mem_w_o, ffn2_norm,
           ffn2_w_gate, ffn2_w_up, ffn2_w_down, final_norm):
    batch, seq, d = x.shape
    depth = ffn1_norm.shape[0]
    width = attn_out_gain.shape[1]
    assert batch == 1 and mem.shape[0] == 1 and depth == 1
    layer = 0

    h = x.reshape(seq, d)
    mem2 = mem.reshape(mem.shape[1], d)

    cos, sin, xn = _prep(positions, h, ffn1_norm[layer])
    a, wd_bf = _ffn_up(xn, ffn1_w_gate[layer], ffn1_w_up[layer], ffn1_w_down[layer])
    h, xn, w_out_bf = _ffn_down(a, h, wd_bf, mix_norm[layer], (w_out[layer],), final=False)
    proj = _proj(xn, w_in[layer], cos, sin, sgu_ln_gain[layer], sgu_ln_bias[layer], width=width)
    attn = _attention(proj, width=width)
    h, xn = _mix(h, attn, proj, sgu_w_s[layer], sgu_b_s[layer], attn_out_gain[layer],
                 sgu_out_gain[layer], w_out_bf, mem_q_norm[layer], width=width)
    kq, vo = _mem_fold(mem2, mem_kv_norm[layer], mem_w_k[layer], mem_w_v[layer],
                       mem_w_q[layer], mem_w_o[layer])
    h, xn = _mem_cross(h, xn, kq, vo, ffn2_norm[layer])
    a, wd_bf = _ffn_up(xn, ffn2_w_gate[layer], ffn2_w_up[layer], ffn2_w_down[layer])
    out = _ffn_down(a, h, wd_bf, final_norm, final=True)[0]
    return out.reshape(batch, seq, d)
```

```python
import functools

import jax
import jax.numpy as jnp
import numpy as np
from jax.experimental import pallas as pl
from jax.experimental.pallas import tpu as pltpu

F32 = jnp.float32
BF16 = jnp.bfloat16

EPS = 1e-6
ROPE_THETA = 10000.0
FFN_RES_SCALE = 0.5
HEAD_DIM = 64
BLOCK = 128
DILATED_BRANCHES = ((128, 1), (512, 4), (2048, 16))
SGU_CHUNK = 128
SGU_GROUP_DIM = 128
N_MEM_HEADS = 4
LANES = 128
PROJ_ROWS = 128
MEM_ROWS = 256
MIX_ROWS = 256
CAST_ROWS = 256

VMEM_LIMIT = 56 * 1024 * 1024


def _params(n_axes, vmem=VMEM_LIMIT):
    return pltpu.CompilerParams(
        dimension_semantics=("arbitrary",) * n_axes, vmem_limit_bytes=vmem)


def _resident(shape):
    return pl.BlockSpec(shape, lambda *_: (0,) * len(shape), pipeline_mode=pl.Buffered(1))


def _rms(x, g):
    return x * jax.lax.rsqrt(jnp.mean(x * x, axis=-1, keepdims=True) + EPS) * g


def _gelu(x):
    c = np.sqrt(2.0 / np.pi).astype(np.float32)
    return 0.5 * x * (1.0 + jnp.tanh(c * (x + 0.044715 * (x * x * x))))


def _cast_to_bf16(src_ref, dst_ref):
    rows = src_ref.shape[0]
    chunk = min(CAST_ROWS, rows)

    def body(i, carry):
        sl = pl.ds(pl.multiple_of(i * chunk, chunk), chunk)
        dst_ref[sl, :] = src_ref[sl, :].astype(BF16)
        return carry

    jax.lax.fori_loop(0, rows // chunk, body, 0)


def _prep_kernel(pos_ref, freq_ref, x_ref, g_ref, cos_ref, sin_ref, xn_ref):
    ang = pos_ref[...].astype(F32) * freq_ref[...]
    lane = jax.lax.broadcasted_iota(jnp.int32, ang.shape, 1)
    sign = jnp.where(lane % HEAD_DIM < HEAD_DIM // 2, -1.0, 1.0)
    cos_ref[...] = jnp.cos(ang)
    sin_ref[...] = jnp.sin(ang) * sign
    xn_ref[...] = _rms(x_ref[...], g_ref[...]).astype(BF16)


def _prep(positions, x, g, *, tm=1024):
    seq, d = x.shape
    half = HEAD_DIM // 2
    inv_freq = ROPE_THETA ** (-jnp.arange(0, HEAD_DIM, 2, dtype=F32) / HEAD_DIM)
    freq_lane = jnp.tile(inv_freq, LANES // half).reshape(1, LANES)
    pos = positions.reshape(seq, 1)
    row = lambda w: pl.BlockSpec((tm, w), lambda i: (i, 0))
    const = lambda w: pl.BlockSpec((1, w), lambda i: (0, 0))
    return pl.pallas_call(
        _prep_kernel,
        grid=(seq // tm,),
        in_specs=[row(1), const(LANES), row(d), const(d)],
        out_specs=[row(LANES), row(LANES), row(d)],
        out_shape=[jax.ShapeDtypeStruct((seq, LANES), F32)] * 2
        + [jax.ShapeDtypeStruct((seq, d), BF16)],
        compiler_params=_params(1),
        name="prep",
    )(pos, freq_lane, x, g.reshape(1, d))


def _ffn_up_kernel(xn_ref, wg_ref, wu_ref, wd_ref, a_ref, wdb_ref, wgb, wub):
    @pl.when(pl.program_id(1) == 0)
    def _():
        _cast_to_bf16(wg_ref, wgb)
        _cast_to_bf16(wu_ref, wub)
        _cast_to_bf16(wd_ref, wdb_ref)

    xn = xn_ref[...]
    gate = jnp.dot(xn, wgb[...], preferred_element_type=F32)
    up = jnp.dot(xn, wub[...], preferred_element_type=F32)
    a_ref[...] = ((gate * (FFN_RES_SCALE / (1.0 + jnp.exp(-gate)))) * up).astype(BF16)


def _ffn_up(xn, wg, wu, wd, *, tm=1024, tf=512):
    seq, d = xn.shape
    dff = wg.shape[1]
    return pl.pallas_call(
        _ffn_up_kernel,
        grid=(dff // tf, seq // tm),
        in_specs=[pl.BlockSpec((tm, d), lambda f, m: (m, 0)),
                  pl.BlockSpec((d, tf), lambda f, m: (0, f)),
                  pl.BlockSpec((d, tf), lambda f, m: (0, f)),
                  pl.BlockSpec((tf, d), lambda f, m: (f, 0))],
        out_specs=[pl.BlockSpec((tm, tf), lambda f, m: (m, f)),
                   pl.BlockSpec((tf, d), lambda f, m: (f, 0))],
        out_shape=[jax.ShapeDtypeStruct((seq, dff), BF16),
                   jax.ShapeDtypeStruct((dff, d), BF16)],
        scratch_shapes=[pltpu.VMEM((d, tf), BF16), pltpu.VMEM((d, tf), BF16)],
        compiler_params=_params(2),
        name="ffn_up",
    )(xn, wg, wu, wd)


def _ffn_down_kernel(*refs, final, n_cast):
    a_ref, x_ref, wd_ref, g_ref = refs[:4]
    cast_in = refs[4:4 + n_cast]
    out_refs = refs[4 + n_cast:]
    h = x_ref[...] + jnp.dot(a_ref[...], wd_ref[...], preferred_element_type=F32)
    if final:
        out_refs[0][...] = _rms(h, g_ref[...])
    else:
        out_refs[0][...] = h
        out_refs[1][...] = _rms(h, g_ref[...]).astype(BF16)
        for src, dst in zip(cast_in, out_refs[2:]):
            dst[...] = src[...].astype(BF16)


def _ffn_down(a, x, wd_bf, g, cast_weights=(), *, final, tm=256):
    seq, d = x.shape
    dff = a.shape[1]
    steps = seq // tm
    row = pl.BlockSpec((tm, d), lambda m: (m, 0))
    out_specs = [row] if final else [row, row]
    out_shape = [jax.ShapeDtypeStruct((seq, d), F32)]
    if not final:
        out_shape.append(jax.ShapeDtypeStruct((seq, d), BF16))
    cast_specs = []
    for w in cast_weights:
        assert not final and w.shape[0] % steps == 0
        spec = pl.BlockSpec((w.shape[0] // steps, w.shape[1]), lambda m: (m, 0))
        cast_specs.append(spec)
        out_specs.append(spec)
        out_shape.append(jax.ShapeDtypeStruct(w.shape, BF16))
    return pl.pallas_call(
        functools.partial(_ffn_down_kernel, final=final, n_cast=len(cast_weights)),
        grid=(steps,),
        in_specs=[pl.BlockSpec((tm, dff), lambda m: (m, 0)), row,
                  _resident((dff, d)), pl.BlockSpec((1, d), lambda m: (0, 0))] + cast_specs,
        out_specs=out_specs,
        out_shape=out_shape,
        compiler_params=_params(1),
        name="ffn_down_final" if final else "ffn_down",
    )(a, x, wd_bf, g.reshape(1, d), *cast_weights)


def _proj_kernel(xn_ref, w_ref, cos_ref, sin_ref, lng_ref, lnb_ref, o_ref, wb, *, q_scale):
    n = pl.program_id(0)
    width = o_ref.shape[1]

    @pl.when(pl.program_id(1) == 0)
    def _():
        _cast_to_bf16(w_ref, wb)

    def chunked(epilogue):
        for r0 in range(0, xn_ref.shape[0], PROJ_ROWS):
            rows = slice(r0, r0 + PROJ_ROWS)
            y = jnp.dot(xn_ref[rows, :], wb[...], preferred_element_type=F32)
            epilogue(y, rows)

    def rope(scale):
        def epilogue(y, rows):
            cos = cos_ref[rows, :]
            sin = sin_ref[rows, :]
            lane = jax.lax.broadcasted_iota(jnp.int32, cos.shape, 1)
            first = lane % HEAD_DIM < HEAD_DIM // 2
            for j in range(width // LANES):
                blk = y[:, j * LANES:(j + 1) * LANES]
                partner = jnp.where(first,
                                    pltpu.roll(blk, LANES - HEAD_DIM // 2, 1),
                                    pltpu.roll(blk, HEAD_DIM // 2, 1))
                out = blk * cos + partner * sin
                if scale != 1.0:
                    out = out * scale
                o_ref[rows, j * LANES:(j + 1) * LANES] = out.astype(o_ref.dtype)
        return epilogue

    def store(fn):
        def epilogue(y, rows):
            o_ref[rows, :] = fn(y).astype(o_ref.dtype)
        return epilogue

    def layernorm(y):
        t = _gelu(y)
        mu = jnp.mean(t, axis=-1, keepdims=True)
        tc = t - mu
        t = tc * jax.lax.rsqrt(jnp.mean(tc * tc, axis=-1, keepdims=True) + EPS)
        return t * lng_ref[...] + lnb_ref[...]

    @pl.when(n == 0)
    def _():
        chunked(rope(q_scale))

    @pl.when(n == 1)
    def _():
        chunked(rope(1.0))

    @pl.when(n == 2)
    def _():
        chunked(store(lambda y: y))

    @pl.when(n == 3)
    def _():
        chunked(store(_gelu))

    @pl.when(n == 4)
    def _():
        chunked(store(layernorm))


def _proj(xn, w_in, cos, sin, ln_g, ln_b, *, width, tm=1024):
    seq, d = xn.shape
    d_in = w_in.shape[1]
    assert d_in == 5 * width
    return pl.pallas_call(
        functools.partial(_proj_kernel, q_scale=HEAD_DIM ** -0.5 * np.log2(np.e)),
        grid=(d_in // width, seq // tm),
        in_specs=[
            pl.BlockSpec((tm, d), lambda n, m: (m, 0)),
            pl.BlockSpec((d, width), lambda n, m: (0, n)),
            pl.BlockSpec((tm, LANES), lambda n, m: (m, 0)),
            pl.BlockSpec((tm, LANES), lambda n, m: (m, 0)),
            pl.BlockSpec((1, width), lambda n, m: (0, 0)),
            pl.BlockSpec((1, width), lambda n, m: (0, 0)),
        ],
        out_specs=pl.BlockSpec((tm, width), lambda n, m: (m, n)),
        out_shape=jax.ShapeDtypeStruct((seq, d_in), BF16),
        scratch_shapes=[pltpu.VMEM((d, width), BF16)],
        compiler_params=_params(2),
        name="proj",
    )(xn, w_in, cos, sin, ln_g.reshape(1, width), ln_b.reshape(1, width))


def _attn_kernel(q_ref, kc_ref, kp_ref, vc_ref, vp_ref, o_ref,
                 stage, q4, k4, v4, kb, vb, m4, l4, acc4, m_s, l_s, acc_s, bias, *, dil):
    t = pl.program_id(0)
    sup = q_ref.shape[0]
    sub = sup // dil
    n_units = sup // BLOCK

    def split(dst, n_rows):
        part = n_rows // dil
        for b in range(dil):
            dst[b * part:(b + 1) * part, :] = stage[pl.ds(b, part, stride=dil), :]

    stage[:sup] = kp_ref[...].astype(F32)
    stage[sup:] = kc_ref[...].astype(F32)
    split(k4, 2 * sup)
    stage[:sup] = vp_ref[...].astype(F32)
    stage[sup:] = vc_ref[...].astype(F32)
    split(v4, 2 * sup)
    stage[:sup] = q_ref[...].astype(F32)
    split(q4, sup)
    kb[:BLOCK] = kp_ref[sup - BLOCK:, :]
    kb[BLOCK:] = kc_ref[...]
    vb[:BLOCK] = vp_ref[sup - BLOCK:, :]
    vb[BLOCK:] = vc_ref[...]

    qi = jax.lax.broadcasted_iota(jnp.int32, (BLOCK, 2 * BLOCK), 0)
    kj = jax.lax.broadcasted_iota(jnp.int32, (BLOCK, 2 * BLOCK), 1)
    diff = qi + BLOCK - kj
    band = (diff >= 0) & (diff <= BLOCK)
    bias[0] = jnp.where(band & (kj >= BLOCK), 0.0, -jnp.inf)
    bias[1] = jnp.where(band, 0.0, -jnp.inf)
    low = jax.lax.broadcasted_iota(jnp.int32, (BLOCK, LANES), 1) < HEAD_DIM

    def head_pair(q, k, v, has_prev):
        mask = bias[has_prev.astype(jnp.int32)]
        ms, ls, os = [], [], []
        for hh in range(2):
            qm = jnp.where(low if hh == 0 else ~low, q, jnp.zeros_like(q))
            s = jax.lax.dot_general(qm, k, (((1,), (1,)), ((), ())),
                                    preferred_element_type=F32)
            s = s + mask
            m = jnp.max(s, axis=-1, keepdims=True)
            e = jnp.exp2(s - m)
            ms.append(m)
            ls.append(jnp.sum(e, axis=-1, keepdims=True))
            os.append(jnp.dot(e.astype(BF16), v, preferred_element_type=F32))
        return (jnp.where(low, ms[0], ms[1]), jnp.where(low, ls[0], ls[1]),
                jnp.where(low, os[0], os[1]))

    def merge(refs, rows, stats):
        m_ref, l_ref, acc_ref = refs
        m_u, l_u, o_u = stats
        m_old = m_ref[rows, :]
        m_new = jnp.maximum(m_old, m_u)
        a_old = jnp.exp2(m_old - m_new)
        a_u = jnp.exp2(m_u - m_new)
        m_ref[rows, :] = m_new
        l_ref[rows, :] = a_old * l_ref[rows, :] + a_u * l_u
        acc_ref[rows, :] = a_old * acc_ref[rows, :] + a_u * o_u

    def mid_unit(u, carry):
        r = u // (sub // BLOCK)
        n = u % (sub // BLOCK)
        rows = pl.ds(pl.multiple_of(r * sub + n * BLOCK, BLOCK), BLOCK)
        kv_rows = pl.ds(pl.multiple_of(r * 2 * sub + sub + (n - 1) * BLOCK, BLOCK), 2 * BLOCK)
        m_u, l_u, o_u = head_pair(q4[rows, :].astype(BF16), k4[kv_rows, :].astype(BF16),
                                  v4[kv_rows, :].astype(BF16), (t > 0) | (n > 0))
        m4[rows, :] = m_u
        l4[rows, :] = l_u
        acc4[rows, :] = o_u
        return carry

    def wide_unit(r, carry):
        a = r // dil
        b = r % dil
        rows = pl.ds(b * sub + a, BLOCK, stride=dil)
        kv_rows = pl.ds(b * 2 * sub + a, 2 * BLOCK, stride=dil)
        stats = head_pair(q4[rows, :].astype(BF16), k4[kv_rows, :].astype(BF16),
                          v4[kv_rows, :].astype(BF16), t > 0)
        merge((m4, l4, acc4), rows, stats)
        return carry

    def near_unit(n, carry):
        rows = pl.ds(pl.multiple_of(n * BLOCK, BLOCK), BLOCK)
        kv_rows = pl.ds(pl.multiple_of(n * BLOCK, BLOCK), 2 * BLOCK)
        stats = head_pair(q_ref[rows, :], kb[kv_rows, :], vb[kv_rows, :], (t > 0) | (n > 0))
        merge((m_s, l_s, acc_s), rows, stats)
        return carry

    jax.lax.fori_loop(0, n_units, mid_unit, 0, unroll=True)
    jax.lax.fori_loop(0, n_units, wide_unit, 0, unroll=True)
    for b in range(dil):
        src = slice(b * sub, (b + 1) * sub)
        dst = pl.ds(b, sub, stride=dil)
        m_s[dst, :] = m4[src, :]
        l_s[dst, :] = l4[src, :]
        acc_s[dst, :] = acc4[src, :]
    jax.lax.fori_loop(0, n_units, near_unit, 0, unroll=True)

    o_ref[...] = (acc_s[...] / l_s[...]).astype(o_ref.dtype)


def _attention(proj, *, width):
    seq, d_in = proj.shape
    dilations = tuple(d for _, d in DILATED_BRANCHES)
    steps = {w // d for w, d in DILATED_BRANCHES}
    assert steps == {BLOCK}, "every branch must span exactly one previous block"
    dil = dilations[1]
    assert dilations == (1, dil, dil * dil)
    sup = BLOCK * dil * dil
    seg = width // LANES
    cur = lambda i: pl.BlockSpec((sup, LANES), lambda t, p: (t, i * seg + p))
    prev = lambda i: pl.BlockSpec((sup, LANES),
                                  lambda t, p: (jnp.maximum(t - 1, 0), i * seg + p))
    f32_rows = lambda n: pltpu.VMEM((n, LANES), F32)
    return pl.pallas_call(
        functools.partial(_attn_kernel, dil=dil),
        grid=(seq // sup, seg),
        in_specs=[cur(0), cur(1), prev(1), cur(2), prev(2)],
        out_specs=pl.BlockSpec((sup, LANES), lambda t, p: (t, p)),
        out_shape=jax.ShapeDtypeStruct((seq, width), BF16),
        scratch_shapes=[f32_rows(2 * sup),
                        f32_rows(sup), f32_rows(2 * sup), f32_rows(2 * sup),
                        pltpu.VMEM((sup + BLOCK, LANES), BF16),
                        pltpu.VMEM((sup + BLOCK, LANES), BF16),
                        f32_rows(sup), f32_rows(sup), f32_rows(sup),
                        f32_rows(sup), f32_rows(sup), f32_rows(sup),
                        pltpu.VMEM((2, BLOCK, 2 * BLOCK), F32)],
        compiler_params=_params(2),
        name="dilated_attn",
    )(proj, proj, proj, proj, proj)


def _mix_kernel(h_ref, attn_ref, u_ref, gv_ref, ws_ref, bs_ref, ag_ref, sg_ref, wo_ref, ng_ref,
                out_ref, xn_ref):
    tm = h_ref.shape[0]
    width = u_ref.shape[1]
    n_grp = width // SGU_GROUP_DIM
    ci = jax.lax.broadcasted_iota(jnp.int32, (SGU_CHUNK, SGU_CHUNK), 0)
    cj = jax.lax.broadcasted_iota(jnp.int32, (SGU_CHUNK, SGU_CHUNK), 1)
    causal = cj <= ci
    w_tril = [jnp.where(causal, ws_ref[g], 0.0).astype(BF16) for g in range(n_grp)]
    for r0 in range(0, tm, MIX_ROWS):
        rs = slice(r0, r0 + MIX_ROWS)
        attn = _rms(attn_ref[rs, :].astype(F32), ag_ref[...]).astype(BF16)
        gated = []
        for c0 in range(r0, r0 + MIX_ROWS, SGU_CHUNK):
            cs = slice(c0, c0 + SGU_CHUNK)
            groups = []
            for g in range(n_grp):
                gs = slice(g * SGU_GROUP_DIM, (g + 1) * SGU_GROUP_DIM)
                sv = jnp.dot(w_tril[g], gv_ref[cs, gs], preferred_element_type=F32)
                groups.append(u_ref[cs, gs].astype(F32) * (sv + bs_ref[:, g:g + 1]))
            gated.append(jnp.concatenate(groups, axis=1))
        sgu = _rms(jnp.concatenate(gated, axis=0), sg_ref[...]).astype(BF16)
        mixed = jnp.concatenate([attn, sgu], axis=1)
        h = h_ref[rs, :] + jnp.dot(mixed, wo_ref[...], preferred_element_type=F32)
        out_ref[rs, :] = h
        xn_ref[rs, :] = _rms(h, ng_ref[...]).astype(BF16)


def _mix(h, attn, proj, w_s, b_s, attn_g, sgu_g, w_out, next_g, *, width, tm=512):
    seq, d = h.shape
    n_grp = w_s.shape[0]
    row = lambda m: (m, 0)
    const2 = lambda m: (0, 0)
    return pl.pallas_call(
        _mix_kernel,
        grid=(seq // tm,),
        in_specs=[pl.BlockSpec((tm, d), row),
                  pl.BlockSpec((tm, width), row),
                  pl.BlockSpec((tm, width), lambda m: (m, 3)),
                  pl.BlockSpec((tm, width), lambda m: (m, 4)),
                  pl.BlockSpec((n_grp, SGU_CHUNK, SGU_CHUNK), lambda m: (0, 0, 0)),
                  pl.BlockSpec((SGU_CHUNK, n_grp), const2),
                  pl.BlockSpec((1, width), const2),
                  pl.BlockSpec((1, width), const2),
                  _resident((2 * width, d)),
                  pl.BlockSpec((1, d), const2)],
        out_specs=[pl.BlockSpec((tm, d), row), pl.BlockSpec((tm, d), row)],
        out_shape=[jax.ShapeDtypeStruct((seq, d), F32), jax.ShapeDtypeStruct((seq, d), BF16)],
        compiler_params=_params(1),
        name="mix_out",
    )(h, attn, proj, proj, w_s, b_s.T, attn_g.reshape(1, width), sgu_g.reshape(1, width),
      w_out, next_g.reshape(1, d))


def _mem_fold_kernel(mem_ref, g_ref, wk_ref, wv_ref, wq_ref, wo_ref, kq_ref, vo_ref):
    hd = wk_ref.shape[1]
    mk = _rms(mem_ref[...], g_ref[...]).astype(BF16)
    k = jnp.dot(mk, wk_ref[...].astype(BF16), preferred_element_type=F32).astype(BF16)
    v = jnp.dot(mk, wv_ref[...].astype(BF16), preferred_element_type=F32).astype(BF16)
    kq = jax.lax.dot_general(wq_ref[...].astype(BF16), k, (((1,), (1,)), ((), ())),
                             preferred_element_type=F32)
    kq_ref[...] = (kq * (hd ** -0.5)).astype(BF16)
    vo_ref[...] = jnp.dot(v, wo_ref[...].astype(BF16), preferred_element_type=F32).astype(BF16)


def _mem_fold(mem, g, w_k, w_v, w_q, w_o):
    n_mem, d = mem.shape
    hd = d // N_MEM_HEADS
    cols = pl.BlockSpec((d, hd), lambda h: (0, h))
    return pl.pallas_call(
        _mem_fold_kernel,
        grid=(N_MEM_HEADS,),
        in_specs=[pl.BlockSpec((n_mem, d), lambda h: (0, 0)),
                  pl.BlockSpec((1, d), lambda h: (0, 0)),
                  cols, cols, cols,
                  pl.BlockSpec((hd, d), lambda h: (h, 0))],
        out_specs=[pl.BlockSpec((d, n_mem), lambda h: (0, h)),
                   pl.BlockSpec((n_mem, d), lambda h: (h, 0))],
        out_shape=[jax.ShapeDtypeStruct((d, N_MEM_HEADS * n_mem), BF16),
                   jax.ShapeDtypeStruct((N_MEM_HEADS * n_mem, d), BF16)],
        compiler_params=_params(1),
        name="mem_fold",
    )(mem, g.reshape(1, d), w_k, w_v, w_q, w_o)


def _mem_cross_kernel(h_ref, xq_ref, kq_ref, vo_ref, ng_ref, out_ref, xn_ref):
    n_mem = kq_ref.shape[1] // N_MEM_HEADS
    for r0 in range(0, h_ref.shape[0], MEM_ROWS):
        rs = slice(r0, r0 + MEM_ROWS)
        s = jnp.dot(xq_ref[rs, :], kq_ref[...], preferred_element_type=F32)
        probs = []
        for i in range(N_MEM_HEADS):
            sh = s[:, i * n_mem:(i + 1) * n_mem]
            e = jnp.exp(sh - jnp.max(sh, axis=-1, keepdims=True))
            probs.append((e / jnp.sum(e, axis=-1, keepdims=True)).astype(BF16))
        p = jnp.concatenate(probs, axis=1)
        h = h_ref[rs, :] + jnp.dot(p, vo_ref[...], preferred_element_type=F32)
        out_ref[rs, :] = h
        xn_ref[rs, :] = _rms(h, ng_ref[...]).astype(BF16)


def _mem_cross(h, xq, kq, vo, next_g, *, tm=512):
    seq, d = h.shape
    row = pl.BlockSpec((tm, d), lambda m: (m, 0))
    return pl.pallas_call(
        _mem_cross_kernel,
        grid=(seq // tm,),
        in_specs=[row, row, _resident(kq.shape), _resident(vo.shape),
                  pl.BlockSpec((1, d), lambda m: (0, 0))],
        out_specs=[row, row],
        out_shape=[jax.ShapeDtypeStruct((seq, d), F32), jax.ShapeDtypeStruct((seq, d), BF16)],
        compiler_params=_params(1),
        name="mem_cross",
    )(h, xq, kq, vo, next_g.reshape(1, d))


def kernel(x, mem, positions, ffn1_norm, ffn1_w_gate, ffn1_w_up, ffn1_w_down, mix_norm, w_in,
           sgu_ln_gain, sgu_ln_bias, sgu_w_s, sgu_b_s, attn_out_gain, sgu_out_gain, w_out,
           mem_q_norm, mem_kv_norm, mem_w_q, mem_w_k, mem_w_v, mem_w_o, ffn2_norm,
           ffn2_w_gate, ffn2_w_up, ffn2_w_down, final_norm):
    batch, seq, d = x.shape
    depth = ffn1_norm.shape[0]
    width = attn_out_gain.shape[1]
    assert batch == 1 and mem.shape[0] == 1 and depth == 1
    layer = 0

    h = x.reshape(seq, d)
    mem2 = mem.reshape(mem.shape[1], d)

    cos, sin, xn = _prep(positions, h, ffn1_norm[layer])
    a, wd_bf = _ffn_up(xn, ffn1_w_gate[layer], ffn1_w_up[layer], ffn1_w_down[layer])
    h, xn, w_out_bf = _ffn_down(a, h, wd_bf, mix_norm[layer], (w_out[layer],), final=False)
    proj = _proj(xn, w_in[layer], cos, sin, sgu_ln_gain[layer], sgu_ln_bias[layer], width=width)
    attn = _attention(proj, width=width)
    h, xn = _mix(h, attn, proj, sgu_w_s[layer], sgu_b_s[layer], attn_out_gain[layer],
                 sgu_out_gain[layer], w_out_bf, mem_q_norm[layer], width=width)
    kq, vo = _mem_fold(mem2, mem_kv_norm[layer], mem_w_k[layer], mem_w_v[layer],
                       mem_w_q[layer], mem_w_o[layer])
    h, xn = _mem_cross(h, xn, kq, vo, ffn2_norm[layer])
    a, wd_bf = _ffn_up(xn, ffn2_w_gate[layer], ffn2_w_up[layer], ffn2_w_down[layer])
    out = _ffn_down(a, h, wd_bf, final_norm, final=True)[0]
    return out.reshape(batch, seq, d)
```

```python
import functools

import jax
import jax.numpy as jnp
import numpy as np
from jax.experimental import pallas as pl
from jax.experimental.pallas import tpu as pltpu

F32 = jnp.float32
BF16 = jnp.bfloat16

EPS = 1e-6
ROPE_THETA = 10000.0
FFN_RES_SCALE = 0.5
HEAD_DIM = 64
BLOCK = 128
DILATED_BRANCHES = ((128, 1), (512, 4), (2048, 16))
SGU_CHUNK = 128
SGU_GROUP_DIM = 128
N_MEM_HEADS = 4
LANES = 128
MXU_COLS = 256
CAST_ROWS = 256
FFN_UP_ROWS = 1024

VMEM_LIMIT = 56 * 1024 * 1024


def _params(n_axes, vmem=VMEM_LIMIT):
    return pltpu.CompilerParams(
        dimension_semantics=("arbitrary",) * n_axes, vmem_limit_bytes=vmem)


def _resident(shape):
    return pl.BlockSpec(shape, lambda *_: (0,) * len(shape), pipeline_mode=pl.Buffered(1))


def _rms(x, g):
    return x * jax.lax.rsqrt(jnp.mean(x * x, axis=-1, keepdims=True) + EPS) * g


def _gelu(x):
    c = np.sqrt(2.0 / np.pi).astype(np.float32)
    return 0.5 * x * (1.0 + jnp.tanh(c * (x + 0.044715 * (x * x * x))))


def _cast_to_bf16(src_ref, dst_ref):
    rows = src_ref.shape[0]
    chunk = min(CAST_ROWS, rows)

    def body(i, carry):
        sl = pl.ds(pl.multiple_of(i * chunk, chunk), chunk)
        dst_ref[sl, :] = src_ref[sl, :].astype(BF16)
        return carry

    jax.lax.fori_loop(0, rows // chunk, body, 0)


def _prep_kernel(pos_ref, freq_ref, x_ref, g_ref, cos_ref, sin_ref, xn_ref):
    ang = pos_ref[...].astype(F32) * freq_ref[...]
    lane = jax.lax.broadcasted_iota(jnp.int32, ang.shape, 1)
    sign = jnp.where(lane % HEAD_DIM < HEAD_DIM // 2, -1.0, 1.0)
    cos_ref[...] = jnp.cos(ang)
    sin_ref[...] = jnp.sin(ang) * sign
    xn_ref[...] = _rms(x_ref[...], g_ref[...]).astype(BF16)


def _prep(positions, x, g, *, tm=1024):
    seq, d = x.shape
    half = HEAD_DIM // 2
    inv_freq = ROPE_THETA ** (-jnp.arange(0, HEAD_DIM, 2, dtype=F32) / HEAD_DIM)
    freq_lane = jnp.tile(inv_freq, LANES // half).reshape(1, LANES)
    pos = positions.reshape(seq, 1)
    row = lambda w: pl.BlockSpec((tm, w), lambda i: (i, 0))
    const = lambda w: pl.BlockSpec((1, w), lambda i: (0, 0))
    return pl.pallas_call(
        _prep_kernel,
        grid=(seq // tm,),
        in_specs=[row(1), const(LANES), row(d), const(d)],
        out_specs=[row(LANES), row(LANES), row(d)],
        out_shape=[jax.ShapeDtypeStruct((seq, LANES), F32)] * 2
        + [jax.ShapeDtypeStruct((seq, d), BF16)],
        compiler_params=_params(1),
        name="prep",
    )(pos, freq_lane, x, g.reshape(1, d))


def _ffn_up_kernel(*refs, precast):
    if precast:
        xn_ref, wgb, wub, a_ref = refs
    else:
        xn_ref, wg_ref, wu_ref, wd_ref, a_ref, wdb_ref, wgb, wub = refs

        @pl.when(pl.program_id(1) == 0)
        def _():
            _cast_to_bf16(wg_ref, wgb)
            _cast_to_bf16(wu_ref, wub)
            _cast_to_bf16(wd_ref, wdb_ref)

    for r0 in range(0, xn_ref.shape[0], FFN_UP_ROWS):
        rows = slice(r0, r0 + FFN_UP_ROWS)
        xn = xn_ref[rows, :]
        gate = jnp.dot(xn, wgb[...], preferred_element_type=F32)
        up = jnp.dot(xn, wub[...], preferred_element_type=F32)
        a_ref[rows, :] = ((gate * (FFN_RES_SCALE / (1.0 + jnp.exp(-gate)))) * up).astype(BF16)


def _ffn_up(xn, wg, wu, wd=None, *, tf=512):
    seq, d = xn.shape
    dff = wg.shape[1]
    precast = wd is None
    tm = FFN_UP_ROWS * (2 if precast else 1)
    in_specs = [pl.BlockSpec((tm, d), lambda f, m: (m, 0)),
                pl.BlockSpec((d, tf), lambda f, m: (0, f)),
                pl.BlockSpec((d, tf), lambda f, m: (0, f))]
    out_specs = [pl.BlockSpec((tm, tf), lambda f, m: (m, f))]
    out_shape = [jax.ShapeDtypeStruct((seq, dff), BF16)]
    scratch, args = [], [xn, wg, wu]
    if not precast:
        in_specs.append(pl.BlockSpec((tf, d), lambda f, m: (f, 0)))
        out_specs.append(pl.BlockSpec((tf, d), lambda f, m: (f, 0)))
        out_shape.append(jax.ShapeDtypeStruct((dff, d), BF16))
        scratch = [pltpu.VMEM((d, tf), BF16), pltpu.VMEM((d, tf), BF16)]
        args.append(wd)
    return pl.pallas_call(
        functools.partial(_ffn_up_kernel, precast=precast),
        grid=(dff // tf, seq // tm),
        in_specs=in_specs,
        out_specs=out_specs,
        out_shape=out_shape,
        scratch_shapes=scratch,
        compiler_params=_params(2),
        name="ffn_up_bf16" if precast else "ffn_up",
    )(*args)


def _ffn_down_kernel(*refs, final, n_cast):
    a_ref, x_ref, wd_ref, g_ref = refs[:4]
    cast_in = refs[4:4 + n_cast]
    out_refs = refs[4 + n_cast:]
    h = x_ref[...] + jnp.dot(a_ref[...], wd_ref[...], preferred_element_type=F32)
    if final:
        out_refs[0][...] = _rms(h, g_ref[...])
    else:
        out_refs[0][...] = h
        out_refs[1][...] = _rms(h, g_ref[...]).astype(BF16)
        for src, dst in zip(cast_in, out_refs[2:]):
            dst[...] = src[...].astype(BF16)


def _ffn_down(a, x, wd_bf, g, cast_weights=(), *, final, tm=256):
    seq, d = x.shape
    dff = a.shape[1]
    steps = seq // tm
    row = pl.BlockSpec((tm, d), lambda m: (m, 0))
    out_specs = [row] if final else [row, row]
    out_shape = [jax.ShapeDtypeStruct((seq, d), F32)]
    if not final:
        out_shape.append(jax.ShapeDtypeStruct((seq, d), BF16))
    cast_specs = []
    for w in cast_weights:
        assert not final and w.shape[0] % steps == 0
        spec = pl.BlockSpec((w.shape[0] // steps, w.shape[1]), lambda m: (m, 0))
        cast_specs.append(spec)
        out_specs.append(spec)
        out_shape.append(jax.ShapeDtypeStruct(w.shape, BF16))
    return pl.pallas_call(
        functools.partial(_ffn_down_kernel, final=final, n_cast=len(cast_weights)),
        grid=(steps,),
        in_specs=[pl.BlockSpec((tm, dff), lambda m: (m, 0)), row,
                  _resident((dff, d)), pl.BlockSpec((1, d), lambda m: (0, 0))] + cast_specs,
        out_specs=out_specs,
        out_shape=out_shape,
        compiler_params=_params(1),
        name="ffn_down_final" if final else "ffn_down",
    )(a, x, wd_bf, g.reshape(1, d), *cast_weights)


def _proj_kernel(*refs, q_scale, n_cast, cast_steps):
    xn_ref, w_ref, cos_ref, sin_ref, lng_ref, lnb_ref = refs[:6]
    cast_in = refs[6:6 + n_cast]
    o_ref = refs[6 + n_cast]
    cast_out = refs[7 + n_cast:7 + 2 * n_cast]
    wb, t_ref = refs[7 + 2 * n_cast:]
    n = pl.program_id(0)
    width = o_ref.shape[1]

    @pl.when(n * pl.num_programs(1) + pl.program_id(1) < cast_steps)
    def _():
        for src, dst in zip(cast_in, cast_out):
            dst[...] = src[...].astype(BF16)

    @pl.when(pl.program_id(1) == 0)
    def _():
        _cast_to_bf16(w_ref, wb)

    half = xn_ref.shape[0] // 2

    def chunked(epilogue):
        for j in range(width // MXU_COLS):
            c0 = j * MXU_COLS
            for r0 in (0, half):
                rows = slice(r0, r0 + half)
                y = jnp.dot(xn_ref[rows, :], wb[:, c0:c0 + MXU_COLS],
                            preferred_element_type=F32)
                epilogue(y, rows, c0)

    def rope(scale):
        def epilogue(y, rows, c0):
            cos = cos_ref[rows, :]
            sin = sin_ref[rows, :]
            lane = jax.lax.broadcasted_iota(jnp.int32, cos.shape, 1)
            first = lane % HEAD_DIM < HEAD_DIM // 2
            for j in range(MXU_COLS // LANES):
                blk = y[:, j * LANES:(j + 1) * LANES]
                partner = jnp.where(first,
                                    pltpu.roll(blk, LANES - HEAD_DIM // 2, 1),
                                    pltpu.roll(blk, HEAD_DIM // 2, 1))
                out = blk * cos + partner * sin
                if scale != 1.0:
                    out = out * scale
                o_ref[rows, c0 + j * LANES:c0 + (j + 1) * LANES] = out.astype(o_ref.dtype)
        return epilogue

    def store(fn):
        def epilogue(y, rows, c0):
            o_ref[rows, c0:c0 + MXU_COLS] = fn(y).astype(o_ref.dtype)
        return epilogue

    @pl.when(n == 0)
    def _():
        chunked(rope(q_scale))

    @pl.when(n == 1)
    def _():
        chunked(rope(1.0))

    @pl.when(n == 2)
    def _():
        chunked(store(lambda y: y))

    @pl.when(n == 3)
    def _():
        chunked(store(_gelu))

    @pl.when(n == 4)
    def _():
        sums = {0: [], half: []}

        def epilogue(y, rows, c0):
            t = _gelu(y)
            t_ref[rows, c0:c0 + MXU_COLS] = t
            sums[rows.start].append(jnp.sum(t, axis=-1, keepdims=True))

        chunked(epilogue)
        for r0 in (0, half):
            rows = slice(r0, r0 + half)
            mu = sum(sums[r0]) * (1.0 / width)
            tc = t_ref[rows, :] - mu
            t = tc * jax.lax.rsqrt(jnp.mean(tc * tc, axis=-1, keepdims=True) + EPS)
            o_ref[rows, :] = (t * lng_ref[...] + lnb_ref[...]).astype(o_ref.dtype)


def _proj(xn, w_in, cos, sin, ln_g, ln_b, cast_weights=(), *, width, tm=1024, cast_steps=32):
    seq, d = xn.shape
    d_in = w_in.shape[1]
    assert d_in == 5 * width
    m_steps = seq // tm
    assert cast_steps <= (d_in // width) * m_steps
    cast_specs, cast_shapes = [], []
    for w in cast_weights:
        assert w.shape[0] % cast_steps == 0
        cast_specs.append(pl.BlockSpec(
            (w.shape[0] // cast_steps, w.shape[1]),
            lambda n, m: (jnp.minimum(n * m_steps + m, cast_steps - 1), 0)))
        cast_shapes.append(jax.ShapeDtypeStruct(w.shape, BF16))
    return pl.pallas_call(
        functools.partial(_proj_kernel, q_scale=HEAD_DIM ** -0.5 * np.log2(np.e),
                          n_cast=len(cast_weights), cast_steps=cast_steps),
        grid=(d_in // width, m_steps),
        in_specs=[
            pl.BlockSpec((tm, d), lambda n, m: (m, 0)),
            pl.BlockSpec((d, width), lambda n, m: (0, n)),
            pl.BlockSpec((tm, LANES), lambda n, m: (m, 0)),
            pl.BlockSpec((tm, LANES), lambda n, m: (m, 0)),
            pl.BlockSpec((1, width), lambda n, m: (0, 0)),
            pl.BlockSpec((1, width), lambda n, m: (0, 0)),
        ] + cast_specs,
        out_specs=[pl.BlockSpec((tm, width), lambda n, m: (m, n))] + cast_specs,
        out_shape=[jax.ShapeDtypeStruct((seq, d_in), BF16)] + cast_shapes,
        scratch_shapes=[pltpu.VMEM((d, width), BF16), pltpu.VMEM((tm, width), F32)],
        compiler_params=_params(2),
        name="proj",
    )(xn, w_in, cos, sin, ln_g.reshape(1, width), ln_b.reshape(1, width), *cast_weights)


def _attn_kernel(q_ref, kc_ref, kp_ref, vc_ref, vp_ref, o_ref,
                 stage, q4, k4, v4, kb, vb, m4, l4, acc4, m_s, l_s, acc_s, bias, *, dil):
    t = pl.program_id(0)
    sup = q_ref.shape[0]
    sub = sup // dil
    n_units = sup // BLOCK

    def split(dst, n_rows):
        part = n_rows // dil
        for b in range(dil):
            dst[b * part:(b + 1) * part, :] = stage[pl.ds(b, part, stride=dil), :]

    stage[:sup] = kp_ref[...].astype(F32)
    stage[sup:] = kc_ref[...].astype(F32)
    split(k4, 2 * sup)
    stage[:sup] = vp_ref[...].astype(F32)
    stage[sup:] = vc_ref[...].astype(F32)
    split(v4, 2 * sup)
    stage[:sup] = q_ref[...].astype(F32)
    split(q4, sup)
    kb[:BLOCK] = kp_ref[sup - BLOCK:, :]
    kb[BLOCK:] = kc_ref[...]
    vb[:BLOCK] = vp_ref[sup - BLOCK:, :]
    vb[BLOCK:] = vc_ref[...]

    qi = jax.lax.broadcasted_iota(jnp.int32, (BLOCK, 2 * BLOCK), 0)
    kj = jax.lax.broadcasted_iota(jnp.int32, (BLOCK, 2 * BLOCK), 1)
    diff = qi + BLOCK - kj
    band = (diff >= 0) & (diff <= BLOCK)
    bias[0] = jnp.where(band & (kj >= BLOCK), 0.0, -jnp.inf)
    bias[1] = jnp.where(band, 0.0, -jnp.inf)
    low = jax.lax.broadcasted_iota(jnp.int32, (BLOCK, LANES), 1) < HEAD_DIM

    def head_pair(q, k, v, has_prev):
        mask = bias[has_prev.astype(jnp.int32)]
        ms, ls, os = [], [], []
        for hh in range(2):
            qm = jnp.where(low if hh == 0 else ~low, q, jnp.zeros_like(q))
            s = jax.lax.dot_general(qm, k, (((1,), (1,)), ((), ())),
                                    preferred_element_type=F32)
            s = s + mask
            m = jnp.max(s, axis=-1, keepdims=True)
            e = jnp.exp2(s - m)
            ms.append(m)
            ls.append(jnp.sum(e, axis=-1, keepdims=True))
            os.append(jnp.dot(e.astype(BF16), v, preferred_element_type=F32))
        return (jnp.where(low, ms[0], ms[1]), jnp.where(low, ls[0], ls[1]),
                jnp.where(low, os[0], os[1]))

    def merge(refs, rows, stats):
        m_ref, l_ref, acc_ref = refs
        m_u, l_u, o_u = stats
        m_old = m_ref[rows, :]
        m_new = jnp.maximum(m_old, m_u)
        a_old = jnp.exp2(m_old - m_new)
        a_u = jnp.exp2(m_u - m_new)
        m_ref[rows, :] = m_new
        l_ref[rows, :] = a_old * l_ref[rows, :] + a_u * l_u
        acc_ref[rows, :] = a_old * acc_ref[rows, :] + a_u * o_u

    def mid_unit(u, carry):
        r = u // (sub // BLOCK)
        n = u % (sub // BLOCK)
        rows = pl.ds(pl.multiple_of(r * sub + n * BLOCK, BLOCK), BLOCK)
        kv_rows = pl.ds(pl.multiple_of(r * 2 * sub + sub + (n - 1) * BLOCK, BLOCK), 2 * BLOCK)
        m_u, l_u, o_u = head_pair(q4[rows, :].astype(BF16), k4[kv_rows, :].astype(BF16),
                                  v4[kv_rows, :].astype(BF16), (t > 0) | (n > 0))
        m4[rows, :] = m_u
        l4[rows, :] = l_u
        acc4[rows, :] = o_u
        return carry

    def wide_unit(r, carry):
        a = r // dil
        b = r % dil
        rows = pl.ds(b * sub + a, BLOCK, stride=dil)
        kv_rows = pl.ds(b * 2 * sub + a, 2 * BLOCK, stride=dil)
        stats = head_pair(q4[rows, :].astype(BF16), k4[kv_rows, :].astype(BF16),
                          v4[kv_rows, :].astype(BF16), t > 0)
        merge((m4, l4, acc4), rows, stats)
        return carry

    def near_unit(n, carry):
        rows = pl.ds(pl.multiple_of(n * BLOCK, BLOCK), BLOCK)
        kv_rows = pl.ds(pl.multiple_of(n * BLOCK, BLOCK), 2 * BLOCK)
        stats = head_pair(q_ref[rows, :], kb[kv_rows, :], vb[kv_rows, :], (t > 0) | (n > 0))
        merge((m_s, l_s, acc_s), rows, stats)
        return carry

    jax.lax.fori_loop(0, n_units, mid_unit, 0, unroll=True)
    jax.lax.fori_loop(0, n_units, wide_unit, 0, unroll=True)
    for b in range(dil):
        src = slice(b * sub, (b + 1) * sub)
        dst = pl.ds(b, sub, stride=dil)
        m_s[dst, :] = m4[src, :]
        l_s[dst, :] = l4[src, :]
        acc_s[dst, :] = acc4[src, :]
    jax.lax.fori_loop(0, n_units, near_unit, 0, unroll=True)

    o_ref[...] = (acc_s[...] / l_s[...]).astype(o_ref.dtype)


def _attention(proj, *, width):
    seq, d_in = proj.shape
    dilations = tuple(d for _, d in DILATED_BRANCHES)
    steps = {w // d for w, d in DILATED_BRANCHES}
    assert steps == {BLOCK}, "every branch must span exactly one previous block"
    dil = dilations[1]
    assert dilations == (1, dil, dil * dil)
    sup = BLOCK * dil * dil
    seg = width // LANES
    cur = lambda i: pl.BlockSpec((sup, LANES), lambda t, p: (t, i * seg + p))
    prev = lambda i: pl.BlockSpec((sup, LANES),
                                  lambda t, p: (jnp.maximum(t - 1, 0), i * seg + p))
    f32_rows = lambda n: pltpu.VMEM((n, LANES), F32)
    return pl.pallas_call(
        functools.partial(_attn_kernel, dil=dil),
        grid=(seq // sup, seg),
        in_specs=[cur(0), cur(1), prev(1), cur(2), prev(2)],
        out_specs=pl.BlockSpec((sup, LANES), lambda t, p: (t, p)),
        out_shape=jax.ShapeDtypeStruct((seq, width), BF16),
        scratch_shapes=[f32_rows(2 * sup),
                        f32_rows(sup), f32_rows(2 * sup), f32_rows(2 * sup),
                        pltpu.VMEM((sup + BLOCK, LANES), BF16),
                        pltpu.VMEM((sup + BLOCK, LANES), BF16),
                        f32_rows(sup), f32_rows(sup), f32_rows(sup),
                        f32_rows(sup), f32_rows(sup), f32_rows(sup),
                        pltpu.VMEM((2, BLOCK, 2 * BLOCK), F32)],
        compiler_params=_params(2),
        name="dilated_attn",
    )(proj, proj, proj, proj, proj)


def _mix_kernel(h_ref, attn_ref, u_ref, gv_ref, ws_ref, bs_ref, ag_ref, sg_ref, wo_ref, ng_ref,
                out_ref, xn_ref, mixed_ref):
    tm = h_ref.shape[0]
    width = u_ref.shape[1]
    mixed_ref[:, :width] = _rms(attn_ref[...].astype(F32), ag_ref[...]).astype(BF16)

    ci = jax.lax.broadcasted_iota(jnp.int32, (SGU_CHUNK, SGU_CHUNK), 0)
    cj = jax.lax.broadcasted_iota(jnp.int32, (SGU_CHUNK, SGU_CHUNK), 1)
    causal = cj <= ci
    for g in range(width // SGU_GROUP_DIM):
        gs = slice(g * SGU_GROUP_DIM, (g + 1) * SGU_GROUP_DIM)
        w = jnp.where(causal, ws_ref[g], 0.0).astype(BF16)
        b = bs_ref[:, g:g + 1]
        for c in range(tm // SGU_CHUNK):
            rs = slice(c * SGU_CHUNK, (c + 1) * SGU_CHUNK)
            sv = jnp.dot(w, gv_ref[rs, gs], preferred_element_type=F32) + b
            out_ref[rs, gs] = u_ref[rs, gs].astype(F32) * sv
    sgu = out_ref[:, :width]
    mixed_ref[:, width:] = _rms(sgu, sg_ref[...]).astype(BF16)

    h = h_ref[...] + jnp.dot(mixed_ref[...], wo_ref[...], preferred_element_type=F32)
    out_ref[...] = h
    xn_ref[...] = _rms(h, ng_ref[...]).astype(BF16)


def _mix(h, attn, proj, w_s, b_s, attn_g, sgu_g, w_out, next_g, *, width, tm=512):
    seq, d = h.shape
    n_grp = w_s.shape[0]
    row = lambda m: (m, 0)
    const2 = lambda m: (0, 0)
    return pl.pallas_call(
        _mix_kernel,
        grid=(seq // tm,),
        in_specs=[pl.BlockSpec((tm, d), row),
                  pl.BlockSpec((tm, width), row),
                  pl.BlockSpec((tm, width), lambda m: (m, 3)),
                  pl.BlockSpec((tm, width), lambda m: (m, 4)),
                  pl.BlockSpec((n_grp, SGU_CHUNK, SGU_CHUNK), lambda m: (0, 0, 0)),
                  pl.BlockSpec((SGU_CHUNK, n_grp), const2),
                  pl.BlockSpec((1, width), const2),
                  pl.BlockSpec((1, width), const2),
                  _resident((2 * width, d)),
                  pl.BlockSpec((1, d), const2)],
        out_specs=[pl.BlockSpec((tm, d), row), pl.BlockSpec((tm, d), row)],
        out_shape=[jax.ShapeDtypeStruct((seq, d), F32), jax.ShapeDtypeStruct((seq, d), BF16)],
        scratch_shapes=[pltpu.VMEM((tm, 2 * width), BF16)],
        compiler_params=_params(1),
        name="mix_out",
    )(h, attn, proj, proj, w_s, b_s.T, attn_g.reshape(1, width), sgu_g.reshape(1, width),
      w_out, next_g.reshape(1, d))


def _mem_fold_kernel(mem_ref, g_ref, wk_ref, wv_ref, wq_ref, wo_ref, kq_ref, vo_ref):
    hd = wk_ref.shape[1]
    mk = _rms(mem_ref[...], g_ref[...]).astype(BF16)
    k = jnp.dot(mk, wk_ref[...].astype(BF16), preferred_element_type=F32).astype(BF16)
    v = jnp.dot(mk, wv_ref[...].astype(BF16), preferred_element_type=F32).astype(BF16)
    kq = jax.lax.dot_general(wq_ref[...].astype(BF16), k, (((1,), (1,)), ((), ())),
                             preferred_element_type=F32)
    kq_ref[...] = (kq * (hd ** -0.5)).astype(BF16)
    vo_ref[...] = jnp.dot(v, wo_ref[...].astype(BF16), preferred_element_type=F32).astype(BF16)


def _mem_fold(mem, g, w_k, w_v, w_q, w_o):
    n_mem, d = mem.shape
    hd = d // N_MEM_HEADS
    cols = pl.BlockSpec((d, hd), lambda h: (0, h))
    return pl.pallas_call(
        _mem_fold_kernel,
        grid=(N_MEM_HEADS,),
        in_specs=[pl.BlockSpec((n_mem, d), lambda h: (0, 0)),
                  pl.BlockSpec((1, d), lambda h: (0, 0)),
                  cols, cols, cols,
                  pl.BlockSpec((hd, d), lambda h: (h, 0))],
        out_specs=[pl.BlockSpec((d, n_mem), lambda h: (0, h)),
                   pl.BlockSpec((n_mem, d), lambda h: (h, 0))],
        out_shape=[jax.ShapeDtypeStruct((d, N_MEM_HEADS * n_mem), BF16),
                   jax.ShapeDtypeStruct((N_MEM_HEADS * n_mem, d), BF16)],
        compiler_params=_params(1),
        name="mem_fold",
    )(mem, g.reshape(1, d), w_k, w_v, w_q, w_o)


def _mem_cross_kernel(h_ref, xq_ref, kq_ref, vo_ref, ng_ref, out_ref, xn_ref, p_ref):
    n_mem = kq_ref.shape[1] // N_MEM_HEADS
    for i in range(N_MEM_HEADS):
        sl = slice(i * n_mem, (i + 1) * n_mem)
        s = jnp.dot(xq_ref[...], kq_ref[:, sl], preferred_element_type=F32)
        m = jnp.max(s, axis=-1, keepdims=True)
        e = jnp.exp(s - m)
        p_ref[:, sl] = (e / jnp.sum(e, axis=-1, keepdims=True)).astype(BF16)
    h = h_ref[...] + jnp.dot(p_ref[...], vo_ref[...], preferred_element_type=F32)
    out_ref[...] = h
    xn_ref[...] = _rms(h, ng_ref[...]).astype(BF16)


def _mem_cross(h, xq, kq, vo, next_g, *, tm=512):
    seq, d = h.shape
    row = pl.BlockSpec((tm, d), lambda m: (m, 0))
    return pl.pallas_call(
        _mem_cross_kernel,
        grid=(seq // tm,),
        in_specs=[row, row, _resident(kq.shape), _resident(vo.shape),
                  pl.BlockSpec((1, d), lambda m: (0, 0))],
        out_specs=[row, row],
        out_shape=[jax.ShapeDtypeStruct((seq, d), F32), jax.ShapeDtypeStruct((seq, d), BF16)],
        scratch_shapes=[pltpu.VMEM((tm, kq.shape[1]), BF16)],
        compiler_params=_params(1),
        name="mem_cross",
    )(h, xq, kq, vo, next_g.reshape(1, d))


def kernel(x, mem, positions, ffn1_norm, ffn1_w_gate, ffn1_w_up, ffn1_w_down, mix_norm, w_in,
           sgu_ln_gain, sgu_ln_bias, sgu_w_s, sgu_b_s, attn_out_gain, sgu_out_gain, w_out,
           mem_q_norm, mem_kv_norm, mem_w_q, mem_w_k, mem_w_v, mem_w_o, ffn2_norm,
           ffn2_w_gate, ffn2_w_up, ffn2_w_down, final_norm):
    batch, seq, d = x.shape
    depth = ffn1_norm.shape[0]
    width = attn_out_gain.shape[1]
    assert batch == 1 and mem.shape[0] == 1 and depth == 1
    layer = 0

    h = x.reshape(seq, d)
    mem2 = mem.reshape(mem.shape[1], d)

    cos, sin, xn = _prep(positions, h, ffn1_norm[layer])
    a, wd_bf = _ffn_up(xn, ffn1_w_gate[layer], ffn1_w_up[layer], ffn1_w_down[layer])
    h, xn, w_out_bf = _ffn_down(a, h, wd_bf, mix_norm[layer], (w_out[layer],), final=False)
    proj, wg2_bf, wu2_bf, wd2_bf = _proj(
        xn, w_in[layer], cos, sin, sgu_ln_gain[layer], sgu_ln_bias[layer],
        (ffn2_w_gate[layer], ffn2_w_up[layer], ffn2_w_down[layer]), width=width)
    attn = _attention(proj, width=width)
    h, xn = _mix(h, attn, proj, sgu_w_s[layer], sgu_b_s[layer], attn_out_gain[layer],
                 sgu_out_gain[layer], w_out_bf, mem_q_norm[layer], width=width)
    kq, vo = _mem_fold(mem2, mem_kv_norm[layer], mem_w_k[layer], mem_w_v[layer],
                       mem_w_q[layer], mem_w_o[layer])
    h, xn = _mem_cross(h, xn, kq, vo, ffn2_norm[layer])
    a, = _ffn_up(xn, wg2_bf, wu2_bf)
    out = _ffn_down(a, h, wd2_bf, final_norm, final=True)[0]
    return out.reshape(batch, seq, d)
```

```python
import functools

import jax
import jax.numpy as jnp
import numpy as np
from jax.experimental import pallas as pl
from jax.experimental.pallas import tpu as pltpu

F32 = jnp.float32
BF16 = jnp.bfloat16

EPS = 1e-6
ROPE_THETA = 10000.0
FFN_RES_SCALE = 0.5
HEAD_DIM = 64
BLOCK = 128
DILATED_BRANCHES = ((128, 1), (512, 4), (2048, 16))
SGU_CHUNK = 128
SGU_GROUP_DIM = 128
N_MEM_HEADS = 4
LANES = 128
MXU_COLS = 256
CAST_ROWS = 256
FFN_UP_ROWS = 1024

VMEM_LIMIT = 56 * 1024 * 1024


def _params(n_axes, vmem=VMEM_LIMIT):
    return pltpu.CompilerParams(
        dimension_semantics=("arbitrary",) * n_axes, vmem_limit_bytes=vmem)


def _resident(shape):
    return pl.BlockSpec(shape, lambda *_: (0,) * len(shape), pipeline_mode=pl.Buffered(1))


def _rms(x, g):
    return x * jax.lax.rsqrt(jnp.mean(x * x, axis=-1, keepdims=True) + EPS) * g


def _gelu(x):
    c = np.sqrt(2.0 / np.pi).astype(np.float32)
    return 0.5 * x * (1.0 + jnp.tanh(c * (x + 0.044715 * (x * x * x))))


def _cast_to_bf16(src_ref, dst_ref):
    rows = src_ref.shape[0]
    chunk = min(CAST_ROWS, rows)

    def body(i, carry):
        sl = pl.ds(pl.multiple_of(i * chunk, chunk), chunk)
        dst_ref[sl, :] = src_ref[sl, :].astype(BF16)
        return carry

    jax.lax.fori_loop(0, rows // chunk, body, 0)


def _prep_kernel(pos_ref, freq_ref, x_ref, g_ref, cos_ref, sin_ref, xn_ref):
    ang = pos_ref[...].astype(F32) * freq_ref[...]
    lane = jax.lax.broadcasted_iota(jnp.int32, ang.shape, 1)
    sign = jnp.where(lane % HEAD_DIM < HEAD_DIM // 2, -1.0, 1.0)
    cos_ref[...] = jnp.cos(ang)
    sin_ref[...] = jnp.sin(ang) * sign
    xn_ref[...] = _rms(x_ref[...], g_ref[...]).astype(BF16)


def _prep(positions, x, g, *, tm=1024):
    seq, d = x.shape
    half = HEAD_DIM // 2
    inv_freq = ROPE_THETA ** (-jnp.arange(0, HEAD_DIM, 2, dtype=F32) / HEAD_DIM)
    freq_lane = jnp.tile(inv_freq, LANES // half).reshape(1, LANES)
    pos = positions.reshape(seq, 1)
    row = lambda w: pl.BlockSpec((tm, w), lambda i: (i, 0))
    const = lambda w: pl.BlockSpec((1, w), lambda i: (0, 0))
    return pl.pallas_call(
        _prep_kernel,
        grid=(seq // tm,),
        in_specs=[row(1), const(LANES), row(d), const(d)],
        out_specs=[row(LANES), row(LANES), row(d)],
        out_shape=[jax.ShapeDtypeStruct((seq, LANES), F32)] * 2
        + [jax.ShapeDtypeStruct((seq, d), BF16)],
        compiler_params=_params(1),
        name="prep",
    )(pos, freq_lane, x, g.reshape(1, d))


def _ffn_up_kernel(*refs, precast):
    if precast:
        xn_ref, wgb, wub, a_ref = refs
    else:
        xn_ref, wg_ref, wu_ref, wd_ref, a_ref, wdb_ref, wgb, wub = refs
        wdb_ref[...] = wd_ref[...].astype(BF16)

        @pl.when(pl.program_id(1) == 0)
        def _():
            _cast_to_bf16(wg_ref, wgb)
            _cast_to_bf16(wu_ref, wub)

    for r0 in range(0, xn_ref.shape[0], FFN_UP_ROWS):
        rows = slice(r0, r0 + FFN_UP_ROWS)
        xn = xn_ref[rows, :]
        gate = jnp.dot(xn, wgb[...], preferred_element_type=F32)
        up = jnp.dot(xn, wub[...], preferred_element_type=F32)
        a_ref[rows, :] = ((gate * (FFN_RES_SCALE / (1.0 + jnp.exp(-gate)))) * up).astype(BF16)


def _ffn_up(xn, wg, wu, wd=None, *, tf=512):
    seq, d = xn.shape
    dff = wg.shape[1]
    precast = wd is None
    tm = 2 * FFN_UP_ROWS
    m_steps = seq // tm
    in_specs = [pl.BlockSpec((tm, d), lambda f, m: (m, 0)),
                pl.BlockSpec((d, tf), lambda f, m: (0, f)),
                pl.BlockSpec((d, tf), lambda f, m: (0, f))]
    out_specs = [pl.BlockSpec((tm, tf), lambda f, m: (m, f))]
    out_shape = [jax.ShapeDtypeStruct((seq, dff), BF16)]
    scratch, args = [], [xn, wg, wu]
    if not precast:
        assert dff % ((dff // tf) * m_steps) == 0
        wd_rows = dff // ((dff // tf) * m_steps)
        wd_spec = pl.BlockSpec((wd_rows, d), lambda f, m: (f * m_steps + m, 0))
        in_specs.append(wd_spec)
        out_specs.append(wd_spec)
        out_shape.append(jax.ShapeDtypeStruct((dff, d), BF16))
        scratch = [pltpu.VMEM((d, tf), BF16), pltpu.VMEM((d, tf), BF16)]
        args.append(wd)
    return pl.pallas_call(
        functools.partial(_ffn_up_kernel, precast=precast),
        grid=(dff // tf, seq // tm),
        in_specs=in_specs,
        out_specs=out_specs,
        out_shape=out_shape,
        scratch_shapes=scratch,
        compiler_params=_params(2),
        name="ffn_up_bf16" if precast else "ffn_up",
    )(*args)


def _ffn_down_kernel(*refs, final, n_cast):
    a_ref, x_ref, wd_ref, g_ref = refs[:4]
    cast_in = refs[4:4 + n_cast]
    out_refs = refs[4 + n_cast:]
    h = x_ref[...] + jnp.dot(a_ref[...], wd_ref[...], preferred_element_type=F32)
    if final:
        out_refs[0][...] = _rms(h, g_ref[...])
    else:
        out_refs[0][...] = h
        out_refs[1][...] = _rms(h, g_ref[...]).astype(BF16)
        for src, dst in zip(cast_in, out_refs[2:]):
            dst[...] = src[...].astype(BF16)


def _ffn_down(a, x, wd_bf, g, cast_weights=(), *, final, tm=256):
    seq, d = x.shape
    dff = a.shape[1]
    steps = seq // tm
    row = pl.BlockSpec((tm, d), lambda m: (m, 0))
    out_specs = [row] if final else [row, row]
    out_shape = [jax.ShapeDtypeStruct((seq, d), F32)]
    if not final:
        out_shape.append(jax.ShapeDtypeStruct((seq, d), BF16))
    cast_specs = []
    for w in cast_weights:
        assert not final and w.shape[0] % steps == 0
        spec = pl.BlockSpec((w.shape[0] // steps, w.shape[1]), lambda m: (m, 0))
        cast_specs.append(spec)
        out_specs.append(spec)
        out_shape.append(jax.ShapeDtypeStruct(w.shape, BF16))
    return pl.pallas_call(
        functools.partial(_ffn_down_kernel, final=final, n_cast=len(cast_weights)),
        grid=(steps,),
        in_specs=[pl.BlockSpec((tm, dff), lambda m: (m, 0)), row,
                  _resident((dff, d)), pl.BlockSpec((1, d), lambda m: (0, 0))] + cast_specs,
        out_specs=out_specs,
        out_shape=out_shape,
        compiler_params=_params(1),
        name="ffn_down_final" if final else "ffn_down",
    )(a, x, wd_bf, g.reshape(1, d), *cast_weights)


def _proj_kernel(*refs, q_scale, n_cast, cast_steps):
    xn_ref, w_ref, cos_ref, sin_ref, lng_ref, lnb_ref = refs[:6]
    cast_in = refs[6:6 + n_cast]
    o_ref = refs[6 + n_cast]
    cast_out = refs[7 + n_cast:7 + 2 * n_cast]
    wb, t_ref = refs[7 + 2 * n_cast:]
    n = pl.program_id(0)
    width = o_ref.shape[1]

    @pl.when(n * pl.num_programs(1) + pl.program_id(1) < cast_steps)
    def _():
        for src, dst in zip(cast_in, cast_out):
            dst[...] = src[...].astype(BF16)

    @pl.when(pl.program_id(1) == 0)
    def _():
        _cast_to_bf16(w_ref, wb)

    half = xn_ref.shape[0] // 2

    def chunked(epilogue):
        for j in range(width // MXU_COLS):
            c0 = j * MXU_COLS
            for r0 in (0, half):
                rows = slice(r0, r0 + half)
                y = jnp.dot(xn_ref[rows, :], wb[:, c0:c0 + MXU_COLS],
                            preferred_element_type=F32)
                epilogue(y, rows, c0)

    def rope(scale):
        def epilogue(y, rows, c0):
            cos = cos_ref[rows, :]
            sin = sin_ref[rows, :]
            lane = jax.lax.broadcasted_iota(jnp.int32, cos.shape, 1)
            first = lane % HEAD_DIM < HEAD_DIM // 2
            for j in range(MXU_COLS // LANES):
                blk = y[:, j * LANES:(j + 1) * LANES]
                partner = jnp.where(first,
                                    pltpu.roll(blk, LANES - HEAD_DIM // 2, 1),
                                    pltpu.roll(blk, HEAD_DIM // 2, 1))
                out = blk * cos + partner * sin
                if scale != 1.0:
                    out = out * scale
                o_ref[rows, c0 + j * LANES:c0 + (j + 1) * LANES] = out.astype(o_ref.dtype)
        return epilogue

    def store(fn):
        def epilogue(y, rows, c0):
            o_ref[rows, c0:c0 + MXU_COLS] = fn(y).astype(o_ref.dtype)
        return epilogue

    @pl.when(n == 0)
    def _():
        chunked(rope(q_scale))

    @pl.when(n == 1)
    def _():
        chunked(rope(1.0))

    @pl.when(n == 2)
    def _():
        chunked(store(lambda y: y))

    @pl.when(n == 3)
    def _():
        chunked(store(_gelu))

    @pl.when(n == 4)
    def _():
        sums = {0: [], half: []}

        def epilogue(y, rows, c0):
            t = _gelu(y)
            t_ref[rows, c0:c0 + MXU_COLS] = t
            sums[rows.start].append(jnp.sum(t, axis=-1, keepdims=True))

        chunked(epilogue)
        for r0 in (0, half):
            rows = slice(r0, r0 + half)
            mu = sum(sums[r0]) * (1.0 / width)
            tc = t_ref[rows, :] - mu
            t = tc * jax.lax.rsqrt(jnp.mean(tc * tc, axis=-1, keepdims=True) + EPS)
            o_ref[rows, :] = (t * lng_ref[...] + lnb_ref[...]).astype(o_ref.dtype)


def _proj(xn, w_in, cos, sin, ln_g, ln_b, cast_weights=(), *, width, tm=1024, cast_steps=32):
    seq, d = xn.shape
    d_in = w_in.shape[1]
    assert d_in == 5 * width
    m_steps = seq // tm
    assert cast_steps <= (d_in // width) * m_steps
    cast_specs, cast_shapes = [], []
    for w in cast_weights:
        assert w.shape[0] % cast_steps == 0
        cast_specs.append(pl.BlockSpec(
            (w.shape[0] // cast_steps, w.shape[1]),
            lambda n, m: (jnp.minimum(n * m_steps + m, cast_steps - 1), 0)))
        cast_shapes.append(jax.ShapeDtypeStruct(w.shape, BF16))
    return pl.pallas_call(
        functools.partial(_proj_kernel, q_scale=HEAD_DIM ** -0.5 * np.log2(np.e),
                          n_cast=len(cast_weights), cast_steps=cast_steps),
        grid=(d_in // width, m_steps),
        in_specs=[
            pl.BlockSpec((tm, d), lambda n, m: (m, 0)),
            pl.BlockSpec((d, width), lambda n, m: (0, n)),
            pl.BlockSpec((tm, LANES), lambda n, m: (m, 0)),
            pl.BlockSpec((tm, LANES), lambda n, m: (m, 0)),
            pl.BlockSpec((1, width), lambda n, m: (0, 0)),
            pl.BlockSpec((1, width), lambda n, m: (0, 0)),
        ] + cast_specs,
        out_specs=[pl.BlockSpec((tm, width), lambda n, m: (m, n))] + cast_specs,
        out_shape=[jax.ShapeDtypeStruct((seq, d_in), BF16)] + cast_shapes,
        scratch_shapes=[pltpu.VMEM((d, width), BF16), pltpu.VMEM((tm, width), F32)],
        compiler_params=_params(2),
        name="proj",
    )(xn, w_in, cos, sin, ln_g.reshape(1, width), ln_b.reshape(1, width), *cast_weights)


def _attn_kernel(q_ref, kc_ref, kp_ref, vc_ref, vp_ref, o_ref,
                 stage, q4, k4, v4, kb, vb, m4, l4, acc4, m_s, l_s, acc_s, bias, *, dil):
    t = pl.program_id(0)
    sup = q_ref.shape[0]
    sub = sup // dil
    n_units = sup // BLOCK

    def split(dst, n_rows):
        part = n_rows // dil
        for b in range(dil):
            dst[b * part:(b + 1) * part, :] = stage[pl.ds(b, part, stride=dil), :]

    stage[:sup] = kp_ref[...].astype(F32)
    stage[sup:] = kc_ref[...].astype(F32)
    split(k4, 2 * sup)
    stage[:sup] = vp_ref[...].astype(F32)
    stage[sup:] = vc_ref[...].astype(F32)
    split(v4, 2 * sup)
    stage[:sup] = q_ref[...].astype(F32)
    split(q4, sup)
    kb[:BLOCK] = kp_ref[sup - BLOCK:, :]
    kb[BLOCK:] = kc_ref[...]
    vb[:BLOCK] = vp_ref[sup - BLOCK:, :]
    vb[BLOCK:] = vc_ref[...]

    qi = jax.lax.broadcasted_iota(jnp.int32, (BLOCK, 2 * BLOCK), 0)
    kj = jax.lax.broadcasted_iota(jnp.int32, (BLOCK, 2 * BLOCK), 1)
    diff = qi + BLOCK - kj
    band = (diff >= 0) & (diff <= BLOCK)
    bias[0] = jnp.where(band & (kj >= BLOCK), 0.0, -jnp.inf)
    bias[1] = jnp.where(band, 0.0, -jnp.inf)
    low = jax.lax.broadcasted_iota(jnp.int32, (BLOCK, LANES), 1) < HEAD_DIM

    def head_pair(q, k, v, has_prev):
        mask = bias[has_prev.astype(jnp.int32)]
        ms, ls, os = [], [], []
        for hh in range(2):
            qm = jnp.where(low if hh == 0 else ~low, q, jnp.zeros_like(q))
            s = jax.lax.dot_general(qm, k, (((1,), (1,)), ((), ())),
                                    preferred_element_type=F32)
            s = s + mask
            m = jnp.max(s, axis=-1, keepdims=True)
            e = jnp.exp2(s - m)
            ms.append(m)
            ls.append(jnp.sum(e, axis=-1, keepdims=True))
            os.append(jnp.dot(e.astype(BF16), v, preferred_element_type=F32))
        return (jnp.where(low, ms[0], ms[1]), jnp.where(low, ls[0], ls[1]),
                jnp.where(low, os[0], os[1]))

    def merge(refs, rows, stats):
        m_ref, l_ref, acc_ref = refs
        m_u, l_u, o_u = stats
        m_old = m_ref[rows, :]
        m_new = jnp.maximum(m_old, m_u)
        a_old = jnp.exp2(m_old - m_new)
        a_u = jnp.exp2(m_u - m_new)
        m_ref[rows, :] = m_new
        l_ref[rows, :] = a_old * l_ref[rows, :] + a_u * l_u
        acc_ref[rows, :] = a_old * acc_ref[rows, :] + a_u * o_u

    def mid_unit(u, carry):
        r = u // (sub // BLOCK)
        n = u % (sub // BLOCK)
        rows = pl.ds(pl.multiple_of(r * sub + n * BLOCK, BLOCK), BLOCK)
        kv_rows = pl.ds(pl.multiple_of(r * 2 * sub + sub + (n - 1) * BLOCK, BLOCK), 2 * BLOCK)
        m_u, l_u, o_u = head_pair(q4[rows, :].astype(BF16), k4[kv_rows, :].astype(BF16),
                                  v4[kv_rows, :].astype(BF16), (t > 0) | (n > 0))
        m4[rows, :] = m_u
        l4[rows, :] = l_u
        acc4[rows, :] = o_u
        return carry

    def wide_unit(r, carry):
        a = r // dil
        b = r % dil
        rows = pl.ds(b * sub + a, BLOCK, stride=dil)
        kv_rows = pl.ds(b * 2 * sub + a, 2 * BLOCK, stride=dil)
        stats = head_pair(q4[rows, :].astype(BF16), k4[kv_rows, :].astype(BF16),
                          v4[kv_rows, :].astype(BF16), t > 0)
        merge((m4, l4, acc4), rows, stats)
        return carry

    def near_unit(n, carry):
        rows = pl.ds(pl.multiple_of(n * BLOCK, BLOCK), BLOCK)
        kv_rows = pl.ds(pl.multiple_of(n * BLOCK, BLOCK), 2 * BLOCK)
        stats = head_pair(q_ref[rows, :], kb[kv_rows, :], vb[kv_rows, :], (t > 0) | (n > 0))
        merge((m_s, l_s, acc_s), rows, stats)
        return carry

    jax.lax.fori_loop(0, n_units, mid_unit, 0, unroll=True)
    jax.lax.fori_loop(0, n_units, wide_unit, 0, unroll=True)
    for b in range(dil):
        src = slice(b * sub, (b + 1) * sub)
        dst = pl.ds(b, sub, stride=dil)
        m_s[dst, :] = m4[src, :]
        l_s[dst, :] = l4[src, :]
        acc_s[dst, :] = acc4[src, :]
    jax.lax.fori_loop(0, n_units, near_unit, 0, unroll=True)

    o_ref[...] = (acc_s[...] / l_s[...]).astype(o_ref.dtype)


def _attention(proj, *, width):
    seq, d_in = proj.shape
    dilations = tuple(d for _, d in DILATED_BRANCHES)
    steps = {w // d for w, d in DILATED_BRANCHES}
    assert steps == {BLOCK}, "every branch must span exactly one previous block"
    dil = dilations[1]
    assert dilations == (1, dil, dil * dil)
    sup = BLOCK * dil * dil
    seg = width // LANES
    cur = lambda i: pl.BlockSpec((sup, LANES), lambda t, p: (t, i * seg + p))
    prev = lambda i: pl.BlockSpec((sup, LANES),
                                  lambda t, p: (jnp.maximum(t - 1, 0), i * seg + p))
    f32_rows = lambda n: pltpu.VMEM((n, LANES), F32)
    return pl.pallas_call(
        functools.partial(_attn_kernel, dil=dil),
        grid=(seq // sup, seg),
        in_specs=[cur(0), cur(1), prev(1), cur(2), prev(2)],
        out_specs=pl.BlockSpec((sup, LANES), lambda t, p: (t, p)),
        out_shape=jax.ShapeDtypeStruct((seq, width), BF16),
        scratch_shapes=[f32_rows(2 * sup),
                        f32_rows(sup), f32_rows(2 * sup), f32_rows(2 * sup),
                        pltpu.VMEM((sup + BLOCK, LANES), BF16),
                        pltpu.VMEM((sup + BLOCK, LANES), BF16),
                        f32_rows(sup), f32_rows(sup), f32_rows(sup),
                        f32_rows(sup), f32_rows(sup), f32_rows(sup),
                        pltpu.VMEM((2, BLOCK, 2 * BLOCK), F32)],
        compiler_params=_params(2),
        name="dilated_attn",
    )(proj, proj, proj, proj, proj)


def _mix_kernel(h_ref, attn_ref, u_ref, gv_ref, ws_ref, bs_ref, ag_ref, sg_ref, wo_ref, ng_ref,
                out_ref, xn_ref, mixed_ref):
    tm = h_ref.shape[0]
    width = u_ref.shape[1]
    mixed_ref[:, :width] = _rms(attn_ref[...].astype(F32), ag_ref[...]).astype(BF16)

    ci = jax.lax.broadcasted_iota(jnp.int32, (SGU_CHUNK, SGU_CHUNK), 0)
    cj = jax.lax.broadcasted_iota(jnp.int32, (SGU_CHUNK, SGU_CHUNK), 1)
    causal = cj <= ci
    for g in range(width // SGU_GROUP_DIM):
        gs = slice(g * SGU_GROUP_DIM, (g + 1) * SGU_GROUP_DIM)
        w = jnp.where(causal, ws_ref[g], 0.0).astype(BF16)
        b = bs_ref[:, g:g + 1]
        for c in range(tm // SGU_CHUNK):
            rs = slice(c * SGU_CHUNK, (c + 1) * SGU_CHUNK)
            sv = jnp.dot(w, gv_ref[rs, gs], preferred_element_type=F32) + b
            out_ref[rs, gs] = u_ref[rs, gs].astype(F32) * sv
    sgu = out_ref[:, :width]
    mixed_ref[:, width:] = _rms(sgu, sg_ref[...]).astype(BF16)

    h = h_ref[...] + jnp.dot(mixed_ref[...], wo_ref[...], preferred_element_type=F32)
    out_ref[...] = h
    xn_ref[...] = _rms(h, ng_ref[...]).astype(BF16)


def _mix(h, attn, proj, w_s, b_s, attn_g, sgu_g, w_out, next_g, *, width, tm=512):
    seq, d = h.shape
    n_grp = w_s.shape[0]
    row = lambda m: (m, 0)
    const2 = lambda m: (0, 0)
    return pl.pallas_call(
        _mix_kernel,
        grid=(seq // tm,),
        in_specs=[pl.BlockSpec((tm, d), row),
                  pl.BlockSpec((tm, width), row),
                  pl.BlockSpec((tm, width), lambda m: (m, 3)),
                  pl.BlockSpec((tm, width), lambda m: (m, 4)),
                  pl.BlockSpec((n_grp, SGU_CHUNK, SGU_CHUNK), lambda m: (0, 0, 0)),
                  pl.BlockSpec((SGU_CHUNK, n_grp), const2),
                  pl.BlockSpec((1, width), const2),
                  pl.BlockSpec((1, width), const2),
                  _resident((2 * width, d)),
                  pl.BlockSpec((1, d), const2)],
        out_specs=[pl.BlockSpec((tm, d), row), pl.BlockSpec((tm, d), row)],
        out_shape=[jax.ShapeDtypeStruct((seq, d), F32), jax.ShapeDtypeStruct((seq, d), BF16)],
        scratch_shapes=[pltpu.VMEM((tm, 2 * width), BF16)],
        compiler_params=_params(1),
        name="mix_out",
    )(h, attn, proj, proj, w_s, b_s.T, attn_g.reshape(1, width), sgu_g.reshape(1, width),
      w_out, next_g.reshape(1, d))


def _mem_fold_kernel(mem_ref, g_ref, wk_ref, wv_ref, wq_ref, wo_ref, kq_ref, vo_ref):
    hd = wk_ref.shape[1]
    mk = _rms(mem_ref[...], g_ref[...]).astype(BF16)
    k = jnp.dot(mk, wk_ref[...].astype(BF16), preferred_element_type=F32).astype(BF16)
    v = jnp.dot(mk, wv_ref[...].astype(BF16), preferred_element_type=F32).astype(BF16)
    kq = jax.lax.dot_general(wq_ref[...].astype(BF16), k, (((1,), (1,)), ((), ())),
                             preferred_element_type=F32)
    kq_ref[...] = (kq * (hd ** -0.5)).astype(BF16)
    vo_ref[...] = jnp.dot(v, wo_ref[...].astype(BF16), preferred_element_type=F32).astype(BF16)


def _mem_fold(mem, g, w_k, w_v, w_q, w_o):
    n_mem, d = mem.shape
    hd = d // N_MEM_HEADS
    cols = pl.BlockSpec((d, hd), lambda h: (0, h))
    return pl.pallas_call(
        _mem_fold_kernel,
        grid=(N_MEM_HEADS,),
        in_specs=[pl.BlockSpec((n_mem, d), lambda h: (0, 0)),
                  pl.BlockSpec((1, d), lambda h: (0, 0)),
                  cols, cols, cols,
                  pl.BlockSpec((hd, d), lambda h: (h, 0))],
        out_specs=[pl.BlockSpec((d, n_mem), lambda h: (0, h)),
                   pl.BlockSpec((n_mem, d), lambda h: (h, 0))],
        out_shape=[jax.ShapeDtypeStruct((d, N_MEM_HEADS * n_mem), BF16),
                   jax.ShapeDtypeStruct((N_MEM_HEADS * n_mem, d), BF16)],
        compiler_params=_params(1),
        name="mem_fold",
    )(mem, g.reshape(1, d), w_k, w_v, w_q, w_o)


def _mem_cross_kernel(h_ref, xq_ref, kq_ref, vo_ref, ng_ref, out_ref, xn_ref, p_ref):
    n_mem = kq_ref.shape[1] // N_MEM_HEADS
    for i in range(N_MEM_HEADS):
        sl = slice(i * n_mem, (i + 1) * n_mem)
        s = jnp.dot(xq_ref[...], kq_ref[:, sl], preferred_element_type=F32)
        m = jnp.max(s, axis=-1, keepdims=True)
        e = jnp.exp(s - m)
        p_ref[:, sl] = (e / jnp.sum(e, axis=-1, keepdims=True)).astype(BF16)
    h = h_ref[...] + jnp.dot(p_ref[...], vo_ref[...], preferred_element_type=F32)
    out_ref[...] = h
    xn_ref[...] = _rms(h, ng_ref[...]).astype(BF16)


def _mem_cross(h, xq, kq, vo, next_g, *, tm=512):
    seq, d = h.shape
    row = pl.BlockSpec((tm, d), lambda m: (m, 0))
    return pl.pallas_call(
        _mem_cross_kernel,
        grid=(seq // tm,),
        in_specs=[row, row, _resident(kq.shape), _resident(vo.shape),
                  pl.BlockSpec((1, d), lambda m: (0, 0))],
        out_specs=[row, row],
        out_shape=[jax.ShapeDtypeStruct((seq, d), F32), jax.ShapeDtypeStruct((seq, d), BF16)],
        scratch_shapes=[pltpu.VMEM((tm, kq.shape[1]), BF16)],
        compiler_params=_params(1),
        name="mem_cross",
    )(h, xq, kq, vo, next_g.reshape(1, d))


def kernel(x, mem, positions, ffn1_norm, ffn1_w_gate, ffn1_w_up, ffn1_w_down, mix_norm, w_in,
           sgu_ln_gain, sgu_ln_bias, sgu_w_s, sgu_b_s, attn_out_gain, sgu_out_gain, w_out,
           mem_q_norm, mem_kv_norm, mem_w_q, mem_w_k, mem_w_v, mem_w_o, ffn2_norm,
           ffn2_w_gate, ffn2_w_up, ffn2_w_down, final_norm):
    batch, seq, d = x.shape
    depth = ffn1_norm.shape[0]
    width = attn_out_gain.shape[1]
    assert batch == 1 and mem.shape[0] == 1 and depth == 1
    layer = 0

    h = x.reshape(seq, d)
    mem2 = mem.reshape(mem.shape[1], d)

    cos, sin, xn = _prep(positions, h, ffn1_norm[layer])
    a, wd_bf = _ffn_up(xn, ffn1_w_gate[layer], ffn1_w_up[layer], ffn1_w_down[layer])
    h, xn, w_out_bf = _ffn_down(a, h, wd_bf, mix_norm[layer], (w_out[layer],), final=False)
    proj, wg2_bf, wu2_bf, wd2_bf = _proj(
        xn, w_in[layer], cos, sin, sgu_ln_gain[layer], sgu_ln_bias[layer],
        (ffn2_w_gate[layer], ffn2_w_up[layer], ffn2_w_down[layer]), width=width)
    attn = _attention(proj, width=width)
    h, xn = _mix(h, attn, proj, sgu_w_s[layer], sgu_b_s[layer], attn_out_gain[layer],
                 sgu_out_gain[layer], w_out_bf, mem_q_norm[layer], width=width)
    kq, vo = _mem_fold(mem2, mem_kv_norm[layer], mem_w_k[layer], mem_w_v[layer],
                       mem_w_q[layer], mem_w_o[layer])
    h, xn = _mem_cross(h, xn, kq, vo, ffn2_norm[layer])
    a, = _ffn_up(xn, wg2_bf, wu2_bf)
    out = _ffn_down(a, h, wd2_bf, final_norm, final=True)[0]
    return out.reshape(batch, seq, d)
```

```python
import functools

import jax
import jax.numpy as jnp
import numpy as np
from jax.experimental import pallas as pl
from jax.experimental.pallas import tpu as pltpu

F32 = jnp.float32
BF16 = jnp.bfloat16

EPS = 1e-6
ROPE_THETA = 10000.0
FFN_RES_SCALE = 0.5
HEAD_DIM = 64
BLOCK = 128
DILATED_BRANCHES = ((128, 1), (512, 4), (2048, 16))
SGU_CHUNK = 128
SGU_GROUP_DIM = 128
N_MEM_HEADS = 4
LANES = 128
MXU_COLS = 256
CAST_ROWS = 256
FFN_UP_ROWS = 1024

FFN_DOWN_ROWS = 256

VMEM_LIMIT = 56 * 1024 * 1024
VMEM_LIMIT_DOWN = 61 * 1024 * 1024


def _params(n_axes, vmem=VMEM_LIMIT):
    return pltpu.CompilerParams(
        dimension_semantics=("arbitrary",) * n_axes, vmem_limit_bytes=vmem)


def _resident(shape):
    return pl.BlockSpec(shape, lambda *_: (0,) * len(shape), pipeline_mode=pl.Buffered(1))


def _rms(x, g):
    return x * jax.lax.rsqrt(jnp.mean(x * x, axis=-1, keepdims=True) + EPS) * g


def _gelu(x):
    c = np.sqrt(2.0 / np.pi).astype(np.float32)
    return 0.5 * x * (1.0 + jnp.tanh(c * (x + 0.044715 * (x * x * x))))


def _cast_to_bf16(src_ref, dst_ref):
    rows = src_ref.shape[0]
    chunk = min(CAST_ROWS, rows)

    def body(i, carry):
        sl = pl.ds(pl.multiple_of(i * chunk, chunk), chunk)
        dst_ref[sl, :] = src_ref[sl, :].astype(BF16)
        return carry

    jax.lax.fori_loop(0, rows // chunk, body, 0)


def _prep_kernel(pos_ref, freq_ref, x_ref, g_ref, cos_ref, sin_ref, xn_ref):
    ang = pos_ref[...].astype(F32) * freq_ref[...]
    lane = jax.lax.broadcasted_iota(jnp.int32, ang.shape, 1)
    sign = jnp.where(lane % HEAD_DIM < HEAD_DIM // 2, -1.0, 1.0)
    cos_ref[...] = jnp.cos(ang)
    sin_ref[...] = jnp.sin(ang) * sign
    xn_ref[...] = _rms(x_ref[...], g_ref[...]).astype(BF16)


def _prep(positions, x, g, *, tm=1024):
    seq, d = x.shape
    half = HEAD_DIM // 2
    inv_freq = ROPE_THETA ** (-jnp.arange(0, HEAD_DIM, 2, dtype=F32) / HEAD_DIM)
    freq_lane = jnp.tile(inv_freq, LANES // half).reshape(1, LANES)
    pos = positions.reshape(seq, 1)
    row = lambda w: pl.BlockSpec((tm, w), lambda i: (i, 0))
    const = lambda w: pl.BlockSpec((1, w), lambda i: (0, 0))
    return pl.pallas_call(
        _prep_kernel,
        grid=(seq // tm,),
        in_specs=[row(1), const(LANES), row(d), const(d)],
        out_specs=[row(LANES), row(LANES), row(d)],
        out_shape=[jax.ShapeDtypeStruct((seq, LANES), F32)] * 2
        + [jax.ShapeDtypeStruct((seq, d), BF16)],
        compiler_params=_params(1),
        name="prep",
    )(pos, freq_lane, x, g.reshape(1, d))


def _ffn_up_kernel(*refs, precast):
    if precast:
        xn_ref, wgb, wub, a_ref = refs
    else:
        xn_ref, wg_ref, wu_ref, wd_ref, a_ref, wdb_ref, wgb, wub = refs
        wdb_ref[...] = wd_ref[...].astype(BF16)

        @pl.when(pl.program_id(1) == 0)
        def _():
            _cast_to_bf16(wg_ref, wgb)
            _cast_to_bf16(wu_ref, wub)

    for r0 in range(0, xn_ref.shape[0], FFN_UP_ROWS):
        rows = slice(r0, r0 + FFN_UP_ROWS)
        xn = xn_ref[rows, :]
        gate = jnp.dot(xn, wgb[...], preferred_element_type=F32)
        up = jnp.dot(xn, wub[...], preferred_element_type=F32)
        a_ref[rows, :] = ((gate * (FFN_RES_SCALE / (1.0 + jnp.exp(-gate)))) * up).astype(BF16)


def _ffn_up(xn, wg, wu, wd=None, *, tf=512):
    seq, d = xn.shape
    dff = wg.shape[1]
    precast = wd is None
    tm = 2 * FFN_UP_ROWS
    m_steps = seq // tm
    in_specs = [pl.BlockSpec((tm, d), lambda f, m: (m, 0)),
                pl.BlockSpec((d, tf), lambda f, m: (0, f)),
                pl.BlockSpec((d, tf), lambda f, m: (0, f))]
    out_specs = [pl.BlockSpec((tm, tf), lambda f, m: (m, f))]
    out_shape = [jax.ShapeDtypeStruct((seq, dff), BF16)]
    scratch, args = [], [xn, wg, wu]
    if not precast:
        assert dff % ((dff // tf) * m_steps) == 0
        wd_rows = dff // ((dff // tf) * m_steps)
        wd_spec = pl.BlockSpec((wd_rows, d), lambda f, m: (f * m_steps + m, 0))
        in_specs.append(wd_spec)
        out_specs.append(wd_spec)
        out_shape.append(jax.ShapeDtypeStruct((dff, d), BF16))
        scratch = [pltpu.VMEM((d, tf), BF16), pltpu.VMEM((d, tf), BF16)]
        args.append(wd)
    return pl.pallas_call(
        functools.partial(_ffn_up_kernel, precast=precast),
        grid=(dff // tf, seq // tm),
        in_specs=in_specs,
        out_specs=out_specs,
        out_shape=out_shape,
        scratch_shapes=scratch,
        compiler_params=_params(2),
        name="ffn_up_bf16" if precast else "ffn_up",
    )(*args)


def _ffn_down_kernel(*refs, final, n_cast):
    a_ref, x_ref, wd_ref, g_ref = refs[:4]
    cast_in = refs[4:4 + n_cast]
    out_refs = refs[4 + n_cast:]
    for r0 in range(0, x_ref.shape[0], FFN_DOWN_ROWS):
        rows = slice(r0, r0 + FFN_DOWN_ROWS)
        h = x_ref[rows, :] + jnp.dot(a_ref[rows, :], wd_ref[...], preferred_element_type=F32)
        if final:
            out_refs[0][rows, :] = _rms(h, g_ref[...])
        else:
            out_refs[0][rows, :] = h
            out_refs[1][rows, :] = _rms(h, g_ref[...]).astype(BF16)
    for src, dst in zip(cast_in, out_refs[2:]):
        dst[...] = src[...].astype(BF16)


def _ffn_down(a, x, wd_bf, g, cast_weights=(), *, final, tm=512):
    seq, d = x.shape
    dff = a.shape[1]
    steps = seq // tm
    row = pl.BlockSpec((tm, d), lambda m: (m, 0))
    out_specs = [row] if final else [row, row]
    out_shape = [jax.ShapeDtypeStruct((seq, d), F32)]
    if not final:
        out_shape.append(jax.ShapeDtypeStruct((seq, d), BF16))
    cast_specs = []
    for w in cast_weights:
        assert not final and w.shape[0] % steps == 0
        spec = pl.BlockSpec((w.shape[0] // steps, w.shape[1]), lambda m: (m, 0))
        cast_specs.append(spec)
        out_specs.append(spec)
        out_shape.append(jax.ShapeDtypeStruct(w.shape, BF16))
    return pl.pallas_call(
        functools.partial(_ffn_down_kernel, final=final, n_cast=len(cast_weights)),
        grid=(steps,),
        in_specs=[pl.BlockSpec((tm, dff), lambda m: (m, 0)), row,
                  _resident((dff, d)), pl.BlockSpec((1, d), lambda m: (0, 0))] + cast_specs,
        out_specs=out_specs,
        out_shape=out_shape,
        compiler_params=_params(1, vmem=VMEM_LIMIT_DOWN),
        name="ffn_down_final" if final else "ffn_down",
    )(a, x, wd_bf, g.reshape(1, d), *cast_weights)


def _proj_kernel(*refs, q_scale, n_cast, cast_steps):
    xn_ref, w_ref, cos_ref, sin_ref, lng_ref, lnb_ref = refs[:6]
    cast_in = refs[6:6 + n_cast]
    o_ref = refs[6 + n_cast]
    cast_out = refs[7 + n_cast:7 + 2 * n_cast]
    wb, t_ref = refs[7 + 2 * n_cast:]
    n = pl.program_id(0)
    width = o_ref.shape[1]

    @pl.when(n * pl.num_programs(1) + pl.program_id(1) < cast_steps)
    def _():
        for src, dst in zip(cast_in, cast_out):
            dst[...] = src[...].astype(BF16)

    @pl.when(pl.program_id(1) == 0)
    def _():
        _cast_to_bf16(w_ref, wb)

    half = xn_ref.shape[0] // 2

    def chunked(epilogue):
        for j in range(width // MXU_COLS):
            c0 = j * MXU_COLS
            for r0 in (0, half):
                rows = slice(r0, r0 + half)
                y = jnp.dot(xn_ref[rows, :], wb[:, c0:c0 + MXU_COLS],
                            preferred_element_type=F32)
                epilogue(y, rows, c0)

    def rope(scale):
        def epilogue(y, rows, c0):
            cos = cos_ref[rows, :]
            sin = sin_ref[rows, :]
            lane = jax.lax.broadcasted_iota(jnp.int32, cos.shape, 1)
            first = lane % HEAD_DIM < HEAD_DIM // 2
            for j in range(MXU_COLS // LANES):
                blk = y[:, j * LANES:(j + 1) * LANES]
                partner = jnp.where(first,
                                    pltpu.roll(blk, LANES - HEAD_DIM // 2, 1),
                                    pltpu.roll(blk, HEAD_DIM // 2, 1))
                out = blk * cos + partner * sin
                if scale != 1.0:
                    out = out * scale
                o_ref[rows, c0 + j * LANES:c0 + (j + 1) * LANES] = out.astype(o_ref.dtype)
        return epilogue

    def store(fn):
        def epilogue(y, rows, c0):
            o_ref[rows, c0:c0 + MXU_COLS] = fn(y).astype(o_ref.dtype)
        return epilogue

    @pl.when(n == 0)
    def _():
        chunked(rope(q_scale))

    @pl.when(n == 1)
    def _():
        chunked(rope(1.0))

    @pl.when(n == 2)
    def _():
        chunked(store(lambda y: y))

    @pl.when(n == 3)
    def _():
        chunked(store(_gelu))

    @pl.when(n == 4)
    def _():
        sums = {0: [], half: []}

        def epilogue(y, rows, c0):
            t = _gelu(y)
            t_ref[rows, c0:c0 + MXU_COLS] = t
            sums[rows.start].append(jnp.sum(t, axis=-1, keepdims=True))

        chunked(epilogue)
        for r0 in (0, half):
            rows = slice(r0, r0 + half)
            mu = sum(sums[r0]) * (1.0 / width)
            tc = t_ref[rows, :] - mu
            t = tc * jax.lax.rsqrt(jnp.mean(tc * tc, axis=-1, keepdims=True) + EPS)
            o_ref[rows, :] = (t * lng_ref[...] + lnb_ref[...]).astype(o_ref.dtype)


def _proj(xn, w_in, cos, sin, ln_g, ln_b, cast_weights=(), *, width, tm=1024, cast_steps=32):
    seq, d = xn.shape
    d_in = w_in.shape[1]
    assert d_in == 5 * width
    m_steps = seq // tm
    assert cast_steps <= (d_in // width) * m_steps
    cast_specs, cast_shapes = [], []
    for w in cast_weights:
        assert w.shape[0] % cast_steps == 0
        cast_specs.append(pl.BlockSpec(
            (w.shape[0] // cast_steps, w.shape[1]),
            lambda n, m: (jnp.minimum(n * m_steps + m, cast_steps - 1), 0)))
        cast_shapes.append(jax.ShapeDtypeStruct(w.shape, BF16))
    return pl.pallas_call(
        functools.partial(_proj_kernel, q_scale=HEAD_DIM ** -0.5 * np.log2(np.e),
                          n_cast=len(cast_weights), cast_steps=cast_steps),
        grid=(d_in // width, m_steps),
        in_specs=[
            pl.BlockSpec((tm, d), lambda n, m: (m, 0)),
            pl.BlockSpec((d, width), lambda n, m: (0, n)),
            pl.BlockSpec((tm, LANES), lambda n, m: (m, 0)),
            pl.BlockSpec((tm, LANES), lambda n, m: (m, 0)),
            pl.BlockSpec((1, width), lambda n, m: (0, 0)),
            pl.BlockSpec((1, width), lambda n, m: (0, 0)),
        ] + cast_specs,
        out_specs=[pl.BlockSpec((tm, width), lambda n, m: (m, n))] + cast_specs,
        out_shape=[jax.ShapeDtypeStruct((seq, d_in), BF16)] + cast_shapes,
        scratch_shapes=[pltpu.VMEM((d, width), BF16), pltpu.VMEM((tm, width), F32)],
        compiler_params=_params(2),
        name="proj",
    )(xn, w_in, cos, sin, ln_g.reshape(1, width), ln_b.reshape(1, width), *cast_weights)


def _attn_kernel(q_ref, kc_ref, kp_ref, vc_ref, vp_ref, o_ref,
                 stage, q4, k4, v4, kb, vb, m4, l4, acc4, m_s, l_s, acc_s, bias, *, dil):
    t = pl.program_id(0)
    sup = q_ref.shape[0]
    sub = sup // dil
    n_units = sup // BLOCK

    def split(dst, n_rows):
        part = n_rows // dil
        for b in range(dil):
            dst[b * part:(b + 1) * part, :] = stage[pl.ds(b, part, stride=dil), :]

    stage[:sup] = kp_ref[...].astype(F32)
    stage[sup:] = kc_ref[...].astype(F32)
    split(k4, 2 * sup)
    stage[:sup] = vp_ref[...].astype(F32)
    stage[sup:] = vc_ref[...].astype(F32)
    split(v4, 2 * sup)
    stage[:sup] = q_ref[...].astype(F32)
    split(q4, sup)
    kb[:BLOCK] = kp_ref[sup - BLOCK:, :]
    kb[BLOCK:] = kc_ref[...]
    vb[:BLOCK] = vp_ref[sup - BLOCK:, :]
    vb[BLOCK:] = vc_ref[...]

    qi = jax.lax.broadcasted_iota(jnp.int32, (BLOCK, 2 * BLOCK), 0)
    kj = jax.lax.broadcasted_iota(jnp.int32, (BLOCK, 2 * BLOCK), 1)
    diff = qi + BLOCK - kj
    band = (diff >= 0) & (diff <= BLOCK)
    bias[0] = jnp.where(band & (kj >= BLOCK), 0.0, -jnp.inf)
    bias[1] = jnp.where(band, 0.0, -jnp.inf)
    low = jax.lax.broadcasted_iota(jnp.int32, (BLOCK, LANES), 1) < HEAD_DIM

    def head_pair(q, k, v, has_prev):
        mask = bias[has_prev.astype(jnp.int32)]
        ms, ls, os = [], [], []
        for hh in range(2):
            qm = jnp.where(low if hh == 0 else ~low, q, jnp.zeros_like(q))
            s = jax.lax.dot_general(qm, k, (((1,), (1,)), ((), ())),
                                    preferred_element_type=F32)
            s = s + mask
            m = jnp.max(s, axis=-1, keepdims=True)
            e = jnp.exp2(s - m)
            ms.append(m)
            ls.append(jnp.sum(e, axis=-1, keepdims=True))
            os.append(jnp.dot(e.astype(BF16), v, preferred_element_type=F32))
        return (jnp.where(low, ms[0], ms[1]), jnp.where(low, ls[0], ls[1]),
                jnp.where(low, os[0], os[1]))

    def merge(refs, rows, stats):
        m_ref, l_ref, acc_ref = refs
        m_u, l_u, o_u = stats
        m_old = m_ref[rows, :]
        m_new = jnp.maximum(m_old, m_u)
        a_old = jnp.exp2(m_old - m_new)
        a_u = jnp.exp2(m_u - m_new)
        m_ref[rows, :] = m_new
        l_ref[rows, :] = a_old * l_ref[rows, :] + a_u * l_u
        acc_ref[rows, :] = a_old * acc_ref[rows, :] + a_u * o_u

    def mid_unit(u, carry):
        r = u // (sub // BLOCK)
        n = u % (sub // BLOCK)
        rows = pl.ds(pl.multiple_of(r * sub + n * BLOCK, BLOCK), BLOCK)
        kv_rows = pl.ds(pl.multiple_of(r * 2 * sub + sub + (n - 1) * BLOCK, BLOCK), 2 * BLOCK)
        m_u, l_u, o_u = head_pair(q4[rows, :].astype(BF16), k4[kv_rows, :].astype(BF16),
                                  v4[kv_rows, :].astype(BF16), (t > 0) | (n > 0))
        m4[rows, :] = m_u
        l4[rows, :] = l_u
        acc4[rows, :] = o_u
        return carry

    def wide_unit(r, carry):
        a = r // dil
        b = r % dil
        rows = pl.ds(b * sub + a, BLOCK, stride=dil)
        kv_rows = pl.ds(b * 2 * sub + a, 2 * BLOCK, stride=dil)
        stats = head_pair(q4[rows, :].astype(BF16), k4[kv_rows, :].astype(BF16),
                          v4[kv_rows, :].astype(BF16), t > 0)
        merge((m4, l4, acc4), rows, stats)
        return carry

    def near_unit(n, carry):
        rows = pl.ds(pl.multiple_of(n * BLOCK, BLOCK), BLOCK)
        kv_rows = pl.ds(pl.multiple_of(n * BLOCK, BLOCK), 2 * BLOCK)
        stats = head_pair(q_ref[rows, :], kb[kv_rows, :], vb[kv_rows, :], (t > 0) | (n > 0))
        merge((m_s, l_s, acc_s), rows, stats)
        return carry

    jax.lax.fori_loop(0, n_units, mid_unit, 0, unroll=True)
    jax.lax.fori_loop(0, n_units, wide_unit, 0, unroll=True)
    for b in range(dil):
        src = slice(b * sub, (b + 1) * sub)
        dst = pl.ds(b, sub, stride=dil)
        m_s[dst, :] = m4[src, :]
        l_s[dst, :] = l4[src, :]
        acc_s[dst, :] = acc4[src, :]
    jax.lax.fori_loop(0, n_units, near_unit, 0, unroll=True)

    o_ref[...] = (acc_s[...] / l_s[...]).astype(o_ref.dtype)


def _attention(proj, *, width):
    seq, d_in = proj.shape
    dilations = tuple(d for _, d in DILATED_BRANCHES)
    steps = {w // d for w, d in DILATED_BRANCHES}
    assert steps == {BLOCK}, "every branch must span exactly one previous block"
    dil = dilations[1]
    assert dilations == (1, dil, dil * dil)
    sup = BLOCK * dil * dil
    seg = width // LANES
    cur = lambda i: pl.BlockSpec((sup, LANES), lambda t, p: (t, i * seg + p))
    prev = lambda i: pl.BlockSpec((sup, LANES),
                                  lambda t, p: (jnp.maximum(t - 1, 0), i * seg + p))
    f32_rows = lambda n: pltpu.VMEM((n, LANES), F32)
    return pl.pallas_call(
        functools.partial(_attn_kernel, dil=dil),
        grid=(seq // sup, seg),
        in_specs=[cur(0), cur(1), prev(1), cur(2), prev(2)],
        out_specs=pl.BlockSpec((sup, LANES), lambda t, p: (t, p)),
        out_shape=jax.ShapeDtypeStruct((seq, width), BF16),
        scratch_shapes=[f32_rows(2 * sup),
                        f32_rows(sup), f32_rows(2 * sup), f32_rows(2 * sup),
                        pltpu.VMEM((sup + BLOCK, LANES), BF16),
                        pltpu.VMEM((sup + BLOCK, LANES), BF16),
                        f32_rows(sup), f32_rows(sup), f32_rows(sup),
                        f32_rows(sup), f32_rows(sup), f32_rows(sup),
                        pltpu.VMEM((2, BLOCK, 2 * BLOCK), F32)],
        compiler_params=_params(2),
        name="dilated_attn",
    )(proj, proj, proj, proj, proj)


def _mix_kernel(h_ref, attn_ref, u_ref, gv_ref, ws_ref, bs_ref, ag_ref, sg_ref, wo_ref, ng_ref,
                out_ref, xn_ref, mixed_ref):
    tm = h_ref.shape[0]
    width = u_ref.shape[1]
    mixed_ref[:, :width] = _rms(attn_ref[...].astype(F32), ag_ref[...]).astype(BF16)

    ci = jax.lax.broadcasted_iota(jnp.int32, (SGU_CHUNK, SGU_CHUNK), 0)
    cj = jax.lax.broadcasted_iota(jnp.int32, (SGU_CHUNK, SGU_CHUNK), 1)
    causal = cj <= ci
    for g in range(width // SGU_GROUP_DIM):
        gs = slice(g * SGU_GROUP_DIM, (g + 1) * SGU_GROUP_DIM)
        w = jnp.where(causal, ws_ref[g], 0.0).astype(BF16)
        b = bs_ref[:, g:g + 1]
        for c in range(tm // SGU_CHUNK):
            rs = slice(c * SGU_CHUNK, (c + 1) * SGU_CHUNK)
            sv = jnp.dot(w, gv_ref[rs, gs], preferred_element_type=F32) + b
            out_ref[rs, gs] = u_ref[rs, gs].astype(F32) * sv
    sgu = out_ref[:, :width]
    mixed_ref[:, width:] = _rms(sgu, sg_ref[...]).astype(BF16)

    h = h_ref[...] + jnp.dot(mixed_ref[...], wo_ref[...], preferred_element_type=F32)
    out_ref[...] = h
    xn_ref[...] = _rms(h, ng_ref[...]).astype(BF16)


def _mix(h, attn, proj, w_s, b_s, attn_g, sgu_g, w_out, next_g, *, width, tm=512):
    seq, d = h.shape
    n_grp = w_s.shape[0]
    row = lambda m: (m, 0)
    const2 = lambda m: (0, 0)
    return pl.pallas_call(
        _mix_kernel,
        grid=(seq // tm,),
        in_specs=[pl.BlockSpec((tm, d), row),
                  pl.BlockSpec((tm, width), row),
                  pl.BlockSpec((tm, width), lambda m: (m, 3)),
                  pl.BlockSpec((tm, width), lambda m: (m, 4)),
                  pl.BlockSpec((n_grp, SGU_CHUNK, SGU_CHUNK), lambda m: (0, 0, 0)),
                  pl.BlockSpec((SGU_CHUNK, n_grp), const2),
                  pl.BlockSpec((1, width), const2),
                  pl.BlockSpec((1, width), const2),
                  _resident((2 * width, d)),
                  pl.BlockSpec((1, d), const2)],
        out_specs=[pl.BlockSpec((tm, d), row), pl.BlockSpec((tm, d), row)],
        out_shape=[jax.ShapeDtypeStruct((seq, d), F32), jax.ShapeDtypeStruct((seq, d), BF16)],
        scratch_shapes=[pltpu.VMEM((tm, 2 * width), BF16)],
        compiler_params=_params(1),
        name="mix_out",
    )(h, attn, proj, proj, w_s, b_s.T, attn_g.reshape(1, width), sgu_g.reshape(1, width),
      w_out, next_g.reshape(1, d))


def _mem_fold_kernel(mem_ref, g_ref, wk_ref, wv_ref, wq_ref, wo_ref, kq_ref, vo_ref):
    hd = wk_ref.shape[1]
    mk = _rms(mem_ref[...], g_ref[...]).astype(BF16)
    k = jnp.dot(mk, wk_ref[...].astype(BF16), preferred_element_type=F32).astype(BF16)
    v = jnp.dot(mk, wv_ref[...].astype(BF16), preferred_element_type=F32).astype(BF16)
    kq = jax.lax.dot_general(wq_ref[...].astype(BF16), k, (((1,), (1,)), ((), ())),
                             preferred_element_type=F32)
    kq_ref[...] = (kq * (hd ** -0.5)).astype(BF16)
    vo_ref[...] = jnp.dot(v, wo_ref[...].astype(BF16), preferred_element_type=F32).astype(BF16)


def _mem_fold(mem, g, w_k, w_v, w_q, w_o):
    n_mem, d = mem.shape
    hd = d // N_MEM_HEADS
    cols = pl.BlockSpec((d, hd), lambda h: (0, h))
    return pl.pallas_call(
        _mem_fold_kernel,
        grid=(N_MEM_HEADS,),
        in_specs=[pl.BlockSpec((n_mem, d), lambda h: (0, 0)),
                  pl.BlockSpec((1, d), lambda h: (0, 0)),
                  cols, cols, cols,
                  pl.BlockSpec((hd, d), lambda h: (h, 0))],
        out_specs=[pl.BlockSpec((d, n_mem), lambda h: (0, h)),
                   pl.BlockSpec((n_mem, d), lambda h: (h, 0))],
        out_shape=[jax.ShapeDtypeStruct((d, N_MEM_HEADS * n_mem), BF16),
                   jax.ShapeDtypeStruct((N_MEM_HEADS * n_mem, d), BF16)],
        compiler_params=_params(1),
        name="mem_fold",
    )(mem, g.reshape(1, d), w_k, w_v, w_q, w_o)


def _mem_cross_kernel(h_ref, xq_ref, kq_ref, vo_ref, ng_ref, out_ref, xn_ref, p_ref):
    n_mem = kq_ref.shape[1] // N_MEM_HEADS
    for i in range(N_MEM_HEADS):
        sl = slice(i * n_mem, (i + 1) * n_mem)
        s = jnp.dot(xq_ref[...], kq_ref[:, sl], preferred_element_type=F32)
        m = jnp.max(s, axis=-1, keepdims=True)
        e = jnp.exp(s - m)
        p_ref[:, sl] = (e / jnp.sum(e, axis=-1, keepdims=True)).astype(BF16)
    h = h_ref[...] + jnp.dot(p_ref[...], vo_ref[...], preferred_element_type=F32)
    out_ref[...] = h
    xn_ref[...] = _rms(h, ng_ref[...]).astype(BF16)


def _mem_cross(h, xq, kq, vo, next_g, *, tm=512):
    seq, d = h.shape
    row = pl.BlockSpec((tm, d), lambda m: (m, 0))
    return pl.pallas_call(
        _mem_cross_kernel,
        grid=(seq // tm,),
        in_specs=[row, row, _resident(kq.shape), _resident(vo.shape),
                  pl.BlockSpec((1, d), lambda m: (0, 0))],
        out_specs=[row, row],
        out_shape=[jax.ShapeDtypeStruct((seq, d), F32), jax.ShapeDtypeStruct((seq, d), BF16)],
        scratch_shapes=[pltpu.VMEM((tm, kq.shape[1]), BF16)],
        compiler_params=_params(1),
        name="mem_cross",
    )(h, xq, kq, vo, next_g.reshape(1, d))


def kernel(x, mem, positions, ffn1_norm, ffn1_w_gate, ffn1_w_up, ffn1_w_down, mix_norm, w_in,
           sgu_ln_gain, sgu_ln_bias, sgu_w_s, sgu_b_s, attn_out_gain, sgu_out_gain, w_out,
           mem_q_norm, mem_kv_norm, mem_w_q, mem_w_k, mem_w_v, mem_w_o, ffn2_norm,
           ffn2_w_gate, ffn2_w_up, ffn2_w_down, final_norm):
    batch, seq, d = x.shape
    depth = ffn1_norm.shape[0]
    width = attn_out_gain.shape[1]
    assert batch == 1 and mem.shape[0] == 1 and depth == 1
    layer = 0

    h = x.reshape(seq, d)
    mem2 = mem.reshape(mem.shape[1], d)

    cos, sin, xn = _prep(positions, h, ffn1_norm[layer])
    a, wd_bf = _ffn_up(xn, ffn1_w_gate[layer], ffn1_w_up[layer], ffn1_w_down[layer])
    h, xn, w_out_bf = _ffn_down(a, h, wd_bf, mix_norm[layer], (w_out[layer],), final=False)
    proj, wg2_bf, wu2_bf, wd2_bf = _proj(
        xn, w_in[layer], cos, sin, sgu_ln_gain[layer], sgu_ln_bias[layer],
        (ffn2_w_gate[layer], ffn2_w_up[layer], ffn2_w_down[layer]), width=width)
    attn = _attention(proj, width=width)
    h, xn = _mix(h, attn, proj, sgu_w_s[layer], sgu_b_s[layer], attn_out_gain[layer],
                 sgu_out_gain[layer], w_out_bf, mem_q_norm[layer], width=width)
    kq, vo = _mem_fold(mem2, mem_kv_norm[layer], mem_w_k[layer], mem_w_v[layer],
                       mem_w_q[layer], mem_w_o[layer])
    h, xn = _mem_cross(h, xn, kq, vo, ffn2_norm[layer])
    a, = _ffn_up(xn, wg2_bf, wu2_bf)
    out = _ffn_down(a, h, wd2_bf, final_norm, final=True)[0]
    return out.reshape(batch, seq, d)
```

```python
import functools

import jax
import jax.numpy as jnp
import numpy as np
from jax.experimental import pallas as pl
from jax.experimental.pallas import tpu as pltpu

F32 = jnp.float32
BF16 = jnp.bfloat16

EPS = 1e-6
ROPE_THETA = 10000.0
FFN_RES_SCALE = 0.5
HEAD_DIM = 64
BLOCK = 128
DILATED_BRANCHES = ((128, 1), (512, 4), (2048, 16))
SGU_CHUNK = 128
SGU_GROUP_DIM = 128
N_MEM_HEADS = 4
LANES = 128
MXU_COLS = 256
CAST_ROWS = 256
FFN_UP_ROWS = 1024

VMEM_LIMIT = 56 * 1024 * 1024


def _params(n_axes, vmem=VMEM_LIMIT):
    return pltpu.CompilerParams(
        dimension_semantics=("arbitrary",) * n_axes, vmem_limit_bytes=vmem)


def _resident(shape):
    return pl.BlockSpec(shape, lambda *_: (0,) * len(shape), pipeline_mode=pl.Buffered(1))


def _rms(x, g):
    return x * jax.lax.rsqrt(jnp.mean(x * x, axis=-1, keepdims=True) + EPS) * g


def _gelu(x):
    c = np.sqrt(2.0 / np.pi).astype(np.float32)
    return 0.5 * x * (1.0 + jnp.tanh(c * (x + 0.044715 * (x * x * x))))


def _cast_to_bf16(src_ref, dst_ref):
    rows = src_ref.shape[0]
    chunk = min(CAST_ROWS, rows)

    def body(i, carry):
        sl = pl.ds(pl.multiple_of(i * chunk, chunk), chunk)
        dst_ref[sl, :] = src_ref[sl, :].astype(BF16)
        return carry

    jax.lax.fori_loop(0, rows // chunk, body, 0)


def _prep_kernel(pos_ref, freq_ref, x_ref, g_ref, cos_ref, sin_ref, xn_ref):
    ang = pos_ref[...].astype(F32) * freq_ref[...]
    lane = jax.lax.broadcasted_iota(jnp.int32, ang.shape, 1)
    sign = jnp.where(lane % HEAD_DIM < HEAD_DIM // 2, -1.0, 1.0)
    cos_ref[...] = jnp.cos(ang)
    sin_ref[...] = jnp.sin(ang) * sign
    xn_ref[...] = _rms(x_ref[...], g_ref[...]).astype(BF16)


def _prep(positions, x, g, *, tm=1024):
    seq, d = x.shape
    half = HEAD_DIM // 2
    inv_freq = ROPE_THETA ** (-jnp.arange(0, HEAD_DIM, 2, dtype=F32) / HEAD_DIM)
    freq_lane = jnp.tile(inv_freq, LANES // half).reshape(1, LANES)
    pos = positions.reshape(seq, 1)
    row = lambda w: pl.BlockSpec((tm, w), lambda i: (i, 0))
    const = lambda w: pl.BlockSpec((1, w), lambda i: (0, 0))
    return pl.pallas_call(
        _prep_kernel,
        grid=(seq // tm,),
        in_specs=[row(1), const(LANES), row(d), const(d)],
        out_specs=[row(LANES), row(LANES), row(d)],
        out_shape=[jax.ShapeDtypeStruct((seq, LANES), F32)] * 2
        + [jax.ShapeDtypeStruct((seq, d), BF16)],
        compiler_params=_params(1),
        name="prep",
    )(pos, freq_lane, x, g.reshape(1, d))


def _ffn_up_kernel(*refs, precast):
    if precast:
        xn_ref, wgb, wub, a_ref = refs
    else:
        xn_ref, wg_ref, wu_ref, wd_ref, a_ref, wdb_ref, wgb, wub = refs
        wdb_ref[...] = wd_ref[...].astype(BF16)

        @pl.when(pl.program_id(1) == 0)
        def _():
            _cast_to_bf16(wg_ref, wgb)
            _cast_to_bf16(wu_ref, wub)

    for r0 in range(0, xn_ref.shape[0], FFN_UP_ROWS):
        rows = slice(r0, r0 + FFN_UP_ROWS)
        xn = xn_ref[rows, :]
        gate = jnp.dot(xn, wgb[...], preferred_element_type=F32)
        up = jnp.dot(xn, wub[...], preferred_element_type=F32)
        a_ref[rows, :] = ((gate * (FFN_RES_SCALE / (1.0 + jnp.exp(-gate)))) * up).astype(BF16)


def _ffn_up(xn, wg, wu, wd=None, *, tf=512):
    seq, d = xn.shape
    dff = wg.shape[1]
    precast = wd is None
    tm = 2 * FFN_UP_ROWS
    m_steps = seq // tm
    in_specs = [pl.BlockSpec((tm, d), lambda f, m: (m, 0)),
                pl.BlockSpec((d, tf), lambda f, m: (0, f)),
                pl.BlockSpec((d, tf), lambda f, m: (0, f))]
    out_specs = [pl.BlockSpec((tm, tf), lambda f, m: (m, f))]
    out_shape = [jax.ShapeDtypeStruct((seq, dff), BF16)]
    scratch, args = [], [xn, wg, wu]
    if not precast:
        assert dff % ((dff // tf) * m_steps) == 0
        wd_rows = dff // ((dff // tf) * m_steps)
        wd_spec = pl.BlockSpec((wd_rows, d), lambda f, m: (f * m_steps + m, 0))
        in_specs.append(wd_spec)
        out_specs.append(wd_spec)
        out_shape.append(jax.ShapeDtypeStruct((dff, d), BF16))
        scratch = [pltpu.VMEM((d, tf), BF16), pltpu.VMEM((d, tf), BF16)]
        args.append(wd)
    return pl.pallas_call(
        functools.partial(_ffn_up_kernel, precast=precast),
        grid=(dff // tf, seq // tm),
        in_specs=in_specs,
        out_specs=out_specs,
        out_shape=out_shape,
        scratch_shapes=scratch,
        compiler_params=_params(2),
        name="ffn_up_bf16" if precast else "ffn_up",
    )(*args)


def _ffn_down_kernel(*refs, final, n_cast):
    a_ref, x_ref, wd_ref, g_ref = refs[:4]
    cast_in = refs[4:4 + n_cast]
    out_refs = refs[4 + n_cast:]
    h = x_ref[...] + jnp.dot(a_ref[...], wd_ref[...], preferred_element_type=F32)
    if final:
        out_refs[0][...] = _rms(h, g_ref[...])
    else:
        out_refs[0][...] = h
        out_refs[1][...] = _rms(h, g_ref[...]).astype(BF16)
        for src, dst in zip(cast_in, out_refs[2:]):
            dst[...] = src[...].astype(BF16)


def _ffn_down(a, x, wd_bf, g, cast_weights=(), *, final, tm=256):
    seq, d = x.shape
    dff = a.shape[1]
    steps = seq // tm
    row = pl.BlockSpec((tm, d), lambda m: (m, 0))
    out_specs = [row] if final else [row, row]
    out_shape = [jax.ShapeDtypeStruct((seq, d), F32)]
    if not final:
        out_shape.append(jax.ShapeDtypeStruct((seq, d), BF16))
    cast_specs = []
    for w in cast_weights:
        assert not final and w.shape[0] % steps == 0
        spec = pl.BlockSpec((w.shape[0] // steps, w.shape[1]), lambda m: (m, 0))
        cast_specs.append(spec)
        out_specs.append(spec)
        out_shape.append(jax.ShapeDtypeStruct(w.shape, BF16))
    return pl.pallas_call(
        functools.partial(_ffn_down_kernel, final=final, n_cast=len(cast_weights)),
        grid=(steps,),
        in_specs=[pl.BlockSpec((tm, dff), lambda m: (m, 0)), row,
                  _resident((dff, d)), pl.BlockSpec((1, d), lambda m: (0, 0))] + cast_specs,
        out_specs=out_specs,
        out_shape=out_shape,
        compiler_params=_params(1),
        name="ffn_down_final" if final else "ffn_down",
    )(a, x, wd_bf, g.reshape(1, d), *cast_weights)


def _proj_kernel(xn_ref, w_ref, cos_ref, sin_ref, lng_ref, lnb_ref, o_ref, t_ref, *, q_scale):
    width = t_ref.shape[1]

    def segment(seg, epilogue):
        for j in range(width // MXU_COLS):
            c0 = seg * width + j * MXU_COLS
            y = jnp.dot(xn_ref[...], w_ref[:, c0:c0 + MXU_COLS], preferred_element_type=F32)
            epilogue(y, c0)

    def rope(scale):
        def epilogue(y, c0):
            cos = cos_ref[...]
            sin = sin_ref[...]
            lane = jax.lax.broadcasted_iota(jnp.int32, cos.shape, 1)
            first = lane % HEAD_DIM < HEAD_DIM // 2
            for j in range(MXU_COLS // LANES):
                blk = y[:, j * LANES:(j + 1) * LANES]
                partner = jnp.where(first,
                                    pltpu.roll(blk, LANES - HEAD_DIM // 2, 1),
                                    pltpu.roll(blk, HEAD_DIM // 2, 1))
                out = blk * cos + partner * sin
                if scale != 1.0:
                    out = out * scale
                o_ref[:, c0 + j * LANES:c0 + (j + 1) * LANES] = out.astype(o_ref.dtype)
        return epilogue

    def store(fn):
        def epilogue(y, c0):
            o_ref[:, c0:c0 + MXU_COLS] = fn(y).astype(o_ref.dtype)
        return epilogue

    sums = []

    def gelu_partial(y, c0):
        t = _gelu(y)
        t_ref[:, c0 - 4 * width:c0 - 4 * width + MXU_COLS] = t
        sums.append(jnp.sum(t, axis=-1, keepdims=True))

    segment(4, gelu_partial)
    mu = sum(sums) * (1.0 / width)
    tc = t_ref[...] - mu
    t = tc * jax.lax.rsqrt(jnp.mean(tc * tc, axis=-1, keepdims=True) + EPS)
    o_ref[:, 4 * width:] = (t * lng_ref[...] + lnb_ref[...]).astype(o_ref.dtype)
    segment(3, store(_gelu))
    segment(0, rope(q_scale))
    segment(1, rope(1.0))
    segment(2, store(lambda y: y))


def _proj(xn, w_in_bf, cos, sin, ln_g, ln_b, *, width, tm=512):
    seq, d = xn.shape
    d_in = w_in_bf.shape[1]
    assert d_in == 5 * width
    return pl.pallas_call(
        functools.partial(_proj_kernel, q_scale=HEAD_DIM ** -0.5 * np.log2(np.e)),
        grid=(seq // tm,),
        in_specs=[
            pl.BlockSpec((tm, d), lambda m: (m, 0)),
            _resident((d, d_in)),
            pl.BlockSpec((tm, LANES), lambda m: (m, 0)),
            pl.BlockSpec((tm, LANES), lambda m: (m, 0)),
            pl.BlockSpec((1, width), lambda m: (0, 0)),
            pl.BlockSpec((1, width), lambda m: (0, 0)),
        ],
        out_specs=pl.BlockSpec((tm, d_in), lambda m: (m, 0)),
        out_shape=jax.ShapeDtypeStruct((seq, d_in), BF16),
        scratch_shapes=[pltpu.VMEM((tm, width), F32)],
        compiler_params=_params(1),
        name="proj",
    )(xn, w_in_bf, cos, sin, ln_g.reshape(1, width), ln_b.reshape(1, width))


def _attn_kernel(*refs, dil, n_cast):
    q_ref, kc_ref, kp_ref, vc_ref, vp_ref = refs[:5]
    cast_in = refs[5:5 + n_cast]
    o_ref = refs[5 + n_cast]
    cast_out = refs[6 + n_cast:6 + 2 * n_cast]
    stage, q4, k4, v4, kb, vb, m4, l4, acc4, m_s, l_s, acc_s, bias = refs[6 + 2 * n_cast:]
    for src, dst in zip(cast_in, cast_out):
        dst[...] = src[...].astype(BF16)
    t = pl.program_id(0)
    sup = q_ref.shape[0]
    sub = sup // dil
    n_units = sup // BLOCK

    def split(dst, n_rows):
        part = n_rows // dil
        for b in range(dil):
            dst[b * part:(b + 1) * part, :] = stage[pl.ds(b, part, stride=dil), :]

    stage[:sup] = kp_ref[...].astype(F32)
    stage[sup:] = kc_ref[...].astype(F32)
    split(k4, 2 * sup)
    stage[:sup] = vp_ref[...].astype(F32)
    stage[sup:] = vc_ref[...].astype(F32)
    split(v4, 2 * sup)
    stage[:sup] = q_ref[...].astype(F32)
    split(q4, sup)
    kb[:BLOCK] = kp_ref[sup - BLOCK:, :]
    kb[BLOCK:] = kc_ref[...]
    vb[:BLOCK] = vp_ref[sup - BLOCK:, :]
    vb[BLOCK:] = vc_ref[...]

    qi = jax.lax.broadcasted_iota(jnp.int32, (BLOCK, 2 * BLOCK), 0)
    kj = jax.lax.broadcasted_iota(jnp.int32, (BLOCK, 2 * BLOCK), 1)
    diff = qi + BLOCK - kj
    band = (diff >= 0) & (diff <= BLOCK)
    bias[0] = jnp.where(band & (kj >= BLOCK), 0.0, -jnp.inf)
    bias[1] = jnp.where(band, 0.0, -jnp.inf)
    low = jax.lax.broadcasted_iota(jnp.int32, (BLOCK, LANES), 1) < HEAD_DIM

    def head_pair(q, k, v, has_prev):
        mask = bias[has_prev.astype(jnp.int32)]
        ms, ls, os = [], [], []
        for hh in range(2):
            qm = jnp.where(low if hh == 0 else ~low, q, jnp.zeros_like(q))
            s = jax.lax.dot_general(qm, k, (((1,), (1,)), ((), ())),
                                    preferred_element_type=F32)
            s = s + mask
            m = jnp.max(s, axis=-1, keepdims=True)
            e = jnp.exp2(s - m)
            ms.append(m)
            ls.append(jnp.sum(e, axis=-1, keepdims=True))
            os.append(jnp.dot(e.astype(BF16), v, preferred_element_type=F32))
        return (jnp.where(low, ms[0], ms[1]), jnp.where(low, ls[0], ls[1]),
                jnp.where(low, os[0], os[1]))

    def merge(refs, rows, stats):
        m_ref, l_ref, acc_ref = refs
        m_u, l_u, o_u = stats
        m_old = m_ref[rows, :]
        m_new = jnp.maximum(m_old, m_u)
        a_old = jnp.exp2(m_old - m_new)
        a_u = jnp.exp2(m_u - m_new)
        m_ref[rows, :] = m_new
        l_ref[rows, :] = a_old * l_ref[rows, :] + a_u * l_u
        acc_ref[rows, :] = a_old * acc_ref[rows, :] + a_u * o_u

    def mid_unit(u, carry):
        r = u // (sub // BLOCK)
        n = u % (sub // BLOCK)
        rows = pl.ds(pl.multiple_of(r * sub + n * BLOCK, BLOCK), BLOCK)
        kv_rows = pl.ds(pl.multiple_of(r * 2 * sub + sub + (n - 1) * BLOCK, BLOCK), 2 * BLOCK)
        m_u, l_u, o_u = head_pair(q4[rows, :].astype(BF16), k4[kv_rows, :].astype(BF16),
                                  v4[kv_rows, :].astype(BF16), (t > 0) | (n > 0))
        m4[rows, :] = m_u
        l4[rows, :] = l_u
        acc4[rows, :] = o_u
        return carry

    def wide_unit(r, carry):
        a = r // dil
        b = r % dil
        rows = pl.ds(b * sub + a, BLOCK, stride=dil)
        kv_rows = pl.ds(b * 2 * sub + a, 2 * BLOCK, stride=dil)
        stats = head_pair(q4[rows, :].astype(BF16), k4[kv_rows, :].astype(BF16),
                          v4[kv_rows, :].astype(BF16), t > 0)
        merge((m4, l4, acc4), rows, stats)
        return carry

    def near_unit(n, carry):
        rows = pl.ds(pl.multiple_of(n * BLOCK, BLOCK), BLOCK)
        kv_rows = pl.ds(pl.multiple_of(n * BLOCK, BLOCK), 2 * BLOCK)
        stats = head_pair(q_ref[rows, :], kb[kv_rows, :], vb[kv_rows, :], (t > 0) | (n > 0))
        merge((m_s, l_s, acc_s), rows, stats)
        return carry

    jax.lax.fori_loop(0, n_units, mid_unit, 0, unroll=True)
    jax.lax.fori_loop(0, n_units, wide_unit, 0, unroll=True)
    for b in range(dil):
        src = slice(b * sub, (b + 1) * sub)
        dst = pl.ds(b, sub, stride=dil)
        m_s[dst, :] = m4[src, :]
        l_s[dst, :] = l4[src, :]
        acc_s[dst, :] = acc4[src, :]
    jax.lax.fori_loop(0, n_units, near_unit, 0, unroll=True)

    o_ref[...] = (acc_s[...] / l_s[...]).astype(o_ref.dtype)


def _attention(proj, cast_weights=(), *, width):
    seq, d_in = proj.shape
    dilations = tuple(d for _, d in DILATED_BRANCHES)
    steps = {w // d for w, d in DILATED_BRANCHES}
    assert steps == {BLOCK}, "every branch must span exactly one previous block"
    dil = dilations[1]
    assert dilations == (1, dil, dil * dil)
    sup = BLOCK * dil * dil
    seg = width // LANES
    cur = lambda i: pl.BlockSpec((sup, LANES), lambda t, p: (t, i * seg + p))
    prev = lambda i: pl.BlockSpec((sup, LANES),
                                  lambda t, p: (jnp.maximum(t - 1, 0), i * seg + p))
    f32_rows = lambda n: pltpu.VMEM((n, LANES), F32)
    steps = (seq // sup) * seg
    cast_specs, cast_shapes = [], []
    for w in cast_weights:
        assert w.shape[0] % steps == 0
        cast_specs.append(pl.BlockSpec((w.shape[0] // steps, w.shape[1]),
                                       lambda t, p: (t * seg + p, 0)))
        cast_shapes.append(jax.ShapeDtypeStruct(w.shape, BF16))
    return pl.pallas_call(
        functools.partial(_attn_kernel, dil=dil, n_cast=len(cast_weights)),
        grid=(seq // sup, seg),
        in_specs=[cur(0), cur(1), prev(1), cur(2), prev(2)] + cast_specs,
        out_specs=[pl.BlockSpec((sup, LANES), lambda t, p: (t, p))] + cast_specs,
        out_shape=[jax.ShapeDtypeStruct((seq, width), BF16)] + cast_shapes,
        scratch_shapes=[f32_rows(2 * sup),
                        f32_rows(sup), f32_rows(2 * sup), f32_rows(2 * sup),
                        pltpu.VMEM((sup + BLOCK, LANES), BF16),
                        pltpu.VMEM((sup + BLOCK, LANES), BF16),
                        f32_rows(sup), f32_rows(sup), f32_rows(sup),
                        f32_rows(sup), f32_rows(sup), f32_rows(sup),
                        pltpu.VMEM((2, BLOCK, 2 * BLOCK), F32)],
        compiler_params=_params(2),
        name="dilated_attn",
    )(proj, proj, proj, proj, proj, *cast_weights)


def _mix_kernel(h_ref, attn_ref, u_ref, gv_ref, ws_ref, bs_ref, ag_ref, sg_ref, wo_ref, ng_ref,
                out_ref, xn_ref, mixed_ref):
    tm = h_ref.shape[0]
    width = u_ref.shape[1]
    mixed_ref[:, :width] = _rms(attn_ref[...].astype(F32), ag_ref[...]).astype(BF16)

    ci = jax.lax.broadcasted_iota(jnp.int32, (SGU_CHUNK, SGU_CHUNK), 0)
    cj = jax.lax.broadcasted_iota(jnp.int32, (SGU_CHUNK, SGU_CHUNK), 1)
    causal = cj <= ci
    for g in range(width // SGU_GROUP_DIM):
        gs = slice(g * SGU_GROUP_DIM, (g + 1) * SGU_GROUP_DIM)
        w = jnp.where(causal, ws_ref[g], 0.0).astype(BF16)
        b = bs_ref[:, g:g + 1]
        for c in range(tm // SGU_CHUNK):
            rs = slice(c * SGU_CHUNK, (c + 1) * SGU_CHUNK)
            sv = jnp.dot(w, gv_ref[rs, gs], preferred_element_type=F32) + b
            out_ref[rs, gs] = u_ref[rs, gs].astype(F32) * sv
    sgu = out_ref[:, :width]
    mixed_ref[:, width:] = _rms(sgu, sg_ref[...]).astype(BF16)

    h = h_ref[...] + jnp.dot(mixed_ref[...], wo_ref[...], preferred_element_type=F32)
    out_ref[...] = h
    xn_ref[...] = _rms(h, ng_ref[...]).astype(BF16)


def _mix(h, attn, proj, w_s, b_s, attn_g, sgu_g, w_out, next_g, *, width, tm=512):
    seq, d = h.shape
    n_grp = w_s.shape[0]
    row = lambda m: (m, 0)
    const2 = lambda m: (0, 0)
    return pl.pallas_call(
        _mix_kernel,
        grid=(seq // tm,),
        in_specs=[pl.BlockSpec((tm, d), row),
                  pl.BlockSpec((tm, width), row),
                  pl.BlockSpec((tm, width), lambda m: (m, 3)),
                  pl.BlockSpec((tm, width), lambda m: (m, 4)),
                  pl.BlockSpec((n_grp, SGU_CHUNK, SGU_CHUNK), lambda m: (0, 0, 0)),
                  pl.BlockSpec((SGU_CHUNK, n_grp), const2),
                  pl.BlockSpec((1, width), const2),
                  pl.BlockSpec((1, width), const2),
                  _resident((2 * width, d)),
                  pl.BlockSpec((1, d), const2)],
        out_specs=[pl.BlockSpec((tm, d), row), pl.BlockSpec((tm, d), row)],
        out_shape=[jax.ShapeDtypeStruct((seq, d), F32), jax.ShapeDtypeStruct((seq, d), BF16)],
        scratch_shapes=[pltpu.VMEM((tm, 2 * width), BF16)],
        compiler_params=_params(1),
        name="mix_out",
    )(h, attn, proj, proj, w_s, b_s.T, attn_g.reshape(1, width), sgu_g.reshape(1, width),
      w_out, next_g.reshape(1, d))


def _mem_fold_kernel(mem_ref, g_ref, wk_ref, wv_ref, wq_ref, wo_ref, kq_ref, vo_ref):
    hd = wk_ref.shape[1]
    mk = _rms(mem_ref[...], g_ref[...]).astype(BF16)
    k = jnp.dot(mk, wk_ref[...].astype(BF16), preferred_element_type=F32).astype(BF16)
    v = jnp.dot(mk, wv_ref[...].astype(BF16), preferred_element_type=F32).astype(BF16)
    kq = jax.lax.dot_general(wq_ref[...].astype(BF16), k, (((1,), (1,)), ((), ())),
                             preferred_element_type=F32)
    kq_ref[...] = (kq * (hd ** -0.5)).astype(BF16)
    vo_ref[...] = jnp.dot(v, wo_ref[...].astype(BF16), preferred_element_type=F32).astype(BF16)


def _mem_fold(mem, g, w_k, w_v, w_q, w_o):
    n_mem, d = mem.shape
    hd = d // N_MEM_HEADS
    cols = pl.BlockSpec((d, hd), lambda h: (0, h))
    return pl.pallas_call(
        _mem_fold_kernel,
        grid=(N_MEM_HEADS,),
        in_specs=[pl.BlockSpec((n_mem, d), lambda h: (0, 0)),
                  pl.BlockSpec((1, d), lambda h: (0, 0)),
                  cols, cols, cols,
                  pl.BlockSpec((hd, d), lambda h: (h, 0))],
        out_specs=[pl.BlockSpec((d, n_mem), lambda h: (0, h)),
                   pl.BlockSpec((n_mem, d), lambda h: (h, 0))],
        out_shape=[jax.ShapeDtypeStruct((d, N_MEM_HEADS * n_mem), BF16),
                   jax.ShapeDtypeStruct((N_MEM_HEADS * n_mem, d), BF16)],
        compiler_params=_params(1),
        name="mem_fold",
    )(mem, g.reshape(1, d), w_k, w_v, w_q, w_o)


def _mem_cross_kernel(h_ref, xq_ref, kq_ref, vo_ref, ng_ref, out_ref, xn_ref, p_ref):
    n_mem = kq_ref.shape[1] // N_MEM_HEADS
    for i in range(N_MEM_HEADS):
        sl = slice(i * n_mem, (i + 1) * n_mem)
        s = jnp.dot(xq_ref[...], kq_ref[:, sl], preferred_element_type=F32)
        m = jnp.max(s, axis=-1, keepdims=True)
        e = jnp.exp(s - m)
        p_ref[:, sl] = (e / jnp.sum(e, axis=-1, keepdims=True)).astype(BF16)
    h = h_ref[...] + jnp.dot(p_ref[...], vo_ref[...], preferred_element_type=F32)
    out_ref[...] = h
    xn_ref[...] = _rms(h, ng_ref[...]).astype(BF16)


def _mem_cross(h, xq, kq, vo, next_g, *, tm=512):
    seq, d = h.shape
    row = pl.BlockSpec((tm, d), lambda m: (m, 0))
    return pl.pallas_call(
        _mem_cross_kernel,
        grid=(seq // tm,),
        in_specs=[row, row, _resident(kq.shape), _resident(vo.shape),
                  pl.BlockSpec((1, d), lambda m: (0, 0))],
        out_specs=[row, row],
        out_shape=[jax.ShapeDtypeStruct((seq, d), F32), jax.ShapeDtypeStruct((seq, d), BF16)],
        scratch_shapes=[pltpu.VMEM((tm, kq.shape[1]), BF16)],
        compiler_params=_params(1),
        name="mem_cross",
    )(h, xq, kq, vo, next_g.reshape(1, d))


def kernel(x, mem, positions, ffn1_norm, ffn1_w_gate, ffn1_w_up, ffn1_w_down, mix_norm, w_in,
           sgu_ln_gain, sgu_ln_bias, sgu_w_s, sgu_b_s, attn_out_gain, sgu_out_gain, w_out,
           mem_q_norm, mem_kv_norm, mem_w_q, mem_w_k, mem_w_v, mem_w_o, ffn2_norm,
           ffn2_w_gate, ffn2_w_up, ffn2_w_down, final_norm):
    batch, seq, d = x.shape
    depth = ffn1_norm.shape[0]
    width = attn_out_gain.shape[1]
    assert batch == 1 and mem.shape[0] == 1 and depth == 1
    layer = 0

    h = x.reshape(seq, d)
    mem2 = mem.reshape(mem.shape[1], d)

    cos, sin, xn = _prep(positions, h, ffn1_norm[layer])
    a, wd_bf = _ffn_up(xn, ffn1_w_gate[layer], ffn1_w_up[layer], ffn1_w_down[layer])
    h, xn, w_out_bf, w_in_bf = _ffn_down(a, h, wd_bf, mix_norm[layer],
                                         (w_out[layer], w_in[layer]), final=False)
    proj = _proj(xn, w_in_bf, cos, sin, sgu_ln_gain[layer], sgu_ln_bias[layer], width=width)
    attn, wg2_bf, wu2_bf, wd2_bf = _attention(
        proj, (ffn2_w_gate[layer], ffn2_w_up[layer], ffn2_w_down[layer]), width=width)
    h, xn = _mix(h, attn, proj, sgu_w_s[layer], sgu_b_s[layer], attn_out_gain[layer],
                 sgu_out_gain[layer], w_out_bf, mem_q_norm[layer], width=width)
    kq, vo = _mem_fold(mem2, mem_kv_norm[layer], mem_w_k[layer], mem_w_v[layer],
                       mem_w_q[layer], mem_w_o[layer])
    h, xn = _mem_cross(h, xn, kq, vo, ffn2_norm[layer])
    a, = _ffn_up(xn, wg2_bf, wu2_bf)
    out = _ffn_down(a, h, wd2_bf, final_norm, final=True)[0]
    return out.reshape(batch, seq, d)
```

```python
import functools

import jax
import jax.numpy as jnp
import numpy as np
from jax.experimental import pallas as pl
from jax.experimental.pallas import tpu as pltpu

F32 = jnp.float32
BF16 = jnp.bfloat16

EPS = 1e-6
ROPE_THETA = 10000.0
FFN_RES_SCALE = 0.5
HEAD_DIM = 64
BLOCK = 128
DILATED_BRANCHES = ((128, 1), (512, 4), (2048, 16))
SGU_CHUNK = 128
SGU_GROUP_DIM = 128
N_MEM_HEADS = 4
LANES = 128
MXU_COLS = 256
CAST_ROWS = 256
FFN_UP_ROWS = 1024

VMEM_LIMIT = 56 * 1024 * 1024


def _params(n_axes, vmem=VMEM_LIMIT):
    return pltpu.CompilerParams(
        dimension_semantics=("arbitrary",) * n_axes, vmem_limit_bytes=vmem)


def _resident(shape):
    return pl.BlockSpec(shape, lambda *_: (0,) * len(shape), pipeline_mode=pl.Buffered(1))


def _rms(x, g):
    return x * jax.lax.rsqrt(jnp.mean(x * x, axis=-1, keepdims=True) + EPS) * g


def _gelu(x):
    c = np.sqrt(2.0 / np.pi).astype(np.float32)
    return 0.5 * x * (1.0 + jnp.tanh(c * (x + 0.044715 * (x * x * x))))


def _cast_to_bf16(src_ref, dst_ref):
    rows = src_ref.shape[0]
    chunk = min(CAST_ROWS, rows)

    def body(i, carry):
        sl = pl.ds(pl.multiple_of(i * chunk, chunk), chunk)
        dst_ref[sl, :] = src_ref[sl, :].astype(BF16)
        return carry

    jax.lax.fori_loop(0, rows // chunk, body, 0)


def _cast_jobs(weights, steps, index_map):
    specs, shapes = [], []
    for w in weights:
        assert w.shape[0] % steps == 0
        specs.append(pl.BlockSpec((w.shape[0] // steps, w.shape[1]), index_map))
        shapes.append(jax.ShapeDtypeStruct(w.shape, BF16))
    return specs, shapes


def _run_casts(srcs, dsts):
    for src, dst in zip(srcs, dsts):
        dst[...] = src[...].astype(BF16)


def _prep_kernel(pos_ref, freq_ref, x_ref, g_ref, cos_ref, sin_ref, xn_ref):
    ang = pos_ref[...].astype(F32) * freq_ref[...]
    lane = jax.lax.broadcasted_iota(jnp.int32, ang.shape, 1)
    sign = jnp.where(lane % HEAD_DIM < HEAD_DIM // 2, -1.0, 1.0)
    cos_ref[...] = jnp.cos(ang)
    sin_ref[...] = jnp.sin(ang) * sign
    xn_ref[...] = _rms(x_ref[...], g_ref[...]).astype(BF16)


def _prep(positions, x, g, *, tm=1024):
    seq, d = x.shape
    half = HEAD_DIM // 2
    inv_freq = ROPE_THETA ** (-jnp.arange(0, HEAD_DIM, 2, dtype=F32) / HEAD_DIM)
    freq_lane = jnp.tile(inv_freq, LANES // half).reshape(1, LANES)
    pos = positions.reshape(seq, 1)
    row = lambda w: pl.BlockSpec((tm, w), lambda i: (i, 0))
    const = lambda w: pl.BlockSpec((1, w), lambda i: (0, 0))
    return pl.pallas_call(
        _prep_kernel,
        grid=(seq // tm,),
        in_specs=[row(1), const(LANES), row(d), const(d)],
        out_specs=[row(LANES), row(LANES), row(d)],
        out_shape=[jax.ShapeDtypeStruct((seq, LANES), F32)] * 2
        + [jax.ShapeDtypeStruct((seq, d), BF16)],
        compiler_params=_params(1),
        name="prep",
    )(pos, freq_lane, x, g.reshape(1, d))


def _ffn_up_kernel(*refs, precast):
    if precast:
        xn_ref, wgb, wub, a_ref = refs
    else:
        xn_ref, wg_ref, wu_ref, wd_ref, a_ref, wdb_ref, wgb, wub = refs
        wdb_ref[...] = wd_ref[...].astype(BF16)

        @pl.when(pl.program_id(1) == 0)
        def _():
            _cast_to_bf16(wg_ref, wgb)
            _cast_to_bf16(wu_ref, wub)

    for r0 in range(0, xn_ref.shape[0], FFN_UP_ROWS):
        rows = slice(r0, r0 + FFN_UP_ROWS)
        xn = xn_ref[rows, :]
        gate = jnp.dot(xn, wgb[...], preferred_element_type=F32)
        up = jnp.dot(xn, wub[...], preferred_element_type=F32)
        a_ref[rows, :] = ((gate * (FFN_RES_SCALE / (1.0 + jnp.exp(-gate)))) * up).astype(BF16)


def _ffn_up(xn, wg, wu, wd=None, *, tf=512):
    seq, d = xn.shape
    dff = wg.shape[1]
    precast = wd is None
    tm = 2 * FFN_UP_ROWS
    m_steps = seq // tm
    in_specs = [pl.BlockSpec((tm, d), lambda f, m: (m, 0)),
                pl.BlockSpec((d, tf), lambda f, m: (0, f)),
                pl.BlockSpec((d, tf), lambda f, m: (0, f))]
    out_specs = [pl.BlockSpec((tm, tf), lambda f, m: (m, f))]
    out_shape = [jax.ShapeDtypeStruct((seq, dff), BF16)]
    scratch, args = [], [xn, wg, wu]
    if not precast:
        assert dff % ((dff // tf) * m_steps) == 0
        wd_rows = dff // ((dff // tf) * m_steps)
        wd_spec = pl.BlockSpec((wd_rows, d), lambda f, m: (f * m_steps + m, 0))
        in_specs.append(wd_spec)
        out_specs.append(wd_spec)
        out_shape.append(jax.ShapeDtypeStruct((dff, d), BF16))
        scratch = [pltpu.VMEM((d, tf), BF16), pltpu.VMEM((d, tf), BF16)]
        args.append(wd)
    return pl.pallas_call(
        functools.partial(_ffn_up_kernel, precast=precast),
        grid=(dff // tf, seq // tm),
        in_specs=in_specs,
        out_specs=out_specs,
        out_shape=out_shape,
        scratch_shapes=scratch,
        compiler_params=_params(2),
        name="ffn_up_bf16" if precast else "ffn_up",
    )(*args)


def _ffn_down_kernel(*refs, final, n_cast):
    a_ref, x_ref, wd_ref, g_ref = refs[:4]
    cast_in = refs[4:4 + n_cast]
    out_refs = refs[4 + n_cast:]
    h = x_ref[...] + jnp.dot(a_ref[...], wd_ref[...], preferred_element_type=F32)
    if final:
        out_refs[0][...] = _rms(h, g_ref[...])
    else:
        out_refs[0][...] = h
        out_refs[1][...] = _rms(h, g_ref[...]).astype(BF16)
        _run_casts(cast_in, out_refs[2:])


def _ffn_down(a, x, wd_bf, g, cast_weights=(), *, final, tm=256):
    seq, d = x.shape
    dff = a.shape[1]
    steps = seq // tm
    row = pl.BlockSpec((tm, d), lambda m: (m, 0))
    out_specs = [row] if final else [row, row]
    out_shape = [jax.ShapeDtypeStruct((seq, d), F32)]
    if not final:
        out_shape.append(jax.ShapeDtypeStruct((seq, d), BF16))
    assert not (final and cast_weights)
    cast_specs, cast_shapes = _cast_jobs(cast_weights, steps, lambda m: (m, 0))
    out_specs += cast_specs
    out_shape += cast_shapes
    return pl.pallas_call(
        functools.partial(_ffn_down_kernel, final=final, n_cast=len(cast_weights)),
        grid=(steps,),
        in_specs=[pl.BlockSpec((tm, dff), lambda m: (m, 0)), row,
                  _resident((dff, d)), pl.BlockSpec((1, d), lambda m: (0, 0))] + cast_specs,
        out_specs=out_specs,
        out_shape=out_shape,
        compiler_params=_params(1),
        name="ffn_down_final" if final else "ffn_down",
    )(a, x, wd_bf, g.reshape(1, d), *cast_weights)


def _proj_kernel(xn_ref, w_ref, cos_ref, sin_ref, lng_ref, lnb_ref, o_ref, t_ref, *, q_scale):
    width = t_ref.shape[1]

    def segment(seg, epilogue):
        for j in range(width // MXU_COLS):
            c0 = seg * width + j * MXU_COLS
            y = jnp.dot(xn_ref[...], w_ref[:, c0:c0 + MXU_COLS], preferred_element_type=F32)
            epilogue(y, c0)

    def rope(scale):
        def epilogue(y, c0):
            cos = cos_ref[...]
            sin = sin_ref[...]
            lane = jax.lax.broadcasted_iota(jnp.int32, cos.shape, 1)
            first = lane % HEAD_DIM < HEAD_DIM // 2
            for j in range(MXU_COLS // LANES):
                blk = y[:, j * LANES:(j + 1) * LANES]
                partner = jnp.where(first,
                                    pltpu.roll(blk, LANES - HEAD_DIM // 2, 1),
                                    pltpu.roll(blk, HEAD_DIM // 2, 1))
                out = blk * cos + partner * sin
                if scale != 1.0:
                    out = out * scale
                o_ref[:, c0 + j * LANES:c0 + (j + 1) * LANES] = out.astype(o_ref.dtype)
        return epilogue

    def store(fn):
        def epilogue(y, c0):
            o_ref[:, c0:c0 + MXU_COLS] = fn(y).astype(o_ref.dtype)
        return epilogue

    sums = []

    def gelu_partial(y, c0):
        t = _gelu(y)
        t_ref[:, c0 - 4 * width:c0 - 4 * width + MXU_COLS] = t
        sums.append(jnp.sum(t, axis=-1, keepdims=True))

    segment(4, gelu_partial)
    mu = sum(sums) * (1.0 / width)
    tc = t_ref[...] - mu
    t = tc * jax.lax.rsqrt(jnp.mean(tc * tc, axis=-1, keepdims=True) + EPS)
    o_ref[:, 4 * width:] = (t * lng_ref[...] + lnb_ref[...]).astype(o_ref.dtype)
    segment(3, store(_gelu))
    segment(0, rope(q_scale))
    segment(1, rope(1.0))
    segment(2, store(lambda y: y))


def _proj(xn, w_in_bf, cos, sin, ln_g, ln_b, *, width, tm=512):
    seq, d = xn.shape
    d_in = w_in_bf.shape[1]
    assert d_in == 5 * width
    return pl.pallas_call(
        functools.partial(_proj_kernel, q_scale=HEAD_DIM ** -0.5 * np.log2(np.e)),
        grid=(seq // tm,),
        in_specs=[
            pl.BlockSpec((tm, d), lambda m: (m, 0)),
            _resident((d, d_in)),
            pl.BlockSpec((tm, LANES), lambda m: (m, 0)),
            pl.BlockSpec((tm, LANES), lambda m: (m, 0)),
            pl.BlockSpec((1, width), lambda m: (0, 0)),
            pl.BlockSpec((1, width), lambda m: (0, 0)),
        ],
        out_specs=pl.BlockSpec((tm, d_in), lambda m: (m, 0)),
        out_shape=jax.ShapeDtypeStruct((seq, d_in), BF16),
        scratch_shapes=[pltpu.VMEM((tm, width), F32)],
        compiler_params=_params(1),
        name="proj",
    )(xn, w_in_bf, cos, sin, ln_g.reshape(1, width), ln_b.reshape(1, width))


def _attn_kernel(q_ref, kc_ref, kp_ref, vc_ref, vp_ref, o_ref,
                 stage, q4, k4, v4, kb, vb, m4, l4, acc4, m_s, l_s, acc_s, bias, *, dil):
    t = pl.program_id(0)
    sup = q_ref.shape[0]
    sub = sup // dil
    n_units = sup // BLOCK

    def split(dst, n_rows):
        part = n_rows // dil
        for b in range(dil):
            dst[b * part:(b + 1) * part, :] = stage[pl.ds(b, part, stride=dil), :]

    stage[:sup] = kp_ref[...].astype(F32)
    stage[sup:] = kc_ref[...].astype(F32)
    split(k4, 2 * sup)
    stage[:sup] = vp_ref[...].astype(F32)
    stage[sup:] = vc_ref[...].astype(F32)
    split(v4, 2 * sup)
    stage[:sup] = q_ref[...].astype(F32)
    split(q4, sup)
    kb[:BLOCK] = kp_ref[sup - BLOCK:, :]
    kb[BLOCK:] = kc_ref[...]
    vb[:BLOCK] = vp_ref[sup - BLOCK:, :]
    vb[BLOCK:] = vc_ref[...]

    qi = jax.lax.broadcasted_iota(jnp.int32, (BLOCK, 2 * BLOCK), 0)
    kj = jax.lax.broadcasted_iota(jnp.int32, (BLOCK, 2 * BLOCK), 1)
    diff = qi + BLOCK - kj
    band = (diff >= 0) & (diff <= BLOCK)
    bias[0] = jnp.where(band & (kj >= BLOCK), 0.0, -jnp.inf)
    bias[1] = jnp.where(band, 0.0, -jnp.inf)
    low = jax.lax.broadcasted_iota(jnp.int32, (BLOCK, LANES), 1) < HEAD_DIM

    def head_pair(q, k, v, has_prev):
        mask = bias[has_prev.astype(jnp.int32)]
        ms, ls, os = [], [], []
        for hh in range(2):
            qm = jnp.where(low if hh == 0 else ~low, q, jnp.zeros_like(q))
            s = jax.lax.dot_general(qm, k, (((1,), (1,)), ((), ())),
                                    preferred_element_type=F32)
            s = s + mask
            m = jnp.max(s, axis=-1, keepdims=True)
            e = jnp.exp2(s - m)
            ms.append(m)
            ls.append(jnp.sum(e, axis=-1, keepdims=True))
            os.append(jnp.dot(e.astype(BF16), v, preferred_element_type=F32))
        return (jnp.where(low, ms[0], ms[1]), jnp.where(low, ls[0], ls[1]),
                jnp.where(low, os[0], os[1]))

    def merge(refs, rows, stats):
        m_ref, l_ref, acc_ref = refs
        m_u, l_u, o_u = stats
        m_old = m_ref[rows, :]
        m_new = jnp.maximum(m_old, m_u)
        a_old = jnp.exp2(m_old - m_new)
        a_u = jnp.exp2(m_u - m_new)
        m_ref[rows, :] = m_new
        l_ref[rows, :] = a_old * l_ref[rows, :] + a_u * l_u
        acc_ref[rows, :] = a_old * acc_ref[rows, :] + a_u * o_u

    def mid_unit(u, carry):
        r = u // (sub // BLOCK)
        n = u % (sub // BLOCK)
        rows = pl.ds(pl.multiple_of(r * sub + n * BLOCK, BLOCK), BLOCK)
        kv_rows = pl.ds(pl.multiple_of(r * 2 * sub + sub + (n - 1) * BLOCK, BLOCK), 2 * BLOCK)
        m_u, l_u, o_u = head_pair(q4[rows, :].astype(BF16), k4[kv_rows, :].astype(BF16),
                                  v4[kv_rows, :].astype(BF16), (t > 0) | (n > 0))
        m4[rows, :] = m_u
        l4[rows, :] = l_u
        acc4[rows, :] = o_u
        return carry

    def wide_unit(r, carry):
        a = r // dil
        b = r % dil
        rows = pl.ds(b * sub + a, BLOCK, stride=dil)
        kv_rows = pl.ds(b * 2 * sub + a, 2 * BLOCK, stride=dil)
        stats = head_pair(q4[rows, :].astype(BF16), k4[kv_rows, :].astype(BF16),
                          v4[kv_rows, :].astype(BF16), t > 0)
        merge((m4, l4, acc4), rows, stats)
        return carry

    def near_unit(n, carry):
        rows = pl.ds(pl.multiple_of(n * BLOCK, BLOCK), BLOCK)
        kv_rows = pl.ds(pl.multiple_of(n * BLOCK, BLOCK), 2 * BLOCK)
        stats = head_pair(q_ref[rows, :], kb[kv_rows, :], vb[kv_rows, :], (t > 0) | (n > 0))
        merge((m_s, l_s, acc_s), rows, stats)
        return carry

    jax.lax.fori_loop(0, n_units, mid_unit, 0, unroll=True)
    jax.lax.fori_loop(0, n_units, wide_unit, 0, unroll=True)
    for b in range(dil):
        src = slice(b * sub, (b + 1) * sub)
        dst = pl.ds(b, sub, stride=dil)
        m_s[dst, :] = m4[src, :]
        l_s[dst, :] = l4[src, :]
        acc_s[dst, :] = acc4[src, :]
    jax.lax.fori_loop(0, n_units, near_unit, 0, unroll=True)

    o_ref[...] = (acc_s[...] / l_s[...]).astype(o_ref.dtype)


def _attention(proj, *, width):
    seq, d_in = proj.shape
    dilations = tuple(d for _, d in DILATED_BRANCHES)
    steps = {w // d for w, d in DILATED_BRANCHES}
    assert steps == {BLOCK}, "every branch must span exactly one previous block"
    dil = dilations[1]
    assert dilations == (1, dil, dil * dil)
    sup = BLOCK * dil * dil
    seg = width // LANES
    cur = lambda i: pl.BlockSpec((sup, LANES), lambda t, p: (t, i * seg + p))
    prev = lambda i: pl.BlockSpec((sup, LANES),
                                  lambda t, p: (jnp.maximum(t - 1, 0), i * seg + p))
    f32_rows = lambda n: pltpu.VMEM((n, LANES), F32)
    return pl.pallas_call(
        functools.partial(_attn_kernel, dil=dil),
        grid=(seq // sup, seg),
        in_specs=[cur(0), cur(1), prev(1), cur(2), prev(2)],
        out_specs=pl.BlockSpec((sup, LANES), lambda t, p: (t, p)),
        out_shape=jax.ShapeDtypeStruct((seq, width), BF16),
        scratch_shapes=[f32_rows(2 * sup),
                        f32_rows(sup), f32_rows(2 * sup), f32_rows(2 * sup),
                        pltpu.VMEM((sup + BLOCK, LANES), BF16),
                        pltpu.VMEM((sup + BLOCK, LANES), BF16),
                        f32_rows(sup), f32_rows(sup), f32_rows(sup),
                        f32_rows(sup), f32_rows(sup), f32_rows(sup),
                        pltpu.VMEM((2, BLOCK, 2 * BLOCK), F32)],
        compiler_params=_params(2),
        name="dilated_attn",
    )(proj, proj, proj, proj, proj)


def _mix_kernel(*refs, n_cast):
    (h_ref, attn_ref, u_ref, gv_ref, ws_ref, bs_ref, ag_ref, sg_ref, wo_ref, ng_ref) = refs[:10]
    out_ref, xn_ref = refs[10 + n_cast:12 + n_cast]
    mixed_ref = refs[-1]
    _run_casts(refs[10:10 + n_cast], refs[12 + n_cast:12 + 2 * n_cast])
    tm = h_ref.shape[0]
    width = u_ref.shape[1]
    mixed_ref[:, :width] = _rms(attn_ref[...].astype(F32), ag_ref[...]).astype(BF16)

    ci = jax.lax.broadcasted_iota(jnp.int32, (SGU_CHUNK, SGU_CHUNK), 0)
    cj = jax.lax.broadcasted_iota(jnp.int32, (SGU_CHUNK, SGU_CHUNK), 1)
    causal = cj <= ci
    for g in range(width // SGU_GROUP_DIM):
        gs = slice(g * SGU_GROUP_DIM, (g + 1) * SGU_GROUP_DIM)
        w = jnp.where(causal, ws_ref[g], 0.0).astype(BF16)
        b = bs_ref[:, g:g + 1]
        for c in range(tm // SGU_CHUNK):
            rs = slice(c * SGU_CHUNK, (c + 1) * SGU_CHUNK)
            sv = jnp.dot(w, gv_ref[rs, gs], preferred_element_type=F32) + b
            out_ref[rs, gs] = u_ref[rs, gs].astype(F32) * sv
    sgu = out_ref[:, :width]
    mixed_ref[:, width:] = _rms(sgu, sg_ref[...]).astype(BF16)

    h = h_ref[...] + jnp.dot(mixed_ref[...], wo_ref[...], preferred_element_type=F32)
    out_ref[...] = h
    xn_ref[...] = _rms(h, ng_ref[...]).astype(BF16)


def _mix(h, attn, proj, w_s, b_s, attn_g, sgu_g, w_out, next_g, cast_weights=(), *, width,
         tm=512):
    seq, d = h.shape
    n_grp = w_s.shape[0]
    row = lambda m: (m, 0)
    const2 = lambda m: (0, 0)
    cast_specs, cast_shapes = _cast_jobs(cast_weights, seq // tm, row)
    return pl.pallas_call(
        functools.partial(_mix_kernel, n_cast=len(cast_weights)),
        grid=(seq // tm,),
        in_specs=[pl.BlockSpec((tm, d), row),
                  pl.BlockSpec((tm, width), row),
                  pl.BlockSpec((tm, width), lambda m: (m, 3)),
                  pl.BlockSpec((tm, width), lambda m: (m, 4)),
                  pl.BlockSpec((n_grp, SGU_CHUNK, SGU_CHUNK), lambda m: (0, 0, 0)),
                  pl.BlockSpec((SGU_CHUNK, n_grp), const2),
                  pl.BlockSpec((1, width), const2),
                  pl.BlockSpec((1, width), const2),
                  _resident((2 * width, d)),
                  pl.BlockSpec((1, d), const2)] + cast_specs,
        out_specs=[pl.BlockSpec((tm, d), row), pl.BlockSpec((tm, d), row)] + cast_specs,
        out_shape=[jax.ShapeDtypeStruct((seq, d), F32), jax.ShapeDtypeStruct((seq, d), BF16)]
        + cast_shapes,
        scratch_shapes=[pltpu.VMEM((tm, 2 * width), BF16)],
        compiler_params=_params(1),
        name="mix_out",
    )(h, attn, proj, proj, w_s, b_s.T, attn_g.reshape(1, width), sgu_g.reshape(1, width),
      w_out, next_g.reshape(1, d), *cast_weights)


def _mem_fold_kernel(mem_ref, g_ref, wk_ref, wv_ref, wq_ref, wo_ref, kq_ref, vo_ref):
    hd = wk_ref.shape[1]
    mk = _rms(mem_ref[...], g_ref[...]).astype(BF16)
    k = jnp.dot(mk, wk_ref[...].astype(BF16), preferred_element_type=F32).astype(BF16)
    v = jnp.dot(mk, wv_ref[...].astype(BF16), preferred_element_type=F32).astype(BF16)
    kq = jax.lax.dot_general(wq_ref[...].astype(BF16), k, (((1,), (1,)), ((), ())),
                             preferred_element_type=F32)
    kq_ref[...] = (kq * (hd ** -0.5)).astype(BF16)
    vo_ref[...] = jnp.dot(v, wo_ref[...].astype(BF16), preferred_element_type=F32).astype(BF16)


def _mem_fold(mem, g, w_k, w_v, w_q, w_o):
    n_mem, d = mem.shape
    hd = d // N_MEM_HEADS
    cols = pl.BlockSpec((d, hd), lambda h: (0, h))
    return pl.pallas_call(
        _mem_fold_kernel,
        grid=(N_MEM_HEADS,),
        in_specs=[pl.BlockSpec((n_mem, d), lambda h: (0, 0)),
                  pl.BlockSpec((1, d), lambda h: (0, 0)),
                  cols, cols, cols,
                  pl.BlockSpec((hd, d), lambda h: (h, 0))],
        out_specs=[pl.BlockSpec((d, n_mem), lambda h: (0, h)),
                   pl.BlockSpec((n_mem, d), lambda h: (h, 0))],
        out_shape=[jax.ShapeDtypeStruct((d, N_MEM_HEADS * n_mem), BF16),
                   jax.ShapeDtypeStruct((N_MEM_HEADS * n_mem, d), BF16)],
        compiler_params=_params(1),
        name="mem_fold",
    )(mem, g.reshape(1, d), w_k, w_v, w_q, w_o)


def _mem_cross_kernel(*refs, n_cast):
    h_ref, xq_ref, kq_ref, vo_ref, ng_ref = refs[:5]
    out_ref, xn_ref = refs[5 + n_cast:7 + n_cast]
    p_ref = refs[-1]
    _run_casts(refs[5:5 + n_cast], refs[7 + n_cast:7 + 2 * n_cast])
    n_mem = kq_ref.shape[1] // N_MEM_HEADS
    for i in range(N_MEM_HEADS):
        sl = slice(i * n_mem, (i + 1) * n_mem)
        s = jnp.dot(xq_ref[...], kq_ref[:, sl], preferred_element_type=F32)
        m = jnp.max(s, axis=-1, keepdims=True)
        e = jnp.exp(s - m)
        p_ref[:, sl] = (e / jnp.sum(e, axis=-1, keepdims=True)).astype(BF16)
    h = h_ref[...] + jnp.dot(p_ref[...], vo_ref[...], preferred_element_type=F32)
    out_ref[...] = h
    xn_ref[...] = _rms(h, ng_ref[...]).astype(BF16)


def _mem_cross(h, xq, kq, vo, next_g, cast_weights=(), *, tm=512):
    seq, d = h.shape
    row = pl.BlockSpec((tm, d), lambda m: (m, 0))
    cast_specs, cast_shapes = _cast_jobs(cast_weights, seq // tm, lambda m: (m, 0))
    return pl.pallas_call(
        functools.partial(_mem_cross_kernel, n_cast=len(cast_weights)),
        grid=(seq // tm,),
        in_specs=[row, row, _resident(kq.shape), _resident(vo.shape),
                  pl.BlockSpec((1, d), lambda m: (0, 0))] + cast_specs,
        out_specs=[row, row] + cast_specs,
        out_shape=[jax.ShapeDtypeStruct((seq, d), F32), jax.ShapeDtypeStruct((seq, d), BF16)]
        + cast_shapes,
        scratch_shapes=[pltpu.VMEM((tm, kq.shape[1]), BF16)],
        compiler_params=_params(1),
        name="mem_cross",
    )(h, xq, kq, vo, next_g.reshape(1, d), *cast_weights)


def kernel(x, mem, positions, ffn1_norm, ffn1_w_gate, ffn1_w_up, ffn1_w_down, mix_norm, w_in,
           sgu_ln_gain, sgu_ln_bias, sgu_w_s, sgu_b_s, attn_out_gain, sgu_out_gain, w_out,
           mem_q_norm, mem_kv_norm, mem_w_q, mem_w_k, mem_w_v, mem_w_o, ffn2_norm,
           ffn2_w_gate, ffn2_w_up, ffn2_w_down, final_norm):
    batch, seq, d = x.shape
    depth = ffn1_norm.shape[0]
    width = attn_out_gain.shape[1]
    assert batch == 1 and mem.shape[0] == 1 and depth == 1
    layer = 0

    h = x.reshape(seq, d)
    mem2 = mem.reshape(mem.shape[1], d)

    cos, sin, xn = _prep(positions, h, ffn1_norm[layer])
    a, wd_bf = _ffn_up(xn, ffn1_w_gate[layer], ffn1_w_up[layer], ffn1_w_down[layer])
    h, xn, w_out_bf, w_in_bf = _ffn_down(a, h, wd_bf, mix_norm[layer],
                                         (w_out[layer], w_in[layer]), final=False)
    proj = _proj(xn, w_in_bf, cos, sin, sgu_ln_gain[layer], sgu_ln_bias[layer], width=width)
    attn = _attention(proj, width=width)
    h, xn, wg2_bf = _mix(h, attn, proj, sgu_w_s[layer], sgu_b_s[layer], attn_out_gain[layer],
                         sgu_out_gain[layer], w_out_bf, mem_q_norm[layer],
                         (ffn2_w_gate[layer],), width=width)
    kq, vo = _mem_fold(mem2, mem_kv_norm[layer], mem_w_k[layer], mem_w_v[layer],
                       mem_w_q[layer], mem_w_o[layer])
    h, xn, wu2_bf, wd2_bf = _mem_cross(h, xn, kq, vo, ffn2_norm[layer],
                                       (ffn2_w_up[layer], ffn2_w_down[layer]))
    a, = _ffn_up(xn, wg2_bf, wu2_bf)
    out = _ffn_down(a, h, wd2_bf, final_norm, final=True)[0]
    return out.reshape(batch, seq, d)
```

```python
import functools

import jax
import jax.numpy as jnp
import numpy as np
from jax.experimental import pallas as pl
from jax.experimental.pallas import tpu as pltpu

F32 = jnp.float32
BF16 = jnp.bfloat16

EPS = 1e-6
ROPE_THETA = 10000.0
FFN_RES_SCALE = 0.5
HEAD_DIM = 64
BLOCK = 128
DILATED_BRANCHES = ((128, 1), (512, 4), (2048, 16))
SGU_CHUNK = 128
SGU_GROUP_DIM = 128
N_MEM_HEADS = 4
LANES = 128
MXU_COLS = 256
CAST_ROWS = 256
FFN_UP_ROWS = 1024

VMEM_LIMIT = 56 * 1024 * 1024


def _params(n_axes, vmem=VMEM_LIMIT):
    return pltpu.CompilerParams(
        dimension_semantics=("arbitrary",) * n_axes, vmem_limit_bytes=vmem)


def _resident(shape):
    return pl.BlockSpec(shape, lambda *_: (0,) * len(shape), pipeline_mode=pl.Buffered(1))


def _rms(x, g):
    return x * jax.lax.rsqrt(jnp.mean(x * x, axis=-1, keepdims=True) + EPS) * g


def _gelu(x):
    c = np.sqrt(2.0 / np.pi).astype(np.float32)
    return 0.5 * x * (1.0 + jnp.tanh(c * (x + 0.044715 * (x * x * x))))


def _cast_to_bf16(src_ref, dst_ref):
    rows = src_ref.shape[0]
    chunk = min(CAST_ROWS, rows)

    def body(i, carry):
        sl = pl.ds(pl.multiple_of(i * chunk, chunk), chunk)
        dst_ref[sl, :] = src_ref[sl, :].astype(BF16)
        return carry

    jax.lax.fori_loop(0, rows // chunk, body, 0)


def _cast_jobs(weights, steps, index_map):
    specs, shapes = [], []
    for w in weights:
        assert w.shape[0] % steps == 0
        specs.append(pl.BlockSpec((w.shape[0] // steps, w.shape[1]), index_map))
        shapes.append(jax.ShapeDtypeStruct(w.shape, BF16))
    return specs, shapes


def _run_casts(srcs, dsts):
    for src, dst in zip(srcs, dsts):
        dst[...] = src[...].astype(BF16)


def _norm_kernel(x_ref, g_ref, xn_ref):
    xn_ref[...] = _rms(x_ref[...], g_ref[...]).astype(BF16)


def _first_norm(x, g, *, tm=1024):
    seq, d = x.shape
    row = pl.BlockSpec((tm, d), lambda i: (i, 0))
    return pl.pallas_call(
        _norm_kernel,
        grid=(seq // tm,),
        in_specs=[row, pl.BlockSpec((1, d), lambda i: (0, 0))],
        out_specs=row,
        out_shape=jax.ShapeDtypeStruct((seq, d), BF16),
        compiler_params=_params(1),
        name="first_norm",
    )(x, g.reshape(1, d))


def _ffn_up_kernel(*refs, precast):
    if precast:
        xn_ref, wgb, wub, a_ref = refs
    else:
        xn_ref, wg_ref, wu_ref, wd_ref, a_ref, wdb_ref, wgb, wub = refs
        wdb_ref[...] = wd_ref[...].astype(BF16)

        @pl.when(pl.program_id(1) == 0)
        def _():
            _cast_to_bf16(wg_ref, wgb)
            _cast_to_bf16(wu_ref, wub)

    for r0 in range(0, xn_ref.shape[0], FFN_UP_ROWS):
        rows = slice(r0, r0 + FFN_UP_ROWS)
        xn = xn_ref[rows, :]
        gate = jnp.dot(xn, wgb[...], preferred_element_type=F32)
        up = jnp.dot(xn, wub[...], preferred_element_type=F32)
        a_ref[rows, :] = ((gate * (FFN_RES_SCALE / (1.0 + jnp.exp(-gate)))) * up).astype(BF16)


def _ffn_up(xn, wg, wu, wd=None, *, tf=512):
    seq, d = xn.shape
    dff = wg.shape[1]
    precast = wd is None
    tm = 2 * FFN_UP_ROWS
    m_steps = seq // tm
    in_specs = [pl.BlockSpec((tm, d), lambda f, m: (m, 0)),
                pl.BlockSpec((d, tf), lambda f, m: (0, f)),
                pl.BlockSpec((d, tf), lambda f, m: (0, f))]
    out_specs = [pl.BlockSpec((tm, tf), lambda f, m: (m, f))]
    out_shape = [jax.ShapeDtypeStruct((seq, dff), BF16)]
    scratch, args = [], [xn, wg, wu]
    if not precast:
        assert dff % ((dff // tf) * m_steps) == 0
        wd_rows = dff // ((dff // tf) * m_steps)
        wd_spec = pl.BlockSpec((wd_rows, d), lambda f, m: (f * m_steps + m, 0))
        in_specs.append(wd_spec)
        out_specs.append(wd_spec)
        out_shape.append(jax.ShapeDtypeStruct((dff, d), BF16))
        scratch = [pltpu.VMEM((d, tf), BF16), pltpu.VMEM((d, tf), BF16)]
        args.append(wd)
    return pl.pallas_call(
        functools.partial(_ffn_up_kernel, precast=precast),
        grid=(dff // tf, seq // tm),
        in_specs=in_specs,
        out_specs=out_specs,
        out_shape=out_shape,
        scratch_shapes=scratch,
        compiler_params=_params(2),
        name="ffn_up_bf16" if precast else "ffn_up",
    )(*args)


def _ffn_down_kernel(*refs, final, n_cast):
    a_ref, x_ref, wd_ref, g_ref = refs[:4]
    h = x_ref[...] + jnp.dot(a_ref[...], wd_ref[...], preferred_element_type=F32)
    if final:
        refs[4][...] = _rms(h, g_ref[...])
        return
    pos_ref, freq_ref = refs[4:6]
    cast_in = refs[6:6 + n_cast]
    h_ref, xn_ref, cos_ref, sin_ref = refs[6 + n_cast:10 + n_cast]
    h_ref[...] = h
    xn_ref[...] = _rms(h, g_ref[...]).astype(BF16)
    ang = pos_ref[...].astype(F32) * freq_ref[...]
    lane = jax.lax.broadcasted_iota(jnp.int32, ang.shape, 1)
    cos_ref[...] = jnp.cos(ang)
    sin_ref[...] = jnp.sin(ang) * jnp.where(lane % HEAD_DIM < HEAD_DIM // 2, -1.0, 1.0)
    _run_casts(cast_in, refs[10 + n_cast:])


def _ffn_down(a, x, wd_bf, g, positions=None, cast_weights=(), *, tm=256):
    seq, d = x.shape
    dff = a.shape[1]
    steps = seq // tm
    final = positions is None
    assert not (final and cast_weights)
    row = pl.BlockSpec((tm, d), lambda m: (m, 0))
    in_specs = [pl.BlockSpec((tm, dff), lambda m: (m, 0)), row,
                _resident((dff, d)), pl.BlockSpec((1, d), lambda m: (0, 0))]
    args = [a, x, wd_bf, g.reshape(1, d)]
    out_specs = [row]
    out_shape = [jax.ShapeDtypeStruct((seq, d), F32)]
    if not final:
        half = HEAD_DIM // 2
        inv_freq = ROPE_THETA ** (-jnp.arange(0, HEAD_DIM, 2, dtype=F32) / HEAD_DIM)
        table = pl.BlockSpec((tm, LANES), lambda m: (m, 0))
        cast_specs, cast_shapes = _cast_jobs(cast_weights, steps, lambda m: (m, 0))
        in_specs += [pl.BlockSpec((tm, 1), lambda m: (m, 0)),
                     pl.BlockSpec((1, LANES), lambda m: (0, 0))] + cast_specs
        args += [positions.reshape(seq, 1), jnp.tile(inv_freq, LANES // half).reshape(1, LANES),
                 *cast_weights]
        out_specs += [row, table, table] + cast_specs
        out_shape += [jax.ShapeDtypeStruct((seq, d), BF16),
                      jax.ShapeDtypeStruct((seq, LANES), F32),
                      jax.ShapeDtypeStruct((seq, LANES), F32)] + cast_shapes
    return pl.pallas_call(
        functools.partial(_ffn_down_kernel, final=final, n_cast=len(cast_weights)),
        grid=(steps,),
        in_specs=in_specs,
        out_specs=out_specs,
        out_shape=out_shape,
        compiler_params=_params(1),
        name="ffn_down_final" if final else "ffn_down",
    )(*args)


def _proj_kernel(xn_ref, w_ref, cos_ref, sin_ref, lng_ref, lnb_ref, o_ref, t_ref, *, q_scale):
    width = t_ref.shape[1]

    def segment(seg, epilogue):
        for j in range(width // MXU_COLS):
            c0 = seg * width + j * MXU_COLS
            y = jnp.dot(xn_ref[...], w_ref[:, c0:c0 + MXU_COLS], preferred_element_type=F32)
            epilogue(y, c0)

    def rope(scale):
        def epilogue(y, c0):
            cos = cos_ref[...]
            sin = sin_ref[...]
            lane = jax.lax.broadcasted_iota(jnp.int32, cos.shape, 1)
            first = lane % HEAD_DIM < HEAD_DIM // 2
            for j in range(MXU_COLS // LANES):
                blk = y[:, j * LANES:(j + 1) * LANES]
                partner = jnp.where(first,
                                    pltpu.roll(blk, LANES - HEAD_DIM // 2, 1),
                                    pltpu.roll(blk, HEAD_DIM // 2, 1))
                out = blk * cos + partner * sin
                if scale != 1.0:
                    out = out * scale
                o_ref[:, c0 + j * LANES:c0 + (j + 1) * LANES] = out.astype(o_ref.dtype)
        return epilogue

    def store(fn):
        def epilogue(y, c0):
            o_ref[:, c0:c0 + MXU_COLS] = fn(y).astype(o_ref.dtype)
        return epilogue

    sums = []

    def gelu_partial(y, c0):
        t = _gelu(y)
        t_ref[:, c0 - 4 * width:c0 - 4 * width + MXU_COLS] = t
        sums.append(jnp.sum(t, axis=-1, keepdims=True))

    segment(4, gelu_partial)
    mu = sum(sums) * (1.0 / width)
    tc = t_ref[...] - mu
    t = tc * jax.lax.rsqrt(jnp.mean(tc * tc, axis=-1, keepdims=True) + EPS)
    o_ref[:, 4 * width:] = (t * lng_ref[...] + lnb_ref[...]).astype(o_ref.dtype)
    segment(3, store(_gelu))
    segment(0, rope(q_scale))
    segment(1, rope(1.0))
    segment(2, store(lambda y: y))


def _proj(xn, w_in_bf, cos, sin, ln_g, ln_b, *, width, tm=512):
    seq, d = xn.shape
    d_in = w_in_bf.shape[1]
    assert d_in == 5 * width
    return pl.pallas_call(
        functools.partial(_proj_kernel, q_scale=HEAD_DIM ** -0.5 * np.log2(np.e)),
        grid=(seq // tm,),
        in_specs=[
            pl.BlockSpec((tm, d), lambda m: (m, 0)),
            _resident((d, d_in)),
            pl.BlockSpec((tm, LANES), lambda m: (m, 0)),
            pl.BlockSpec((tm, LANES), lambda m: (m, 0)),
            pl.BlockSpec((1, width), lambda m: (0, 0)),
            pl.BlockSpec((1, width), lambda m: (0, 0)),
        ],
        out_specs=pl.BlockSpec((tm, d_in), lambda m: (m, 0)),
        out_shape=jax.ShapeDtypeStruct((seq, d_in), BF16),
        scratch_shapes=[pltpu.VMEM((tm, width), F32)],
        compiler_params=_params(1),
        name="proj",
    )(xn, w_in_bf, cos, sin, ln_g.reshape(1, width), ln_b.reshape(1, width))


def _attn_kernel(*refs, dil, n_cast):
    q_ref, kc_ref, kp_ref, vc_ref, vp_ref = refs[:5]
    cast_in = refs[5:5 + n_cast]
    o_ref = refs[5 + n_cast]
    cast_out = refs[6 + n_cast:6 + 2 * n_cast]
    stage, q4, k4, v4, kb, vb, m4, l4, acc4, m_s, l_s, acc_s, bias = refs[6 + 2 * n_cast:]
    _run_casts(cast_in, cast_out)
    t = pl.program_id(0)
    sup = q_ref.shape[0]
    sub = sup // dil
    n_units = sup // BLOCK

    def split(dst, n_rows):
        part = n_rows // dil
        for b in range(dil):
            dst[b * part:(b + 1) * part, :] = stage[pl.ds(b, part, stride=dil), :]

    stage[:sup] = kp_ref[...].astype(F32)
    stage[sup:] = kc_ref[...].astype(F32)
    split(k4, 2 * sup)
    stage[:sup] = vp_ref[...].astype(F32)
    stage[sup:] = vc_ref[...].astype(F32)
    split(v4, 2 * sup)
    stage[:sup] = q_ref[...].astype(F32)
    split(q4, sup)
    kb[:BLOCK] = kp_ref[sup - BLOCK:, :]
    kb[BLOCK:] = kc_ref[...]
    vb[:BLOCK] = vp_ref[sup - BLOCK:, :]
    vb[BLOCK:] = vc_ref[...]

    qi = jax.lax.broadcasted_iota(jnp.int32, (BLOCK, 2 * BLOCK), 0)
    kj = jax.lax.broadcasted_iota(jnp.int32, (BLOCK, 2 * BLOCK), 1)
    diff = qi + BLOCK - kj
    band = (diff >= 0) & (diff <= BLOCK)
    bias[0] = jnp.where(band & (kj >= BLOCK), 0.0, -jnp.inf)
    bias[1] = jnp.where(band, 0.0, -jnp.inf)
    low = jax.lax.broadcasted_iota(jnp.int32, (BLOCK, LANES), 1) < HEAD_DIM

    def head_pair(q, k, v, has_prev):
        mask = bias[has_prev.astype(jnp.int32)]
        ms, ls, os = [], [], []
        for hh in range(2):
            qm = jnp.where(low if hh == 0 else ~low, q, jnp.zeros_like(q))
            s = jax.lax.dot_general(qm, k, (((1,), (1,)), ((), ())),
                                    preferred_element_type=F32)
            s = s + mask
            m = jnp.max(s, axis=-1, keepdims=True)
            e = jnp.exp2(s - m)
            ms.append(m)
            ls.append(jnp.sum(e, axis=-1, keepdims=True))
            os.append(jnp.dot(e.astype(BF16), v, preferred_element_type=F32))
        return (jnp.where(low, ms[0], ms[1]), jnp.where(low, ls[0], ls[1]),
                jnp.where(low, os[0], os[1]))

    def merge(refs, rows, stats):
        m_ref, l_ref, acc_ref = refs
        m_u, l_u, o_u = stats
        m_old = m_ref[rows, :]
        m_new = jnp.maximum(m_old, m_u)
        a_old = jnp.exp2(m_old - m_new)
        a_u = jnp.exp2(m_u - m_new)
        m_ref[rows, :] = m_new
        l_ref[rows, :] = a_old * l_ref[rows, :] + a_u * l_u
        acc_ref[rows, :] = a_old * acc_ref[rows, :] + a_u * o_u

    def mid_unit(u, carry):
        r = u // (sub // BLOCK)
        n = u % (sub // BLOCK)
        rows = pl.ds(pl.multiple_of(r * sub + n * BLOCK, BLOCK), BLOCK)
        kv_rows = pl.ds(pl.multiple_of(r * 2 * sub + sub + (n - 1) * BLOCK, BLOCK), 2 * BLOCK)
        m_u, l_u, o_u = head_pair(q4[rows, :].astype(BF16), k4[kv_rows, :].astype(BF16),
                                  v4[kv_rows, :].astype(BF16), (t > 0) | (n > 0))
        m4[rows, :] = m_u
        l4[rows, :] = l_u
        acc4[rows, :] = o_u
        return carry

    def wide_unit(r, carry):
        a = r // dil
        b = r % dil
        rows = pl.ds(b * sub + a, BLOCK, stride=dil)
        kv_rows = pl.ds(b * 2 * sub + a, 2 * BLOCK, stride=dil)
        stats = head_pair(q4[rows, :].astype(BF16), k4[kv_rows, :].astype(BF16),
                          v4[kv_rows, :].astype(BF16), t > 0)
        merge((m4, l4, acc4), rows, stats)
        return carry

    def near_unit(n, carry):
        rows = pl.ds(pl.multiple_of(n * BLOCK, BLOCK), BLOCK)
        kv_rows = pl.ds(pl.multiple_of(n * BLOCK, BLOCK), 2 * BLOCK)
        stats = head_pair(q_ref[rows, :], kb[kv_rows, :], vb[kv_rows, :], (t > 0) | (n > 0))
        merge((m_s, l_s, acc_s), rows, stats)
        return carry

    jax.lax.fori_loop(0, n_units, mid_unit, 0, unroll=True)
    jax.lax.fori_loop(0, n_units, wide_unit, 0, unroll=True)
    for b in range(dil):
        src = slice(b * sub, (b + 1) * sub)
        dst = pl.ds(b, sub, stride=dil)
        m_s[dst, :] = m4[src, :]
        l_s[dst, :] = l4[src, :]
        acc_s[dst, :] = acc4[src, :]
    jax.lax.fori_loop(0, n_units, near_unit, 0, unroll=True)

    o_ref[...] = (acc_s[...] / l_s[...]).astype(o_ref.dtype)


def _attention(proj, cast_weights=(), *, width):
    seq, d_in = proj.shape
    dilations = tuple(d for _, d in DILATED_BRANCHES)
    steps = {w // d for w, d in DILATED_BRANCHES}
    assert steps == {BLOCK}, "every branch must span exactly one previous block"
    dil = dilations[1]
    assert dilations == (1, dil, dil * dil)
    sup = BLOCK * dil * dil
    seg = width // LANES
    cur = lambda i: pl.BlockSpec((sup, LANES), lambda t, p: (t, i * seg + p))
    prev = lambda i: pl.BlockSpec((sup, LANES),
                                  lambda t, p: (jnp.maximum(t - 1, 0), i * seg + p))
    f32_rows = lambda n: pltpu.VMEM((n, LANES), F32)
    cast_specs, cast_shapes = _cast_jobs(cast_weights, (seq // sup) * seg,
                                         lambda t, p: (t * seg + p, 0))
    return pl.pallas_call(
        functools.partial(_attn_kernel, dil=dil, n_cast=len(cast_weights)),
        grid=(seq // sup, seg),
        in_specs=[cur(0), cur(1), prev(1), cur(2), prev(2)] + cast_specs,
        out_specs=[pl.BlockSpec((sup, LANES), lambda t, p: (t, p))] + cast_specs,
        out_shape=[jax.ShapeDtypeStruct((seq, width), BF16)] + cast_shapes,
        scratch_shapes=[f32_rows(2 * sup),
                        f32_rows(sup), f32_rows(2 * sup), f32_rows(2 * sup),
                        pltpu.VMEM((sup + BLOCK, LANES), BF16),
                        pltpu.VMEM((sup + BLOCK, LANES), BF16),
                        f32_rows(sup), f32_rows(sup), f32_rows(sup),
                        f32_rows(sup), f32_rows(sup), f32_rows(sup),
                        pltpu.VMEM((2, BLOCK, 2 * BLOCK), F32)],
        compiler_params=_params(2),
        name="dilated_attn",
    )(proj, proj, proj, proj, proj, *cast_weights)


def _mix_kernel(h_ref, attn_ref, u_ref, gv_ref, ws_ref, bs_ref, ag_ref, sg_ref, wo_ref, ng_ref,
                out_ref, xn_ref, mixed_ref):
    tm = h_ref.shape[0]
    width = u_ref.shape[1]
    mixed_ref[:, :width] = _rms(attn_ref[...].astype(F32), ag_ref[...]).astype(BF16)

    ci = jax.lax.broadcasted_iota(jnp.int32, (SGU_CHUNK, SGU_CHUNK), 0)
    cj = jax.lax.broadcasted_iota(jnp.int32, (SGU_CHUNK, SGU_CHUNK), 1)
    causal = cj <= ci
    for g in range(width // SGU_GROUP_DIM):
        gs = slice(g * SGU_GROUP_DIM, (g + 1) * SGU_GROUP_DIM)
        w = jnp.where(causal, ws_ref[g], 0.0).astype(BF16)
        b = bs_ref[:, g:g + 1]
        for c in range(tm // SGU_CHUNK):
            rs = slice(c * SGU_CHUNK, (c + 1) * SGU_CHUNK)
            sv = jnp.dot(w, gv_ref[rs, gs], preferred_element_type=F32) + b
            out_ref[rs, gs] = u_ref[rs, gs].astype(F32) * sv
    sgu = out_ref[:, :width]
    mixed_ref[:, width:] = _rms(sgu, sg_ref[...]).astype(BF16)

    h = h_ref[...] + jnp.dot(mixed_ref[...], wo_ref[...], preferred_element_type=F32)
    out_ref[...] = h
    xn_ref[...] = _rms(h, ng_ref[...]).astype(BF16)


def _mix(h, attn, proj, w_s, b_s, attn_g, sgu_g, w_out, next_g, *, width, tm=512):
    seq, d = h.shape
    n_grp = w_s.shape[0]
    row = lambda m: (m, 0)
    const2 = lambda m: (0, 0)
    return pl.pallas_call(
        _mix_kernel,
        grid=(seq // tm,),
        in_specs=[pl.BlockSpec((tm, d), row),
                  pl.BlockSpec((tm, width), row),
                  pl.BlockSpec((tm, width), lambda m: (m, 3)),
                  pl.BlockSpec((tm, width), lambda m: (m, 4)),
                  pl.BlockSpec((n_grp, SGU_CHUNK, SGU_CHUNK), lambda m: (0, 0, 0)),
                  pl.BlockSpec((SGU_CHUNK, n_grp), const2),
                  pl.BlockSpec((1, width), const2),
                  pl.BlockSpec((1, width), const2),
                  _resident((2 * width, d)),
                  pl.BlockSpec((1, d), const2)],
        out_specs=[pl.BlockSpec((tm, d), row), pl.BlockSpec((tm, d), row)],
        out_shape=[jax.ShapeDtypeStruct((seq, d), F32), jax.ShapeDtypeStruct((seq, d), BF16)],
        scratch_shapes=[pltpu.VMEM((tm, 2 * width), BF16)],
        compiler_params=_params(1),
        name="mix_out",
    )(h, attn, proj, proj, w_s, b_s.T, attn_g.reshape(1, width), sgu_g.reshape(1, width),
      w_out, next_g.reshape(1, d))


def _mem_fold_kernel(mem_ref, g_ref, wk_ref, wv_ref, wq_ref, wo_ref, kq_ref, vo_ref):
    hd = wk_ref.shape[1]
    mk = _rms(mem_ref[...], g_ref[...]).astype(BF16)
    k = jnp.dot(mk, wk_ref[...].astype(BF16), preferred_element_type=F32).astype(BF16)
    v = jnp.dot(mk, wv_ref[...].astype(BF16), preferred_element_type=F32).astype(BF16)
    kq = jax.lax.dot_general(wq_ref[...].astype(BF16), k, (((1,), (1,)), ((), ())),
                             preferred_element_type=F32)
    kq_ref[...] = (kq * (hd ** -0.5)).astype(BF16)
    vo_ref[...] = jnp.dot(v, wo_ref[...].astype(BF16), preferred_element_type=F32).astype(BF16)


def _mem_fold(mem, g, w_k, w_v, w_q, w_o):
    n_mem, d = mem.shape
    hd = d // N_MEM_HEADS
    cols = pl.BlockSpec((d, hd), lambda h: (0, h))
    return pl.pallas_call(
        _mem_fold_kernel,
        grid=(N_MEM_HEADS,),
        in_specs=[pl.BlockSpec((n_mem, d), lambda h: (0, 0)),
                  pl.BlockSpec((1, d), lambda h: (0, 0)),
                  cols, cols, cols,
                  pl.BlockSpec((hd, d), lambda h: (h, 0))],
        out_specs=[pl.BlockSpec((d, n_mem), lambda h: (0, h)),
                   pl.BlockSpec((n_mem, d), lambda h: (h, 0))],
        out_shape=[jax.ShapeDtypeStruct((d, N_MEM_HEADS * n_mem), BF16),
                   jax.ShapeDtypeStruct((N_MEM_HEADS * n_mem, d), BF16)],
        compiler_params=_params(1),
        name="mem_fold",
    )(mem, g.reshape(1, d), w_k, w_v, w_q, w_o)


def _mem_cross_kernel(h_ref, xq_ref, kq_ref, vo_ref, ng_ref, out_ref, xn_ref, p_ref):
    n_mem = kq_ref.shape[1] // N_MEM_HEADS
    for i in range(N_MEM_HEADS):
        sl = slice(i * n_mem, (i + 1) * n_mem)
        s = jnp.dot(xq_ref[...], kq_ref[:, sl], preferred_element_type=F32)
        m = jnp.max(s, axis=-1, keepdims=True)
        e = jnp.exp(s - m)
        p_ref[:, sl] = (e / jnp.sum(e, axis=-1, keepdims=True)).astype(BF16)
    h = h_ref[...] + jnp.dot(p_ref[...], vo_ref[...], preferred_element_type=F32)
    out_ref[...] = h
    xn_ref[...] = _rms(h, ng_ref[...]).astype(BF16)


def _mem_cross(h, xq, kq, vo, next_g, *, tm=512):
    seq, d = h.shape
    row = pl.BlockSpec((tm, d), lambda m: (m, 0))
    return pl.pallas_call(
        _mem_cross_kernel,
        grid=(seq // tm,),
        in_specs=[row, row, _resident(kq.shape), _resident(vo.shape),
                  pl.BlockSpec((1, d), lambda m: (0, 0))],
        out_specs=[row, row],
        out_shape=[jax.ShapeDtypeStruct((seq, d), F32), jax.ShapeDtypeStruct((seq, d), BF16)],
        scratch_shapes=[pltpu.VMEM((tm, kq.shape[1]), BF16)],
        compiler_params=_params(1),
        name="mem_cross",
    )(h, xq, kq, vo, next_g.reshape(1, d))


def kernel(x, mem, positions, ffn1_norm, ffn1_w_gate, ffn1_w_up, ffn1_w_down, mix_norm, w_in,
           sgu_ln_gain, sgu_ln_bias, sgu_w_s, sgu_b_s, attn_out_gain, sgu_out_gain, w_out,
           mem_q_norm, mem_kv_norm, mem_w_q, mem_w_k, mem_w_v, mem_w_o, ffn2_norm,
           ffn2_w_gate, ffn2_w_up, ffn2_w_down, final_norm):
    batch, seq, d = x.shape
    depth = ffn1_norm.shape[0]
    width = attn_out_gain.shape[1]
    assert batch == 1 and mem.shape[0] == 1 and depth == 1
    layer = 0

    h = x.reshape(seq, d)
    mem2 = mem.reshape(mem.shape[1], d)

    xn = _first_norm(h, ffn1_norm[layer])
    a, wd_bf = _ffn_up(xn, ffn1_w_gate[layer], ffn1_w_up[layer], ffn1_w_down[layer])
    h, xn, cos, sin, w_out_bf, w_in_bf = _ffn_down(a, h, wd_bf, mix_norm[layer], positions,
                                                   (w_out[layer], w_in[layer]))
    proj = _proj(xn, w_in_bf, cos, sin, sgu_ln_gain[layer], sgu_ln_bias[layer], width=width)
    attn, wg2_bf, wu2_bf, wd2_bf = _attention(
        proj, (ffn2_w_gate[layer], ffn2_w_up[layer], ffn2_w_down[layer]), width=width)
    h, xn = _mix(h, attn, proj, sgu_w_s[layer], sgu_b_s[layer], attn_out_gain[layer],
                 sgu_out_gain[layer], w_out_bf, mem_q_norm[layer], width=width)
    kq, vo = _mem_fold(mem2, mem_kv_norm[layer], mem_w_k[layer], mem_w_v[layer],
                       mem_w_q[layer], mem_w_o[layer])
    h, xn = _mem_cross(h, xn, kq, vo, ffn2_norm[layer])
    a, = _ffn_up(xn, wg2_bf, wu2_bf)
    out = _ffn_down(a, h, wd2_bf, final_norm)[0]
    return out.reshape(batch, seq, d)
```

```python
import functools

import jax
import jax.numpy as jnp
import numpy as np
from jax.experimental import pallas as pl
from jax.experimental.pallas import tpu as pltpu

F32 = jnp.float32
BF16 = jnp.bfloat16

EPS = 1e-6
ROPE_THETA = 10000.0
FFN_RES_SCALE = 0.5
HEAD_DIM = 64
BLOCK = 128
DILATED_BRANCHES = ((128, 1), (512, 4), (2048, 16))
SGU_CHUNK = 128
SGU_GROUP_DIM = 128
N_MEM_HEADS = 4
LANES = 128
MXU_COLS = 1024
CAST_ROWS = 256
FFN_UP_ROWS = 1024

VMEM_LIMIT = 56 * 1024 * 1024


def _params(n_axes, vmem=VMEM_LIMIT):
    return pltpu.CompilerParams(
        dimension_semantics=("arbitrary",) * n_axes, vmem_limit_bytes=vmem)


def _resident(shape):
    return pl.BlockSpec(shape, lambda *_: (0,) * len(shape), pipeline_mode=pl.Buffered(1))


def _rms(x, g):
    return x * jax.lax.rsqrt(jnp.mean(x * x, axis=-1, keepdims=True) + EPS) * g


def _gelu(x):
    c = np.sqrt(2.0 / np.pi).astype(np.float32)
    return 0.5 * x * (1.0 + jnp.tanh(c * (x + 0.044715 * (x * x * x))))


def _cast_to_bf16(src_ref, dst_ref):
    rows = src_ref.shape[0]
    chunk = min(CAST_ROWS, rows)

    def body(i, carry):
        sl = pl.ds(pl.multiple_of(i * chunk, chunk), chunk)
        dst_ref[sl, :] = src_ref[sl, :].astype(BF16)
        return carry

    jax.lax.fori_loop(0, rows // chunk, body, 0)


def _cast_jobs(weights, steps, index_map):
    specs, shapes = [], []
    for w in weights:
        assert w.shape[0] % steps == 0
        specs.append(pl.BlockSpec((w.shape[0] // steps, w.shape[1]), index_map))
        shapes.append(jax.ShapeDtypeStruct(w.shape, BF16))
    return specs, shapes


def _run_casts(srcs, dsts):
    for src, dst in zip(srcs, dsts):
        dst[...] = src[...].astype(BF16)


def _norm_kernel(x_ref, g_ref, xn_ref):
    xn_ref[...] = _rms(x_ref[...], g_ref[...]).astype(BF16)


def _first_norm(x, g, *, tm=1024):
    seq, d = x.shape
    row = pl.BlockSpec((tm, d), lambda i: (i, 0))
    return pl.pallas_call(
        _norm_kernel,
        grid=(seq // tm,),
        in_specs=[row, pl.BlockSpec((1, d), lambda i: (0, 0))],
        out_specs=row,
        out_shape=jax.ShapeDtypeStruct((seq, d), BF16),
        compiler_params=_params(1),
        name="first_norm",
    )(x, g.reshape(1, d))


def _ffn_up_kernel(*refs, precast):
    if precast:
        xn_ref, wgb, wub, a_ref = refs
    else:
        xn_ref, wg_ref, wu_ref, wd_ref, a_ref, wdb_ref, wgb, wub = refs
        wdb_ref[...] = wd_ref[...].astype(BF16)

        @pl.when(pl.program_id(1) == 0)
        def _():
            _cast_to_bf16(wg_ref, wgb)
            _cast_to_bf16(wu_ref, wub)

    for r0 in range(0, xn_ref.shape[0], FFN_UP_ROWS):
        rows = slice(r0, r0 + FFN_UP_ROWS)
        xn = xn_ref[rows, :]
        gate = jnp.dot(xn, wgb[...], preferred_element_type=F32)
        up = jnp.dot(xn, wub[...], preferred_element_type=F32)
        a_ref[rows, :] = ((gate * (FFN_RES_SCALE / (1.0 + jnp.exp(-gate)))) * up).astype(BF16)


def _ffn_up(xn, wg, wu, wd=None, *, tf=512):
    seq, d = xn.shape
    dff = wg.shape[1]
    precast = wd is None
    tm = 2 * FFN_UP_ROWS
    m_steps = seq // tm
    in_specs = [pl.BlockSpec((tm, d), lambda f, m: (m, 0)),
                pl.BlockSpec((d, tf), lambda f, m: (0, f)),
                pl.BlockSpec((d, tf), lambda f, m: (0, f))]
    out_specs = [pl.BlockSpec((tm, tf), lambda f, m: (m, f))]
    out_shape = [jax.ShapeDtypeStruct((seq, dff), BF16)]
    scratch, args = [], [xn, wg, wu]
    if not precast:
        assert dff % ((dff // tf) * m_steps) == 0
        wd_rows = dff // ((dff // tf) * m_steps)
        wd_spec = pl.BlockSpec((wd_rows, d), lambda f, m: (f * m_steps + m, 0))
        in_specs.append(wd_spec)
        out_specs.append(wd_spec)
        out_shape.append(jax.ShapeDtypeStruct((dff, d), BF16))
        scratch = [pltpu.VMEM((d, tf), BF16), pltpu.VMEM((d, tf), BF16)]
        args.append(wd)
    return pl.pallas_call(
        functools.partial(_ffn_up_kernel, precast=precast),
        grid=(dff // tf, seq // tm),
        in_specs=in_specs,
        out_specs=out_specs,
        out_shape=out_shape,
        scratch_shapes=scratch,
        compiler_params=_params(2),
        name="ffn_up_bf16" if precast else "ffn_up",
    )(*args)


def _ffn_down_kernel(*refs, final, n_cast):
    a_ref, x_ref, wd_ref, g_ref = refs[:4]
    h = x_ref[...] + jnp.dot(a_ref[...], wd_ref[...], preferred_element_type=F32)
    if final:
        refs[4][...] = _rms(h, g_ref[...])
        return
    pos_ref, freq_ref = refs[4:6]
    cast_in = refs[6:6 + n_cast]
    h_ref, xn_ref, cos_ref, sin_ref = refs[6 + n_cast:10 + n_cast]
    h_ref[...] = h
    xn_ref[...] = _rms(h, g_ref[...]).astype(BF16)
    ang = pos_ref[...].astype(F32) * freq_ref[...]
    lane = jax.lax.broadcasted_iota(jnp.int32, ang.shape, 1)
    cos_ref[...] = jnp.cos(ang)
    sin_ref[...] = jnp.sin(ang) * jnp.where(lane % HEAD_DIM < HEAD_DIM // 2, -1.0, 1.0)
    _run_casts(cast_in, refs[10 + n_cast:])


def _ffn_down(a, x, wd_bf, g, positions=None, cast_weights=(), *, tm=256):
    seq, d = x.shape
    dff = a.shape[1]
    steps = seq // tm
    final = positions is None
    assert not (final and cast_weights)
    row = pl.BlockSpec((tm, d), lambda m: (m, 0))
    in_specs = [pl.BlockSpec((tm, dff), lambda m: (m, 0)), row,
                _resident((dff, d)), pl.BlockSpec((1, d), lambda m: (0, 0))]
    args = [a, x, wd_bf, g.reshape(1, d)]
    out_specs = [row]
    out_shape = [jax.ShapeDtypeStruct((seq, d), F32)]
    if not final:
        half = HEAD_DIM // 2
        inv_freq = ROPE_THETA ** (-jnp.arange(0, HEAD_DIM, 2, dtype=F32) / HEAD_DIM)
        table = pl.BlockSpec((tm, LANES), lambda m: (m, 0))
        cast_specs, cast_shapes = _cast_jobs(cast_weights, steps, lambda m: (m, 0))
        in_specs += [pl.BlockSpec((tm, 1), lambda m: (m, 0)),
                     pl.BlockSpec((1, LANES), lambda m: (0, 0))] + cast_specs
        args += [positions.reshape(seq, 1), jnp.tile(inv_freq, LANES // half).reshape(1, LANES),
                 *cast_weights]
        out_specs += [row, table, table] + cast_specs
        out_shape += [jax.ShapeDtypeStruct((seq, d), BF16),
                      jax.ShapeDtypeStruct((seq, LANES), F32),
                      jax.ShapeDtypeStruct((seq, LANES), F32)] + cast_shapes
    return pl.pallas_call(
        functools.partial(_ffn_down_kernel, final=final, n_cast=len(cast_weights)),
        grid=(steps,),
        in_specs=in_specs,
        out_specs=out_specs,
        out_shape=out_shape,
        compiler_params=_params(1),
        name="ffn_down_final" if final else "ffn_down",
    )(*args)


def _proj_kernel(xn_ref, w_ref, cos_ref, sin_ref, lng_ref, lnb_ref, o_ref, t_ref, *, q_scale):
    width = t_ref.shape[1]

    def segment(seg, epilogue):
        for j in range(width // MXU_COLS):
            c0 = seg * width + j * MXU_COLS
            y = jnp.dot(xn_ref[...], w_ref[:, c0:c0 + MXU_COLS], preferred_element_type=F32)
            epilogue(y, c0)

    def rope(scale):
        def epilogue(y, c0):
            cos = cos_ref[...]
            sin = sin_ref[...]
            lane = jax.lax.broadcasted_iota(jnp.int32, cos.shape, 1)
            first = lane % HEAD_DIM < HEAD_DIM // 2
            for j in range(MXU_COLS // LANES):
                blk = y[:, j * LANES:(j + 1) * LANES]
                partner = jnp.where(first,
                                    pltpu.roll(blk, LANES - HEAD_DIM // 2, 1),
                                    pltpu.roll(blk, HEAD_DIM // 2, 1))
                out = blk * cos + partner * sin
                if scale != 1.0:
                    out = out * scale
                o_ref[:, c0 + j * LANES:c0 + (j + 1) * LANES] = out.astype(o_ref.dtype)
        return epilogue

    def store(fn):
        def epilogue(y, c0):
            o_ref[:, c0:c0 + MXU_COLS] = fn(y).astype(o_ref.dtype)
        return epilogue

    sums = []

    def gelu_partial(y, c0):
        t = _gelu(y)
        t_ref[:, c0 - 4 * width:c0 - 4 * width + MXU_COLS] = t
        sums.append(jnp.sum(t, axis=-1, keepdims=True))

    segment(4, gelu_partial)
    mu = sum(sums) * (1.0 / width)
    tc = t_ref[...] - mu
    t = tc * jax.lax.rsqrt(jnp.mean(tc * tc, axis=-1, keepdims=True) + EPS)
    o_ref[:, 4 * width:] = (t * lng_ref[...] + lnb_ref[...]).astype(o_ref.dtype)
    segment(3, store(_gelu))
    segment(0, rope(q_scale))
    segment(1, rope(1.0))
    segment(2, store(lambda y: y))


def _proj(xn, w_in_bf, cos, sin, ln_g, ln_b, *, width, tm=512):
    seq, d = xn.shape
    d_in = w_in_bf.shape[1]
    assert d_in == 5 * width
    return pl.pallas_call(
        functools.partial(_proj_kernel, q_scale=HEAD_DIM ** -0.5 * np.log2(np.e)),
        grid=(seq // tm,),
        in_specs=[
            pl.BlockSpec((tm, d), lambda m: (m, 0)),
            _resident((d, d_in)),
            pl.BlockSpec((tm, LANES), lambda m: (m, 0)),
            pl.BlockSpec((tm, LANES), lambda m: (m, 0)),
            pl.BlockSpec((1, width), lambda m: (0, 0)),
            pl.BlockSpec((1, width), lambda m: (0, 0)),
        ],
        out_specs=pl.BlockSpec((tm, d_in), lambda m: (m, 0)),
        out_shape=jax.ShapeDtypeStruct((seq, d_in), BF16),
        scratch_shapes=[pltpu.VMEM((tm, width), F32)],
        compiler_params=_params(1),
        name="proj",
    )(xn, w_in_bf, cos, sin, ln_g.reshape(1, width), ln_b.reshape(1, width))


def _attn_kernel(*refs, dil, n_cast):
    q_ref, k_ref, v_ref = refs[:3]
    cast_in = refs[3:3 + n_cast]
    o_ref = refs[3 + n_cast]
    cast_out = refs[4 + n_cast:4 + 2 * n_cast]
    stage, q4, k4, v4, kb, vb, m4, l4, acc4, m_s, l_s, acc_s, bias = refs[4 + 2 * n_cast:]
    _run_casts(cast_in, cast_out)
    t = pl.program_id(1)
    sup = q_ref.shape[0]
    sub = sup // dil
    n_units = sup // BLOCK

    @pl.when(t == 0)
    def _():
        for dst in (k4, v4):
            for b in range(dil):
                dst[b * 2 * sub:b * 2 * sub + sub, :] = jnp.zeros((sub, LANES), F32)
        kb[:BLOCK] = jnp.zeros((BLOCK, LANES), BF16)
        vb[:BLOCK] = jnp.zeros((BLOCK, LANES), BF16)

    @pl.when(t > 0)
    def _():
        for dst in (k4, v4):
            for b in range(dil):
                dst[b * 2 * sub:b * 2 * sub + sub, :] = dst[b * 2 * sub + sub:(b + 1) * 2 * sub, :]
        kb[:BLOCK] = kb[sup:, :]
        vb[:BLOCK] = vb[sup:, :]

    def split(dst, rows_per_class, offset):
        for b in range(dil):
            dst[b * rows_per_class + offset:b * rows_per_class + offset + sub, :] = (
                stage[pl.ds(b, sub, stride=dil), :])

    stage[...] = k_ref[...].astype(F32)
    split(k4, 2 * sub, sub)
    stage[...] = v_ref[...].astype(F32)
    split(v4, 2 * sub, sub)
    stage[...] = q_ref[...].astype(F32)
    split(q4, sub, 0)
    kb[BLOCK:] = k_ref[...]
    vb[BLOCK:] = v_ref[...]

    qi = jax.lax.broadcasted_iota(jnp.int32, (BLOCK, 2 * BLOCK), 0)
    kj = jax.lax.broadcasted_iota(jnp.int32, (BLOCK, 2 * BLOCK), 1)
    diff = qi + BLOCK - kj
    band = (diff >= 0) & (diff <= BLOCK)
    bias[0] = jnp.where(band & (kj >= BLOCK), 0.0, -jnp.inf)
    bias[1] = jnp.where(band, 0.0, -jnp.inf)
    low = jax.lax.broadcasted_iota(jnp.int32, (BLOCK, LANES), 1) < HEAD_DIM

    def head_pair(q, k, v, has_prev):
        mask = bias[has_prev.astype(jnp.int32)]
        ms, ls, os = [], [], []
        for hh in range(2):
            qm = jnp.where(low if hh == 0 else ~low, q, jnp.zeros_like(q))
            s = jax.lax.dot_general(qm, k, (((1,), (1,)), ((), ())),
                                    preferred_element_type=F32)
            s = s + mask
            m = jnp.max(s, axis=-1, keepdims=True)
            e = jnp.exp2(s - m)
            ms.append(m)
            ls.append(jnp.sum(e, axis=-1, keepdims=True))
            os.append(jnp.dot(e.astype(BF16), v, preferred_element_type=F32))
        return (jnp.where(low, ms[0], ms[1]), jnp.where(low, ls[0], ls[1]),
                jnp.where(low, os[0], os[1]))

    def merge(refs, rows, stats):
        m_ref, l_ref, acc_ref = refs
        m_u, l_u, o_u = stats
        m_old = m_ref[rows, :]
        m_new = jnp.maximum(m_old, m_u)
        a_old = jnp.exp2(m_old - m_new)
        a_u = jnp.exp2(m_u - m_new)
        m_ref[rows, :] = m_new
        l_ref[rows, :] = a_old * l_ref[rows, :] + a_u * l_u
        acc_ref[rows, :] = a_old * acc_ref[rows, :] + a_u * o_u

    def mid_unit(u, carry):
        r = u // (sub // BLOCK)
        n = u % (sub // BLOCK)
        rows = pl.ds(pl.multiple_of(r * sub + n * BLOCK, BLOCK), BLOCK)
        kv_rows = pl.ds(pl.multiple_of(r * 2 * sub + sub + (n - 1) * BLOCK, BLOCK), 2 * BLOCK)
        m_u, l_u, o_u = head_pair(q4[rows, :].astype(BF16), k4[kv_rows, :].astype(BF16),
                                  v4[kv_rows, :].astype(BF16), (t > 0) | (n > 0))
        m4[rows, :] = m_u
        l4[rows, :] = l_u
        acc4[rows, :] = o_u
        return carry

    def wide_unit(r, carry):
        a = r // dil
        b = r % dil
        rows = pl.ds(b * sub + a, BLOCK, stride=dil)
        kv_rows = pl.ds(b * 2 * sub + a, 2 * BLOCK, stride=dil)
        stats = head_pair(q4[rows, :].astype(BF16), k4[kv_rows, :].astype(BF16),
                          v4[kv_rows, :].astype(BF16), t > 0)
        merge((m4, l4, acc4), rows, stats)
        return carry

    def near_unit(n, carry):
        rows = pl.ds(pl.multiple_of(n * BLOCK, BLOCK), BLOCK)
        kv_rows = pl.ds(pl.multiple_of(n * BLOCK, BLOCK), 2 * BLOCK)
        stats = head_pair(q_ref[rows, :], kb[kv_rows, :], vb[kv_rows, :], (t > 0) | (n > 0))
        merge((m_s, l_s, acc_s), rows, stats)
        return carry

    jax.lax.fori_loop(0, n_units, mid_unit, 0, unroll=True)
    jax.lax.fori_loop(0, n_units, wide_unit, 0, unroll=True)
    for b in range(dil):
        src = slice(b * sub, (b + 1) * sub)
        dst = pl.ds(b, sub, stride=dil)
        m_s[dst, :] = m4[src, :]
        l_s[dst, :] = l4[src, :]
        acc_s[dst, :] = acc4[src, :]
    jax.lax.fori_loop(0, n_units, near_unit, 0, unroll=True)

    o_ref[...] = (acc_s[...] / l_s[...]).astype(o_ref.dtype)


def _attention(proj, cast_weights=(), *, width):
    seq, d_in = proj.shape
    dilations = tuple(d for _, d in DILATED_BRANCHES)
    steps = {w // d for w, d in DILATED_BRANCHES}
    assert steps == {BLOCK}, "every branch must span exactly one previous block"
    dil = dilations[1]
    assert dilations == (1, dil, dil * dil)
    sup = BLOCK * dil * dil
    seg = width // LANES
    n_sup = seq // sup
    col = lambda i: pl.BlockSpec((sup, LANES), lambda p, t: (t, i * seg + p))
    f32_rows = lambda n: pltpu.VMEM((n, LANES), F32)
    cast_specs, cast_shapes = _cast_jobs(cast_weights, n_sup * seg,
                                         lambda p, t: (p * n_sup + t, 0))
    return pl.pallas_call(
        functools.partial(_attn_kernel, dil=dil, n_cast=len(cast_weights)),
        grid=(seg, n_sup),
        in_specs=[col(0), col(1), col(2)] + cast_specs,
        out_specs=[pl.BlockSpec((sup, LANES), lambda p, t: (t, p))] + cast_specs,
        out_shape=[jax.ShapeDtypeStruct((seq, width), BF16)] + cast_shapes,
        scratch_shapes=[f32_rows(sup),
                        f32_rows(sup), f32_rows(2 * sup), f32_rows(2 * sup),
                        pltpu.VMEM((sup + BLOCK, LANES), BF16),
                        pltpu.VMEM((sup + BLOCK, LANES), BF16),
                        f32_rows(sup), f32_rows(sup), f32_rows(sup),
                        f32_rows(sup), f32_rows(sup), f32_rows(sup),
                        pltpu.VMEM((2, BLOCK, 2 * BLOCK), F32)],
        compiler_params=_params(2),
        name="dilated_attn",
    )(proj, proj, proj, *cast_weights)


def _mix_kernel(h_ref, attn_ref, u_ref, gv_ref, ws_ref, bs_ref, ag_ref, sg_ref, wo_ref, ng_ref,
                out_ref, xn_ref, mixed_ref):
    tm = h_ref.shape[0]
    width = u_ref.shape[1]
    mixed_ref[:, :width] = _rms(attn_ref[...].astype(F32), ag_ref[...]).astype(BF16)

    ci = jax.lax.broadcasted_iota(jnp.int32, (SGU_CHUNK, SGU_CHUNK), 0)
    cj = jax.lax.broadcasted_iota(jnp.int32, (SGU_CHUNK, SGU_CHUNK), 1)
    causal = cj <= ci
    for g in range(width // SGU_GROUP_DIM):
        gs = slice(g * SGU_GROUP_DIM, (g + 1) * SGU_GROUP_DIM)
        w = jnp.where(causal, ws_ref[g], 0.0).astype(BF16)
        b = bs_ref[:, g:g + 1]
        for c in range(tm // SGU_CHUNK):
            rs = slice(c * SGU_CHUNK, (c + 1) * SGU_CHUNK)
            sv = jnp.dot(w, gv_ref[rs, gs], preferred_element_type=F32) + b
            out_ref[rs, gs] = u_ref[rs, gs].astype(F32) * sv
    sgu = out_ref[:, :width]
    mixed_ref[:, width:] = _rms(sgu, sg_ref[...]).astype(BF16)

    h = h_ref[...] + jnp.dot(mixed_ref[...], wo_ref[...], preferred_element_type=F32)
    out_ref[...] = h
    xn_ref[...] = _rms(h, ng_ref[...]).astype(BF16)


def _mix(h, attn, proj, w_s, b_s, attn_g, sgu_g, w_out, next_g, *, width, tm=512):
    seq, d = h.shape
    n_grp = w_s.shape[0]
    row = lambda m: (m, 0)
    const2 = lambda m: (0, 0)
    return pl.pallas_call(
        _mix_kernel,
        grid=(seq // tm,),
        in_specs=[pl.BlockSpec((tm, d), row),
                  pl.BlockSpec((tm, width), row),
                  pl.BlockSpec((tm, width), lambda m: (m, 3)),
                  pl.BlockSpec((tm, width), lambda m: (m, 4)),
                  pl.BlockSpec((n_grp, SGU_CHUNK, SGU_CHUNK), lambda m: (0, 0, 0)),
                  pl.BlockSpec((SGU_CHUNK, n_grp), const2),
                  pl.BlockSpec((1, width), const2),
                  pl.BlockSpec((1, width), const2),
                  _resident((2 * width, d)),
                  pl.BlockSpec((1, d), const2)],
        out_specs=[pl.BlockSpec((tm, d), row), pl.BlockSpec((tm, d), row)],
        out_shape=[jax.ShapeDtypeStruct((seq, d), F32), jax.ShapeDtypeStruct((seq, d), BF16)],
        scratch_shapes=[pltpu.VMEM((tm, 2 * width), BF16)],
        compiler_params=_params(1),
        name="mix_out",
    )(h, attn, proj, proj, w_s, b_s.T, attn_g.reshape(1, width), sgu_g.reshape(1, width),
      w_out, next_g.reshape(1, d))


def _mem_fold_kernel(mem_ref, g_ref, wk_ref, wv_ref, wq_ref, wo_ref, kq_ref, vo_ref):
    hd = wk_ref.shape[1]
    mk = _rms(mem_ref[...], g_ref[...]).astype(BF16)
    k = jnp.dot(mk, wk_ref[...].astype(BF16), preferred_element_type=F32).astype(BF16)
    v = jnp.dot(mk, wv_ref[...].astype(BF16), preferred_element_type=F32).astype(BF16)
    kq = jax.lax.dot_general(wq_ref[...].astype(BF16), k, (((1,), (1,)), ((), ())),
                             preferred_element_type=F32)
    kq_ref[...] = (kq * (hd ** -0.5)).astype(BF16)
    vo_ref[...] = jnp.dot(v, wo_ref[...].astype(BF16), preferred_element_type=F32).astype(BF16)


def _mem_fold(mem, g, w_k, w_v, w_q, w_o):
    n_mem, d = mem.shape
    hd = d // N_MEM_HEADS
    cols = pl.BlockSpec((d, hd), lambda h: (0, h))
    return pl.pallas_call(
        _mem_fold_kernel,
        grid=(N_MEM_HEADS,),
        in_specs=[pl.BlockSpec((n_mem, d), lambda h: (0, 0)),
                  pl.BlockSpec((1, d), lambda h: (0, 0)),
                  cols, cols, cols,
                  pl.BlockSpec((hd, d), lambda h: (h, 0))],
        out_specs=[pl.BlockSpec((d, n_mem), lambda h: (0, h)),
                   pl.BlockSpec((n_mem, d), lambda h: (h, 0))],
        out_shape=[jax.ShapeDtypeStruct((d, N_MEM_HEADS * n_mem), BF16),
                   jax.ShapeDtypeStruct((N_MEM_HEADS * n_mem, d), BF16)],
        compiler_params=_params(1),
        name="mem_fold",
    )(mem, g.reshape(1, d), w_k, w_v, w_q, w_o)


def _mem_cross_kernel(h_ref, xq_ref, kq_ref, vo_ref, ng_ref, out_ref, xn_ref, p_ref):
    n_mem = kq_ref.shape[1] // N_MEM_HEADS
    for i in range(N_MEM_HEADS):
        sl = slice(i * n_mem, (i + 1) * n_mem)
        s = jnp.dot(xq_ref[...], kq_ref[:, sl], preferred_element_type=F32)
        m = jnp.max(s, axis=-1, keepdims=True)
        e = jnp.exp(s - m)
        p_ref[:, sl] = (e / jnp.sum(e, axis=-1, keepdims=True)).astype(BF16)
    h = h_ref[...] + jnp.dot(p_ref[...], vo_ref[...], preferred_element_type=F32)
    out_ref[...] = h
    xn_ref[...] = _rms(h, ng_ref[...]).astype(BF16)


def _mem_cross(h, xq, kq, vo, next_g, *, tm=512):
    seq, d = h.shape
    row = pl.BlockSpec((tm, d), lambda m: (m, 0))
    return pl.pallas_call(
        _mem_cross_kernel,
        grid=(seq // tm,),
        in_specs=[row, row, _resident(kq.shape), _resident(vo.shape),
                  pl.BlockSpec((1, d), lambda m: (0, 0))],
        out_specs=[row, row],
        out_shape=[jax.ShapeDtypeStruct((seq, d), F32), jax.ShapeDtypeStruct((seq, d), BF16)],
        scratch_shapes=[pltpu.VMEM((tm, kq.shape[1]), BF16)],
        compiler_params=_params(1),
        name="mem_cross",
    )(h, xq, kq, vo, next_g.reshape(1, d))


def kernel(x, mem, positions, ffn1_norm, ffn1_w_gate, ffn1_w_up, ffn1_w_down, mix_norm, w_in,
           sgu_ln_gain, sgu_ln_bias, sgu_w_s, sgu_b_s, attn_out_gain, sgu_out_gain, w_out,
           mem_q_norm, mem_kv_norm, mem_w_q, mem_w_k, mem_w_v, mem_w_o, ffn2_norm,
           ffn2_w_gate, ffn2_w_up, ffn2_w_down, final_norm):
    batch, seq, d = x.shape
    depth = ffn1_norm.shape[0]
    width = attn_out_gain.shape[1]
    assert batch == 1 and mem.shape[0] == 1 and depth == 1
    layer = 0

    h = x.reshape(seq, d)
    mem2 = mem.reshape(mem.shape[1], d)

    xn = _first_norm(h, ffn1_norm[layer])
    a, wd_bf = _ffn_up(xn, ffn1_w_gate[layer], ffn1_w_up[layer], ffn1_w_down[layer])
    h, xn, cos, sin, w_out_bf, w_in_bf = _ffn_down(a, h, wd_bf, mix_norm[layer], positions,
                                                   (w_out[layer], w_in[layer]))
    proj = _proj(xn, w_in_bf, cos, sin, sgu_ln_gain[layer], sgu_ln_bias[layer], width=width)
    attn, wg2_bf, wu2_bf, wd2_bf = _attention(
        proj, (ffn2_w_gate[layer], ffn2_w_up[layer], ffn2_w_down[layer]), width=width)
    h, xn = _mix(h, attn, proj, sgu_w_s[layer], sgu_b_s[layer], attn_out_gain[layer],
                 sgu_out_gain[layer], w_out_bf, mem_q_norm[layer], width=width)
    kq, vo = _mem_fold(mem2, mem_kv_norm[layer], mem_w_k[layer], mem_w_v[layer],
                       mem_w_q[layer], mem_w_o[layer])
    h, xn = _mem_cross(h, xn, kq, vo, ffn2_norm[layer])
    a, = _ffn_up(xn, wg2_bf, wu2_bf)
    out = _ffn_down(a, h, wd2_bf, final_norm)[0]
    return out.reshape(batch, seq, d)
```

```python
import functools

import jax
import jax.numpy as jnp
import numpy as np
from jax.experimental import pallas as pl
from jax.experimental.pallas import tpu as pltpu

F32 = jnp.float32
BF16 = jnp.bfloat16

EPS = 1e-6
ROPE_THETA = 10000.0
FFN_RES_SCALE = 0.5
HEAD_DIM = 64
BLOCK = 128
DILATED_BRANCHES = ((128, 1), (512, 4), (2048, 16))
SGU_CHUNK = 128
SGU_GROUP_DIM = 128
N_MEM_HEADS = 4
LANES = 128
MXU_COLS = 1024
CAST_ROWS = 256
FFN_UP_ROWS = 1024

VMEM_LIMIT = 56 * 1024 * 1024


def _params(n_axes, vmem=VMEM_LIMIT):
    return pltpu.CompilerParams(
        dimension_semantics=("arbitrary",) * n_axes, vmem_limit_bytes=vmem)


def _resident(shape):
    return pl.BlockSpec(shape, lambda *_: (0,) * len(shape), pipeline_mode=pl.Buffered(1))


def _rms(x, g):
    return x * jax.lax.rsqrt(jnp.mean(x * x, axis=-1, keepdims=True) + EPS) * g


def _gelu(x):
    c = np.sqrt(2.0 / np.pi).astype(np.float32)
    return 0.5 * x * (1.0 + jnp.tanh(c * (x + 0.044715 * (x * x * x))))


def _cast_to_bf16(src_ref, dst_ref):
    rows = src_ref.shape[0]
    chunk = min(CAST_ROWS, rows)

    def body(i, carry):
        sl = pl.ds(pl.multiple_of(i * chunk, chunk), chunk)
        dst_ref[sl, :] = src_ref[sl, :].astype(BF16)
        return carry

    jax.lax.fori_loop(0, rows // chunk, body, 0)


def _cast_jobs(weights, steps, index_map):
    specs, shapes = [], []
    for w in weights:
        assert w.shape[0] % steps == 0
        specs.append(pl.BlockSpec((w.shape[0] // steps, w.shape[1]), index_map))
        shapes.append(jax.ShapeDtypeStruct(w.shape, BF16))
    return specs, shapes


def _run_casts(srcs, dsts):
    for src, dst in zip(srcs, dsts):
        dst[...] = src[...].astype(BF16)


def _prep_kernel(pos_ref, freq_ref, x_ref, g_ref, cos_ref, sin_ref, xn_ref):
    ang = pos_ref[...].astype(F32) * freq_ref[...]
    lane = jax.lax.broadcasted_iota(jnp.int32, ang.shape, 1)
    sign = jnp.where(lane % HEAD_DIM < HEAD_DIM // 2, -1.0, 1.0)
    cos_ref[...] = jnp.cos(ang)
    sin_ref[...] = jnp.sin(ang) * sign
    xn_ref[...] = _rms(x_ref[...], g_ref[...]).astype(BF16)


def _prep(positions, x, g, *, tm=1024):
    seq, d = x.shape
    half = HEAD_DIM // 2
    inv_freq = ROPE_THETA ** (-jnp.arange(0, HEAD_DIM, 2, dtype=F32) / HEAD_DIM)
    freq_lane = jnp.tile(inv_freq, LANES // half).reshape(1, LANES)
    pos = positions.reshape(seq, 1)
    row = lambda w: pl.BlockSpec((tm, w), lambda i: (i, 0))
    const = lambda w: pl.BlockSpec((1, w), lambda i: (0, 0))
    return pl.pallas_call(
        _prep_kernel,
        grid=(seq // tm,),
        in_specs=[row(1), const(LANES), row(d), const(d)],
        out_specs=[row(LANES), row(LANES), row(d)],
        out_shape=[jax.ShapeDtypeStruct((seq, LANES), F32)] * 2
        + [jax.ShapeDtypeStruct((seq, d), BF16)],
        compiler_params=_params(1),
        name="prep",
    )(pos, freq_lane, x, g.reshape(1, d))


def _ffn_up_kernel(*refs, precast):
    if precast:
        xn_ref, wgb, wub, a_ref = refs
    else:
        xn_ref, wg_ref, wu_ref, wd_ref, a_ref, wdb_ref, wgb, wub = refs
        wdb_ref[...] = wd_ref[...].astype(BF16)

        @pl.when(pl.program_id(1) == 0)
        def _():
            _cast_to_bf16(wg_ref, wgb)
            _cast_to_bf16(wu_ref, wub)

    for r0 in range(0, xn_ref.shape[0], FFN_UP_ROWS):
        rows = slice(r0, r0 + FFN_UP_ROWS)
        xn = xn_ref[rows, :]
        gate = jnp.dot(xn, wgb[...], preferred_element_type=F32)
        up = jnp.dot(xn, wub[...], preferred_element_type=F32)
        a_ref[rows, :] = ((gate * (FFN_RES_SCALE / (1.0 + jnp.exp(-gate)))) * up).astype(BF16)


def _ffn_up(xn, wg, wu, wd=None, *, tf=512):
    seq, d = xn.shape
    dff = wg.shape[1]
    precast = wd is None
    tm = 2 * FFN_UP_ROWS
    m_steps = seq // tm
    in_specs = [pl.BlockSpec((tm, d), lambda f, m: (m, 0)),
                pl.BlockSpec((d, tf), lambda f, m: (0, f)),
                pl.BlockSpec((d, tf), lambda f, m: (0, f))]
    out_specs = [pl.BlockSpec((tm, tf), lambda f, m: (m, f))]
    out_shape = [jax.ShapeDtypeStruct((seq, dff), BF16)]
    scratch, args = [], [xn, wg, wu]
    if not precast:
        assert dff % ((dff // tf) * m_steps) == 0
        wd_rows = dff // ((dff // tf) * m_steps)
        wd_spec = pl.BlockSpec((wd_rows, d), lambda f, m: (f * m_steps + m, 0))
        in_specs.append(wd_spec)
        out_specs.append(wd_spec)
        out_shape.append(jax.ShapeDtypeStruct((dff, d), BF16))
        scratch = [pltpu.VMEM((d, tf), BF16), pltpu.VMEM((d, tf), BF16)]
        args.append(wd)
    return pl.pallas_call(
        functools.partial(_ffn_up_kernel, precast=precast),
        grid=(dff // tf, seq // tm),
        in_specs=in_specs,
        out_specs=out_specs,
        out_shape=out_shape,
        scratch_shapes=scratch,
        compiler_params=_params(2),
        name="ffn_up_bf16" if precast else "ffn_up",
    )(*args)


def _ffn_down_kernel(*refs, final, n_cast):
    a_ref, x_ref, wd_ref, g_ref = refs[:4]
    cast_in = refs[4:4 + n_cast]
    out_refs = refs[4 + n_cast:]
    h = x_ref[...] + jnp.dot(a_ref[...], wd_ref[...], preferred_element_type=F32)
    if final:
        out_refs[0][...] = _rms(h, g_ref[...])
    else:
        out_refs[0][...] = h
        out_refs[1][...] = _rms(h, g_ref[...]).astype(BF16)
        _run_casts(cast_in, out_refs[2:])


def _ffn_down(a, x, wd_bf, g, cast_weights=(), *, final, tm=256):
    seq, d = x.shape
    dff = a.shape[1]
    steps = seq // tm
    assert not (final and cast_weights)
    row = pl.BlockSpec((tm, d), lambda m: (m, 0))
    cast_specs, cast_shapes = _cast_jobs(cast_weights, steps, lambda m: (m, 0))
    out_specs = ([row] if final else [row, row]) + cast_specs
    out_shape = [jax.ShapeDtypeStruct((seq, d), F32)]
    if not final:
        out_shape.append(jax.ShapeDtypeStruct((seq, d), BF16))
    return pl.pallas_call(
        functools.partial(_ffn_down_kernel, final=final, n_cast=len(cast_weights)),
        grid=(steps,),
        in_specs=[pl.BlockSpec((tm, dff), lambda m: (m, 0)), row,
                  _resident((dff, d)), pl.BlockSpec((1, d), lambda m: (0, 0))] + cast_specs,
        out_specs=out_specs,
        out_shape=out_shape + cast_shapes,
        compiler_params=_params(1),
        name="ffn_down_final" if final else "ffn_down",
    )(a, x, wd_bf, g.reshape(1, d), *cast_weights)


def _proj_kernel(xn_ref, w_ref, cos_ref, sin_ref, lng_ref, lnb_ref, o_ref, t_ref, *, q_scale):
    width = t_ref.shape[1]

    def segment(seg, epilogue):
        for j in range(width // MXU_COLS):
            c0 = seg * width + j * MXU_COLS
            y = jnp.dot(xn_ref[...], w_ref[:, c0:c0 + MXU_COLS], preferred_element_type=F32)
            epilogue(y, c0)

    def rope(scale):
        def epilogue(y, c0):
            cos = cos_ref[...]
            sin = sin_ref[...]
            lane = jax.lax.broadcasted_iota(jnp.int32, cos.shape, 1)
            first = lane % HEAD_DIM < HEAD_DIM // 2
            for j in range(MXU_COLS // LANES):
                blk = y[:, j * LANES:(j + 1) * LANES]
                partner = jnp.where(first,
                                    pltpu.roll(blk, LANES - HEAD_DIM // 2, 1),
                                    pltpu.roll(blk, HEAD_DIM // 2, 1))
                out = blk * cos + partner * sin
                if scale != 1.0:
                    out = out * scale
                o_ref[:, c0 + j * LANES:c0 + (j + 1) * LANES] = out.astype(o_ref.dtype)
        return epilogue

    def store(fn):
        def epilogue(y, c0):
            o_ref[:, c0:c0 + MXU_COLS] = fn(y).astype(o_ref.dtype)
        return epilogue

    sums = []

    def gelu_partial(y, c0):
        t = _gelu(y)
        t_ref[:, c0 - 4 * width:c0 - 4 * width + MXU_COLS] = t
        sums.append(jnp.sum(t, axis=-1, keepdims=True))

    segment(4, gelu_partial)
    mu = sum(sums) * (1.0 / width)
    tc = t_ref[...] - mu
    t = tc * jax.lax.rsqrt(jnp.mean(tc * tc, axis=-1, keepdims=True) + EPS)
    o_ref[:, 4 * width:] = (t * lng_ref[...] + lnb_ref[...]).astype(o_ref.dtype)
    segment(3, store(_gelu))
    segment(0, rope(q_scale))
    segment(1, rope(1.0))
    segment(2, store(lambda y: y))


def _proj(xn, w_in_bf, cos, sin, ln_g, ln_b, *, width, tm=512):
    seq, d = xn.shape
    d_in = w_in_bf.shape[1]
    assert d_in == 5 * width
    return pl.pallas_call(
        functools.partial(_proj_kernel, q_scale=HEAD_DIM ** -0.5 * np.log2(np.e)),
        grid=(seq // tm,),
        in_specs=[
            pl.BlockSpec((tm, d), lambda m: (m, 0)),
            _resident((d, d_in)),
            pl.BlockSpec((tm, LANES), lambda m: (m, 0)),
            pl.BlockSpec((tm, LANES), lambda m: (m, 0)),
            pl.BlockSpec((1, width), lambda m: (0, 0)),
            pl.BlockSpec((1, width), lambda m: (0, 0)),
        ],
        out_specs=pl.BlockSpec((tm, d_in), lambda m: (m, 0)),
        out_shape=jax.ShapeDtypeStruct((seq, d_in), BF16),
        scratch_shapes=[pltpu.VMEM((tm, width), F32)],
        compiler_params=_params(1),
        name="proj",
    )(xn, w_in_bf, cos, sin, ln_g.reshape(1, width), ln_b.reshape(1, width))


def _attn_kernel(*refs, dil, n_cast):
    q_ref, k_ref, v_ref = refs[:3]
    cast_in = refs[3:3 + n_cast]
    o_ref = refs[3 + n_cast]
    cast_out = refs[4 + n_cast:4 + 2 * n_cast]
    stage, q4, k4, v4, kb, vb, m4, l4, acc4, m_s, l_s, acc_s, bias = refs[4 + 2 * n_cast:]
    _run_casts(cast_in, cast_out)
    t = pl.program_id(1)
    sup = q_ref.shape[0]
    sub = sup // dil
    n_units = sup // BLOCK

    @pl.when(t == 0)
    def _():
        for dst in (k4, v4):
            for b in range(dil):
                dst[b * 2 * sub:b * 2 * sub + sub, :] = jnp.zeros((sub, LANES), F32)
        kb[:BLOCK] = jnp.zeros((BLOCK, LANES), BF16)
        vb[:BLOCK] = jnp.zeros((BLOCK, LANES), BF16)

    @pl.when(t > 0)
    def _():
        for dst in (k4, v4):
            for b in range(dil):
                dst[b * 2 * sub:b * 2 * sub + sub, :] = dst[b * 2 * sub + sub:(b + 1) * 2 * sub, :]
        kb[:BLOCK] = kb[sup:, :]
        vb[:BLOCK] = vb[sup:, :]

    def split(dst, rows_per_class, offset):
        for b in range(dil):
            dst[b * rows_per_class + offset:b * rows_per_class + offset + sub, :] = (
                stage[pl.ds(b, sub, stride=dil), :])

    stage[...] = k_ref[...].astype(F32)
    split(k4, 2 * sub, sub)
    stage[...] = v_ref[...].astype(F32)
    split(v4, 2 * sub, sub)
    stage[...] = q_ref[...].astype(F32)
    split(q4, sub, 0)
    kb[BLOCK:] = k_ref[...]
    vb[BLOCK:] = v_ref[...]

    qi = jax.lax.broadcasted_iota(jnp.int32, (BLOCK, 2 * BLOCK), 0)
    kj = jax.lax.broadcasted_iota(jnp.int32, (BLOCK, 2 * BLOCK), 1)
    diff = qi + BLOCK - kj
    band = (diff >= 0) & (diff <= BLOCK)
    bias[0] = jnp.where(band & (kj >= BLOCK), 0.0, -jnp.inf)
    bias[1] = jnp.where(band, 0.0, -jnp.inf)
    low = jax.lax.broadcasted_iota(jnp.int32, (BLOCK, LANES), 1) < HEAD_DIM

    def head_pair(q, k, v, has_prev):
        mask = bias[has_prev.astype(jnp.int32)]
        ms, ls, os = [], [], []
        for hh in range(2):
            qm = jnp.where(low if hh == 0 else ~low, q, jnp.zeros_like(q))
            s = jax.lax.dot_general(qm, k, (((1,), (1,)), ((), ())),
                                    preferred_element_type=F32)
            s = s + mask
            m = jnp.max(s, axis=-1, keepdims=True)
            e = jnp.exp2(s - m)
            ms.append(m)
            ls.append(jnp.sum(e, axis=-1, keepdims=True))
            os.append(jnp.dot(e.astype(BF16), v, preferred_element_type=F32))
        return (jnp.where(low, ms[0], ms[1]), jnp.where(low, ls[0], ls[1]),
                jnp.where(low, os[0], os[1]))

    def merge(refs, rows, stats):
        m_ref, l_ref, acc_ref = refs
        m_u, l_u, o_u = stats
        m_old = m_ref[rows, :]
        m_new = jnp.maximum(m_old, m_u)
        a_old = jnp.exp2(m_old - m_new)
        a_u = jnp.exp2(m_u - m_new)
        m_ref[rows, :] = m_new
        l_ref[rows, :] = a_old * l_ref[rows, :] + a_u * l_u
        acc_ref[rows, :] = a_old * acc_ref[rows, :] + a_u * o_u

    def mid_unit(u, carry):
        r = u // (sub // BLOCK)
        n = u % (sub // BLOCK)
        rows = pl.ds(pl.multiple_of(r * sub + n * BLOCK, BLOCK), BLOCK)
        kv_rows = pl.ds(pl.multiple_of(r * 2 * sub + sub + (n - 1) * BLOCK, BLOCK), 2 * BLOCK)
        m_u, l_u, o_u = head_pair(q4[rows, :].astype(BF16), k4[kv_rows, :].astype(BF16),
                                  v4[kv_rows, :].astype(BF16), (t > 0) | (n > 0))
        m4[rows, :] = m_u
        l4[rows, :] = l_u
        acc4[rows, :] = o_u
        return carry

    def wide_unit(r, carry):
        a = r // dil
        b = r % dil
        rows = pl.ds(b * sub + a, BLOCK, stride=dil)
        kv_rows = pl.ds(b * 2 * sub + a, 2 * BLOCK, stride=dil)
        stats = head_pair(q4[rows, :].astype(BF16), k4[kv_rows, :].astype(BF16),
                          v4[kv_rows, :].astype(BF16), t > 0)
        merge((m4, l4, acc4), rows, stats)
        return carry

    def near_unit(n, carry):
        rows = pl.ds(pl.multiple_of(n * BLOCK, BLOCK), BLOCK)
        kv_rows = pl.ds(pl.multiple_of(n * BLOCK, BLOCK), 2 * BLOCK)
        stats = head_pair(q_ref[rows, :], kb[kv_rows, :], vb[kv_rows, :], (t > 0) | (n > 0))
        merge((m_s, l_s, acc_s), rows, stats)
        return carry

    jax.lax.fori_loop(0, n_units, mid_unit, 0, unroll=True)
    jax.lax.fori_loop(0, n_units, wide_unit, 0, unroll=True)
    for b in range(dil):
        src = slice(b * sub, (b + 1) * sub)
        dst = pl.ds(b, sub, stride=dil)
        m_s[dst, :] = m4[src, :]
        l_s[dst, :] = l4[src, :]
        acc_s[dst, :] = acc4[src, :]
    jax.lax.fori_loop(0, n_units, near_unit, 0, unroll=True)

    o_ref[...] = (acc_s[...] / l_s[...]).astype(o_ref.dtype)


def _attention(proj, cast_weights=(), *, width):
    seq, d_in = proj.shape
    dilations = tuple(d for _, d in DILATED_BRANCHES)
    steps = {w // d for w, d in DILATED_BRANCHES}
    assert steps == {BLOCK}, "every branch must span exactly one previous block"
    dil = dilations[1]
    assert dilations == (1, dil, dil * dil)
    sup = BLOCK * dil * dil
    seg = width // LANES
    n_sup = seq // sup
    col = lambda i: pl.BlockSpec((sup, LANES), lambda p, t: (t, i * seg + p))
    f32_rows = lambda n: pltpu.VMEM((n, LANES), F32)
    cast_specs, cast_shapes = _cast_jobs(cast_weights, n_sup * seg,
                                         lambda p, t: (p * n_sup + t, 0))
    return pl.pallas_call(
        functools.partial(_attn_kernel, dil=dil, n_cast=len(cast_weights)),
        grid=(seg, n_sup),
        in_specs=[col(0), col(1), col(2)] + cast_specs,
        out_specs=[pl.BlockSpec((sup, LANES), lambda p, t: (t, p))] + cast_specs,
        out_shape=[jax.ShapeDtypeStruct((seq, width), BF16)] + cast_shapes,
        scratch_shapes=[f32_rows(sup),
                        f32_rows(sup), f32_rows(2 * sup), f32_rows(2 * sup),
                        pltpu.VMEM((sup + BLOCK, LANES), BF16),
                        pltpu.VMEM((sup + BLOCK, LANES), BF16),
                        f32_rows(sup), f32_rows(sup), f32_rows(sup),
                        f32_rows(sup), f32_rows(sup), f32_rows(sup),
                        pltpu.VMEM((2, BLOCK, 2 * BLOCK), F32)],
        compiler_params=_params(2),
        name="dilated_attn",
    )(proj, proj, proj, *cast_weights)


def _mix_kernel(h_ref, attn_ref, u_ref, gv_ref, ws_ref, bs_ref, ag_ref, sg_ref, wo_ref, ng_ref,
                out_ref, xn_ref, mixed_ref):
    tm = h_ref.shape[0]
    width = u_ref.shape[1]
    mixed_ref[:, :width] = _rms(attn_ref[...].astype(F32), ag_ref[...]).astype(BF16)

    ci = jax.lax.broadcasted_iota(jnp.int32, (SGU_CHUNK, SGU_CHUNK), 0)
    cj = jax.lax.broadcasted_iota(jnp.int32, (SGU_CHUNK, SGU_CHUNK), 1)
    causal = cj <= ci
    for g in range(width // SGU_GROUP_DIM):
        gs = slice(g * SGU_GROUP_DIM, (g + 1) * SGU_GROUP_DIM)
        w = jnp.where(causal, ws_ref[g], 0.0).astype(BF16)
        b = bs_ref[:, g:g + 1]
        for c in range(tm // SGU_CHUNK):
            rs = slice(c * SGU_CHUNK, (c + 1) * SGU_CHUNK)
            sv = jnp.dot(w, gv_ref[rs, gs], preferred_element_type=F32) + b
            out_ref[rs, gs] = u_ref[rs, gs].astype(F32) * sv
    sgu = out_ref[:, :width]
    mixed_ref[:, width:] = _rms(sgu, sg_ref[...]).astype(BF16)

    h = h_ref[...] + jnp.dot(mixed_ref[...], wo_ref[...], preferred_element_type=F32)
    out_ref[...] = h
    xn_ref[...] = _rms(h, ng_ref[...]).astype(BF16)


def _mix(h, attn, proj, w_s, b_s, attn_g, sgu_g, w_out, next_g, *, width, tm=512):
    seq, d = h.shape
    n_grp = w_s.shape[0]
    row = lambda m: (m, 0)
    const2 = lambda m: (0, 0)
    return pl.pallas_call(
        _mix_kernel,
        grid=(seq // tm,),
        in_specs=[pl.BlockSpec((tm, d), row),
                  pl.BlockSpec((tm, width), row),
                  pl.BlockSpec((tm, width), lambda m: (m, 3)),
                  pl.BlockSpec((tm, width), lambda m: (m, 4)),
                  pl.BlockSpec((n_grp, SGU_CHUNK, SGU_CHUNK), lambda m: (0, 0, 0)),
                  pl.BlockSpec((SGU_CHUNK, n_grp), const2),
                  pl.BlockSpec((1, width), const2),
                  pl.BlockSpec((1, width), const2),
                  _resident((2 * width, d)),
                  pl.BlockSpec((1, d), const2)],
        out_specs=[pl.BlockSpec((tm, d), row), pl.BlockSpec((tm, d), row)],
        out_shape=[jax.ShapeDtypeStruct((seq, d), F32), jax.ShapeDtypeStruct((seq, d), BF16)],
        scratch_shapes=[pltpu.VMEM((tm, 2 * width), BF16)],
        compiler_params=_params(1),
        name="mix_out",
    )(h, attn, proj, proj, w_s, b_s.T, attn_g.reshape(1, width), sgu_g.reshape(1, width),
      w_out, next_g.reshape(1, d))


def _mem_fold_kernel(mem_ref, g_ref, wk_ref, wv_ref, wq_ref, wo_ref, kq_ref, vo_ref):
    hd = wk_ref.shape[1]
    mk = _rms(mem_ref[...], g_ref[...]).astype(BF16)
    k = jnp.dot(mk, wk_ref[...].astype(BF16), preferred_element_type=F32).astype(BF16)
    v = jnp.dot(mk, wv_ref[...].astype(BF16), preferred_element_type=F32).astype(BF16)
    kq = jax.lax.dot_general(wq_ref[...].astype(BF16), k, (((1,), (1,)), ((), ())),
                             preferred_element_type=F32)
    kq_ref[...] = (kq * (hd ** -0.5)).astype(BF16)
    vo_ref[...] = jnp.dot(v, wo_ref[...].astype(BF16), preferred_element_type=F32).astype(BF16)


def _mem_fold(mem, g, w_k, w_v, w_q, w_o):
    n_mem, d = mem.shape
    hd = d // N_MEM_HEADS
    cols = pl.BlockSpec((d, hd), lambda h: (0, h))
    return pl.pallas_call(
        _mem_fold_kernel,
        grid=(N_MEM_HEADS,),
        in_specs=[pl.BlockSpec((n_mem, d), lambda h: (0, 0)),
                  pl.BlockSpec((1, d), lambda h: (0, 0)),
                  cols, cols, cols,
                  pl.BlockSpec((hd, d), lambda h: (h, 0))],
        out_specs=[pl.BlockSpec((d, n_mem), lambda h: (0, h)),
                   pl.BlockSpec((n_mem, d), lambda h: (h, 0))],
        out_shape=[jax.ShapeDtypeStruct((d, N_MEM_HEADS * n_mem), BF16),
                   jax.ShapeDtypeStruct((N_MEM_HEADS * n_mem, d), BF16)],
        compiler_params=_params(1),
        name="mem_fold",
    )(mem, g.reshape(1, d), w_k, w_v, w_q, w_o)


def _mem_cross_kernel(h_ref, xq_ref, kq_ref, vo_ref, ng_ref, out_ref, xn_ref, p_ref):
    n_mem = kq_ref.shape[1] // N_MEM_HEADS
    for i in range(N_MEM_HEADS):
        sl = slice(i * n_mem, (i + 1) * n_mem)
        s = jnp.dot(xq_ref[...], kq_ref[:, sl], preferred_element_type=F32)
        m = jnp.max(s, axis=-1, keepdims=True)
        e = jnp.exp(s - m)
        p_ref[:, sl] = (e / jnp.sum(e, axis=-1, keepdims=True)).astype(BF16)
    h = h_ref[...] + jnp.dot(p_ref[...], vo_ref[...], preferred_element_type=F32)
    out_ref[...] = h
    xn_ref[...] = _rms(h, ng_ref[...]).astype(BF16)


def _mem_cross(h, xq, kq, vo, next_g, *, tm=512):
    seq, d = h.shape
    row = pl.BlockSpec((tm, d), lambda m: (m, 0))
    return pl.pallas_call(
        _mem_cross_kernel,
        grid=(seq // tm,),
        in_specs=[row, row, _resident(kq.shape), _resident(vo.shape),
                  pl.BlockSpec((1, d), lambda m: (0, 0))],
        out_specs=[row, row],
        out_shape=[jax.ShapeDtypeStruct((seq, d), F32), jax.ShapeDtypeStruct((seq, d), BF16)],
        scratch_shapes=[pltpu.VMEM((tm, kq.shape[1]), BF16)],
        compiler_params=_params(1),
        name="mem_cross",
    )(h, xq, kq, vo, next_g.reshape(1, d))


def kernel(x, mem, positions, ffn1_norm, ffn1_w_gate, ffn1_w_up, ffn1_w_down, mix_norm, w_in,
           sgu_ln_gain, sgu_ln_bias, sgu_w_s, sgu_b_s, attn_out_gain, sgu_out_gain, w_out,
           mem_q_norm, mem_kv_norm, mem_w_q, mem_w_k, mem_w_v, mem_w_o, ffn2_norm,
           ffn2_w_gate, ffn2_w_up, ffn2_w_down, final_norm):
    batch, seq, d = x.shape
    depth = ffn1_norm.shape[0]
    width = attn_out_gain.shape[1]
    assert batch == 1 and mem.shape[0] == 1 and depth == 1
    layer = 0

    h = x.reshape(seq, d)
    mem2 = mem.reshape(mem.shape[1], d)

    cos, sin, xn = _prep(positions, h, ffn1_norm[layer])
    a, wd_bf = _ffn_up(xn, ffn1_w_gate[layer], ffn1_w_up[layer], ffn1_w_down[layer])
    h, xn, w_out_bf, w_in_bf = _ffn_down(a, h, wd_bf, mix_norm[layer],
                                         (w_out[layer], w_in[layer]), final=False)
    proj = _proj(xn, w_in_bf, cos, sin, sgu_ln_gain[layer], sgu_ln_bias[layer], width=width)
    attn, wg2_bf, wu2_bf, wd2_bf = _attention(
        proj, (ffn2_w_gate[layer], ffn2_w_up[layer], ffn2_w_down[layer]), width=width)
    h, xn = _mix(h, attn, proj, sgu_w_s[layer], sgu_b_s[layer], attn_out_gain[layer],
                 sgu_out_gain[layer], w_out_bf, mem_q_norm[layer], width=width)
    kq, vo = _mem_fold(mem2, mem_kv_norm[layer], mem_w_k[layer], mem_w_v[layer],
                       mem_w_q[layer], mem_w_o[layer])
    h, xn = _mem_cross(h, xn, kq, vo, ffn2_norm[layer])
    a, = _ffn_up(xn, wg2_bf, wu2_bf)
    out = _ffn_down(a, h, wd2_bf, final_norm, final=True)[0]
    return out.reshape(batch, seq, d)
```

```python
import functools

import jax
import jax.numpy as jnp
import numpy as np
from jax.experimental import pallas as pl
from jax.experimental.pallas import tpu as pltpu

F32 = jnp.float32
BF16 = jnp.bfloat16

EPS = 1e-6
ROPE_THETA = 10000.0
FFN_RES_SCALE = 0.5
HEAD_DIM = 64
BLOCK = 128
DILATED_BRANCHES = ((128, 1), (512, 4), (2048, 16))
SGU_CHUNK = 128
SGU_GROUP_DIM = 128
N_MEM_HEADS = 4
LANES = 128
MXU_COLS = 1024
CAST_ROWS = 256
FFN_UP_ROWS = 1024

VMEM_LIMIT = 56 * 1024 * 1024


def _params(n_axes, vmem=VMEM_LIMIT):
    return pltpu.CompilerParams(
        dimension_semantics=("arbitrary",) * n_axes, vmem_limit_bytes=vmem)


def _resident(shape):
    return pl.BlockSpec(shape, lambda *_: (0,) * len(shape), pipeline_mode=pl.Buffered(1))


def _rms(x, g):
    return x * jax.lax.rsqrt(jnp.mean(x * x, axis=-1, keepdims=True) + EPS) * g


def _gelu(x):
    c = np.sqrt(2.0 / np.pi).astype(np.float32)
    return 0.5 * x * (1.0 + jnp.tanh(c * (x + 0.044715 * (x * x * x))))


def _cast_to_bf16(src_ref, dst_ref):
    rows = src_ref.shape[0]
    chunk = min(CAST_ROWS, rows)

    def body(i, carry):
        sl = pl.ds(pl.multiple_of(i * chunk, chunk), chunk)
        dst_ref[sl, :] = src_ref[sl, :].astype(BF16)
        return carry

    jax.lax.fori_loop(0, rows // chunk, body, 0)


def _cast_jobs(weights, steps, index_map):
    specs, shapes = [], []
    for w in weights:
        assert w.shape[0] % steps == 0
        specs.append(pl.BlockSpec((w.shape[0] // steps, w.shape[1]), index_map))
        shapes.append(jax.ShapeDtypeStruct(w.shape, BF16))
    return specs, shapes


def _run_casts(srcs, dsts):
    for src, dst in zip(srcs, dsts):
        dst[...] = src[...].astype(BF16)


def _prep_kernel(pos_ref, freq_ref, x_ref, g_ref, cos_ref, sin_ref, xn_ref):
    ang = pos_ref[...].astype(F32) * freq_ref[...]
    lane = jax.lax.broadcasted_iota(jnp.int32, ang.shape, 1)
    sign = jnp.where(lane % HEAD_DIM < HEAD_DIM // 2, -1.0, 1.0)
    cos_ref[...] = jnp.cos(ang)
    sin_ref[...] = jnp.sin(ang) * sign
    xn_ref[...] = _rms(x_ref[...], g_ref[...]).astype(BF16)


def _prep(positions, x, g, *, tm=1024):
    seq, d = x.shape
    half = HEAD_DIM // 2
    inv_freq = ROPE_THETA ** (-jnp.arange(0, HEAD_DIM, 2, dtype=F32) / HEAD_DIM)
    freq_lane = jnp.tile(inv_freq, LANES // half).reshape(1, LANES)
    pos = positions.reshape(seq, 1)
    row = lambda w: pl.BlockSpec((tm, w), lambda i: (i, 0))
    const = lambda w: pl.BlockSpec((1, w), lambda i: (0, 0))
    return pl.pallas_call(
        _prep_kernel,
        grid=(seq // tm,),
        in_specs=[row(1), const(LANES), row(d), const(d)],
        out_specs=[row(LANES), row(LANES), row(d)],
        out_shape=[jax.ShapeDtypeStruct((seq, LANES), F32)] * 2
        + [jax.ShapeDtypeStruct((seq, d), BF16)],
        compiler_params=_params(1),
        name="prep",
    )(pos, freq_lane, x, g.reshape(1, d))


def _ffn_up_kernel(*refs, precast):
    if precast:
        xn_ref, wgb, wub, a_ref = refs
    else:
        xn_ref, wg_ref, wu_ref, wd_ref, a_ref, wdb_ref, wgb, wub = refs
        wdb_ref[...] = wd_ref[...].astype(BF16)

        @pl.when(pl.program_id(1) == 0)
        def _():
            _cast_to_bf16(wg_ref, wgb)
            _cast_to_bf16(wu_ref, wub)

    for r0 in range(0, xn_ref.shape[0], FFN_UP_ROWS):
        rows = slice(r0, r0 + FFN_UP_ROWS)
        xn = xn_ref[rows, :]
        gate = jnp.dot(xn, wgb[...], preferred_element_type=F32)
        up = jnp.dot(xn, wub[...], preferred_element_type=F32)
        a_ref[rows, :] = ((gate * (FFN_RES_SCALE / (1.0 + jnp.exp(-gate)))) * up).astype(BF16)


def _ffn_up(xn, wg, wu, wd=None, *, tf=512):
    seq, d = xn.shape
    dff = wg.shape[1]
    precast = wd is None
    tm = 2 * FFN_UP_ROWS
    m_steps = seq // tm
    in_specs = [pl.BlockSpec((tm, d), lambda f, m: (m, 0)),
                pl.BlockSpec((d, tf), lambda f, m: (0, f)),
                pl.BlockSpec((d, tf), lambda f, m: (0, f))]
    out_specs = [pl.BlockSpec((tm, tf), lambda f, m: (m, f))]
    out_shape = [jax.ShapeDtypeStruct((seq, dff), BF16)]
    scratch, args = [], [xn, wg, wu]
    if not precast:
        assert dff % ((dff // tf) * m_steps) == 0
        wd_rows = dff // ((dff // tf) * m_steps)
        wd_spec = pl.BlockSpec((wd_rows, d), lambda f, m: (f * m_steps + m, 0))
        in_specs.append(wd_spec)
        out_specs.append(wd_spec)
        out_shape.append(jax.ShapeDtypeStruct((dff, d), BF16))
        scratch = [pltpu.VMEM((d, tf), BF16), pltpu.VMEM((d, tf), BF16)]
        args.append(wd)
    return pl.pallas_call(
        functools.partial(_ffn_up_kernel, precast=precast),
        grid=(dff // tf, seq // tm),
        in_specs=in_specs,
        out_specs=out_specs,
        out_shape=out_shape,
        scratch_shapes=scratch,
        compiler_params=_params(2),
        name="ffn_up_bf16" if precast else "ffn_up",
    )(*args)


def _ffn_down_kernel(*refs, final, n_cast):
    a_ref, x_ref, wd_ref, g_ref = refs[:4]
    cast_in = refs[4:4 + n_cast]
    out_refs = refs[4 + n_cast:]
    h = x_ref[...] + jnp.dot(a_ref[...], wd_ref[...], preferred_element_type=F32)
    if final:
        out_refs[0][...] = _rms(h, g_ref[...])
    else:
        out_refs[0][...] = h
        out_refs[1][...] = _rms(h, g_ref[...]).astype(BF16)
        _run_casts(cast_in, out_refs[2:])


def _ffn_down(a, x, wd_bf, g, cast_weights=(), *, final, tm=256):
    seq, d = x.shape
    dff = a.shape[1]
    steps = seq // tm
    assert not (final and cast_weights)
    row = pl.BlockSpec((tm, d), lambda m: (m, 0))
    cast_specs, cast_shapes = _cast_jobs(cast_weights, steps, lambda m: (m, 0))
    out_specs = ([row] if final else [row, row]) + cast_specs
    out_shape = [jax.ShapeDtypeStruct((seq, d), F32)]
    if not final:
        out_shape.append(jax.ShapeDtypeStruct((seq, d), BF16))
    return pl.pallas_call(
        functools.partial(_ffn_down_kernel, final=final, n_cast=len(cast_weights)),
        grid=(steps,),
        in_specs=[pl.BlockSpec((tm, dff), lambda m: (m, 0)), row,
                  _resident((dff, d)), pl.BlockSpec((1, d), lambda m: (0, 0))] + cast_specs,
        out_specs=out_specs,
        out_shape=out_shape + cast_shapes,
        compiler_params=_params(1),
        name="ffn_down_final" if final else "ffn_down",
    )(a, x, wd_bf, g.reshape(1, d), *cast_weights)


def _proj_kernel(xn_ref, w_ref, cos_ref, sin_ref, lng_ref, lnb_ref, o_ref, t_ref, *, q_scale):
    width = t_ref.shape[1]

    def segment(seg, epilogue):
        for j in range(width // MXU_COLS):
            c0 = seg * width + j * MXU_COLS
            y = jnp.dot(xn_ref[...], w_ref[:, c0:c0 + MXU_COLS], preferred_element_type=F32)
            epilogue(y, c0)

    def rope(scale):
        def epilogue(y, c0):
            cos = cos_ref[...]
            sin = sin_ref[...]
            lane = jax.lax.broadcasted_iota(jnp.int32, cos.shape, 1)
            first = lane % HEAD_DIM < HEAD_DIM // 2
            for j in range(MXU_COLS // LANES):
                blk = y[:, j * LANES:(j + 1) * LANES]
                partner = jnp.where(first,
                                    pltpu.roll(blk, LANES - HEAD_DIM // 2, 1),
                                    pltpu.roll(blk, HEAD_DIM // 2, 1))
                out = blk * cos + partner * sin
                if scale != 1.0:
                    out = out * scale
                o_ref[:, c0 + j * LANES:c0 + (j + 1) * LANES] = out.astype(o_ref.dtype)
        return epilogue

    def store(fn):
        def epilogue(y, c0):
            o_ref[:, c0:c0 + MXU_COLS] = fn(y).astype(o_ref.dtype)
        return epilogue

    sums = []

    def gelu_partial(y, c0):
        t = _gelu(y)
        t_ref[:, c0 - 4 * width:c0 - 4 * width + MXU_COLS] = t
        sums.append(jnp.sum(t, axis=-1, keepdims=True))

    segment(4, gelu_partial)
    mu = sum(sums) * (1.0 / width)
    tc = t_ref[...] - mu
    t = tc * jax.lax.rsqrt(jnp.mean(tc * tc, axis=-1, keepdims=True) + EPS)
    o_ref[:, 4 * width:] = (t * lng_ref[...] + lnb_ref[...]).astype(o_ref.dtype)
    segment(3, store(_gelu))
    segment(0, rope(q_scale))
    segment(1, rope(1.0))
    segment(2, store(lambda y: y))


def _proj(xn, w_in_bf, cos, sin, ln_g, ln_b, *, width, tm=512):
    seq, d = xn.shape
    d_in = w_in_bf.shape[1]
    assert d_in == 5 * width
    return pl.pallas_call(
        functools.partial(_proj_kernel, q_scale=HEAD_DIM ** -0.5 * np.log2(np.e)),
        grid=(seq // tm,),
        in_specs=[
            pl.BlockSpec((tm, d), lambda m: (m, 0)),
            _resident((d, d_in)),
            pl.BlockSpec((tm, LANES), lambda m: (m, 0)),
            pl.BlockSpec((tm, LANES), lambda m: (m, 0)),
            pl.BlockSpec((1, width), lambda m: (0, 0)),
            pl.BlockSpec((1, width), lambda m: (0, 0)),
        ],
        out_specs=pl.BlockSpec((tm, d_in), lambda m: (m, 0)),
        out_shape=jax.ShapeDtypeStruct((seq, d_in), BF16),
        scratch_shapes=[pltpu.VMEM((tm, width), F32)],
        compiler_params=_params(1),
        name="proj",
    )(xn, w_in_bf, cos, sin, ln_g.reshape(1, width), ln_b.reshape(1, width))


def _attn_kernel(*refs, dil, n_cast):
    q_ref, k_ref, v_ref = refs[:3]
    cast_in = refs[3:3 + n_cast]
    o_ref = refs[3 + n_cast]
    cast_out = refs[4 + n_cast:4 + 2 * n_cast]
    stage, q4, k4, v4, kb, vb, m4, l4, acc4, m_s, l_s, acc_s, bias = refs[4 + 2 * n_cast:]
    _run_casts(cast_in, cast_out)
    t = pl.program_id(1)
    sup = q_ref.shape[0]
    sub = sup // dil
    n_units = sup // BLOCK

    @pl.when(t == 0)
    def _():
        for dst in (k4, v4):
            for b in range(dil):
                dst[b * 2 * sub:b * 2 * sub + sub, :] = jnp.zeros((sub, LANES), F32)
        kb[:BLOCK] = jnp.zeros((BLOCK, LANES), BF16)
        vb[:BLOCK] = jnp.zeros((BLOCK, LANES), BF16)

    @pl.when(t > 0)
    def _():
        for dst in (k4, v4):
            for b in range(dil):
                dst[b * 2 * sub:b * 2 * sub + sub, :] = dst[b * 2 * sub + sub:(b + 1) * 2 * sub, :]
        kb[:BLOCK] = kb[sup:, :]
        vb[:BLOCK] = vb[sup:, :]

    def split(dst, rows_per_class, offset):
        for b in range(dil):
            dst[b * rows_per_class + offset:b * rows_per_class + offset + sub, :] = (
                stage[pl.ds(b, sub, stride=dil), :])

    stage[...] = k_ref[...].astype(F32)
    split(k4, 2 * sub, sub)
    stage[...] = v_ref[...].astype(F32)
    split(v4, 2 * sub, sub)
    stage[...] = q_ref[...].astype(F32)
    split(q4, sub, 0)
    kb[BLOCK:] = k_ref[...]
    vb[BLOCK:] = v_ref[...]

    qi = jax.lax.broadcasted_iota(jnp.int32, (BLOCK, 2 * BLOCK), 0)
    kj = jax.lax.broadcasted_iota(jnp.int32, (BLOCK, 2 * BLOCK), 1)
    diff = qi + BLOCK - kj
    band = (diff >= 0) & (diff <= BLOCK)
    bias[0] = jnp.where(band & (kj >= BLOCK), 0.0, -jnp.inf)
    bias[1] = jnp.where(band, 0.0, -jnp.inf)
    low = jax.lax.broadcasted_iota(jnp.int32, (BLOCK, LANES), 1) < HEAD_DIM

    def head_pair(q, k, v, has_prev):
        mask = bias[has_prev.astype(jnp.int32)]
        ms, ls, os = [], [], []
        for hh in range(2):
            qm = jnp.where(low if hh == 0 else ~low, q, jnp.zeros_like(q))
            s = jax.lax.dot_general(qm, k, (((1,), (1,)), ((), ())),
                                    preferred_element_type=F32)
            s = s + mask
            m = jnp.max(s, axis=-1, keepdims=True)
            e = jnp.exp2(s - m)
            ms.append(m)
            ls.append(jnp.sum(e, axis=-1, keepdims=True))
            os.append(jnp.dot(e.astype(BF16), v, preferred_element_type=F32))
        return (jnp.where(low, ms[0], ms[1]), jnp.where(low, ls[0], ls[1]),
                jnp.where(low, os[0], os[1]))

    def merge(refs, rows, stats):
        m_ref, l_ref, acc_ref = refs
        m_u, l_u, o_u = stats
        m_old = m_ref[rows, :]
        m_new = jnp.maximum(m_old, m_u)
        a_old = jnp.exp2(m_old - m_new)
        a_u = jnp.exp2(m_u - m_new)
        m_ref[rows, :] = m_new
        l_ref[rows, :] = a_old * l_ref[rows, :] + a_u * l_u
        acc_ref[rows, :] = a_old * acc_ref[rows, :] + a_u * o_u

    def mid_unit(u, carry):
        r = u // (sub // BLOCK)
        n = u % (sub // BLOCK)
        rows = pl.ds(pl.multiple_of(r * sub + n * BLOCK, BLOCK), BLOCK)
        kv_rows = pl.ds(pl.multiple_of(r * 2 * sub + sub + (n - 1) * BLOCK, BLOCK), 2 * BLOCK)
        m_u, l_u, o_u = head_pair(q4[rows, :].astype(BF16), k4[kv_rows, :].astype(BF16),
                                  v4[kv_rows, :].astype(BF16), (t > 0) | (n > 0))
        m4[rows, :] = m_u
        l4[rows, :] = l_u
        acc4[rows, :] = o_u
        return carry

    def wide_unit(r, carry):
        a = r // dil
        b = r % dil
        rows = pl.ds(b * sub + a, BLOCK, stride=dil)
        kv_rows = pl.ds(b * 2 * sub + a, 2 * BLOCK, stride=dil)
        stats = head_pair(q4[rows, :].astype(BF16), k4[kv_rows, :].astype(BF16),
                          v4[kv_rows, :].astype(BF16), t > 0)
        merge((m4, l4, acc4), rows, stats)
        return carry

    def near_unit(n, carry):
        rows = pl.ds(pl.multiple_of(n * BLOCK, BLOCK), BLOCK)
        kv_rows = pl.ds(pl.multiple_of(n * BLOCK, BLOCK), 2 * BLOCK)
        stats = head_pair(q_ref[rows, :], kb[kv_rows, :], vb[kv_rows, :], (t > 0) | (n > 0))
        merge((m_s, l_s, acc_s), rows, stats)
        return carry

    jax.lax.fori_loop(0, n_units, mid_unit, 0, unroll=True)
    jax.lax.fori_loop(0, n_units, wide_unit, 0, unroll=True)
    for b in range(dil):
        src = slice(b * sub, (b + 1) * sub)
        dst = pl.ds(b, sub, stride=dil)
        m_s[dst, :] = m4[src, :]
        l_s[dst, :] = l4[src, :]
        acc_s[dst, :] = acc4[src, :]
    jax.lax.fori_loop(0, n_units, near_unit, 0, unroll=True)

    o_ref[...] = (acc_s[...] / l_s[...]).astype(o_ref.dtype)


def _attention(proj, cast_weights=(), *, width):
    seq, d_in = proj.shape
    dilations = tuple(d for _, d in DILATED_BRANCHES)
    steps = {w // d for w, d in DILATED_BRANCHES}
    assert steps == {BLOCK}, "every branch must span exactly one previous block"
    dil = dilations[1]
    assert dilations == (1, dil, dil * dil)
    sup = BLOCK * dil * dil
    seg = width // LANES
    n_sup = seq // sup
    col = lambda i: pl.BlockSpec((sup, LANES), lambda p, t: (t, i * seg + p))
    f32_rows = lambda n: pltpu.VMEM((n, LANES), F32)
    cast_specs, cast_shapes = _cast_jobs(cast_weights, n_sup * seg,
                                         lambda p, t: (p * n_sup + t, 0))
    return pl.pallas_call(
        functools.partial(_attn_kernel, dil=dil, n_cast=len(cast_weights)),
        grid=(seg, n_sup),
        in_specs=[col(0), col(1), col(2)] + cast_specs,
        out_specs=[pl.BlockSpec((sup, LANES), lambda p, t: (t, p))] + cast_specs,
        out_shape=[jax.ShapeDtypeStruct((seq, width), BF16)] + cast_shapes,
        scratch_shapes=[f32_rows(sup),
                        f32_rows(sup), f32_rows(2 * sup), f32_rows(2 * sup),
                        pltpu.VMEM((sup + BLOCK, LANES), BF16),
                        pltpu.VMEM((sup + BLOCK, LANES), BF16),
                        f32_rows(sup), f32_rows(sup), f32_rows(sup),
                        f32_rows(sup), f32_rows(sup), f32_rows(sup),
                        pltpu.VMEM((2, BLOCK, 2 * BLOCK), F32)],
        compiler_params=_params(2),
        name="dilated_attn",
    )(proj, proj, proj, *cast_weights)


def _mix_kernel(h_ref, attn_ref, u_ref, gv_ref, ws_ref, bs_ref, ag_ref, sg_ref, wo_ref, ng_ref,
                out_ref, xn_ref, mixed_ref):
    tm = h_ref.shape[0]
    width = u_ref.shape[1]
    mixed_ref[:, :width] = _rms(attn_ref[...].astype(F32), ag_ref[...]).astype(BF16)

    ci = jax.lax.broadcasted_iota(jnp.int32, (SGU_CHUNK, SGU_CHUNK), 0)
    cj = jax.lax.broadcasted_iota(jnp.int32, (SGU_CHUNK, SGU_CHUNK), 1)
    causal = cj <= ci
    for g in range(width // SGU_GROUP_DIM):
        gs = slice(g * SGU_GROUP_DIM, (g + 1) * SGU_GROUP_DIM)
        w = jnp.where(causal, ws_ref[g], 0.0).astype(BF16)
        b = bs_ref[:, g:g + 1]
        for c in range(tm // SGU_CHUNK):
            rs = slice(c * SGU_CHUNK, (c + 1) * SGU_CHUNK)
            sv = jnp.dot(w, gv_ref[rs, gs], preferred_element_type=F32) + b
            out_ref[rs, gs] = u_ref[rs, gs].astype(F32) * sv
    sgu = out_ref[:, :width]
    mixed_ref[:, width:] = _rms(sgu, sg_ref[...]).astype(BF16)

    h = h_ref[...] + jnp.dot(mixed_ref[...], wo_ref[...], preferred_element_type=F32)
    out_ref[...] = h
    xn_ref[...] = _rms(h, ng_ref[...]).astype(BF16)


def _mix(h, attn, proj, w_s, b_s, attn_g, sgu_g, w_out, next_g, *, width, tm=512):
    seq, d = h.shape
    n_grp = w_s.shape[0]
    row = lambda m: (m, 0)
    const2 = lambda m: (0, 0)
    return pl.pallas_call(
        _mix_kernel,
        grid=(seq // tm,),
        in_specs=[pl.BlockSpec((tm, d), row),
                  pl.BlockSpec((tm, width), row),
                  pl.BlockSpec((tm, width), lambda m: (m, 3)),
                  pl.BlockSpec((tm, width), lambda m: (m, 4)),
                  pl.BlockSpec((n_grp, SGU_CHUNK, SGU_CHUNK), lambda m: (0, 0, 0)),
                  pl.BlockSpec((SGU_CHUNK, n_grp), const2),
                  pl.BlockSpec((1, width), const2),
                  pl.BlockSpec((1, width), const2),
                  _resident((2 * width, d)),
                  pl.BlockSpec((1, d), const2)],
        out_specs=[pl.BlockSpec((tm, d), row), pl.BlockSpec((tm, d), row)],
        out_shape=[jax.ShapeDtypeStruct((seq, d), F32), jax.ShapeDtypeStruct((seq, d), BF16)],
        scratch_shapes=[pltpu.VMEM((tm, 2 * width), BF16)],
        compiler_params=_params(1),
        name="mix_out",
    )(h, attn, proj, proj, w_s, b_s.T, attn_g.reshape(1, width), sgu_g.reshape(1, width),
      w_out, next_g.reshape(1, d))


def _mem_fold_kernel(mem_ref, g_ref, wk_ref, wv_ref, wq_ref, wo_ref, kq_ref, vo_ref):
    hd = wk_ref.shape[1]
    mk = _rms(mem_ref[...], g_ref[...]).astype(BF16)
    k = jnp.dot(mk, wk_ref[...], preferred_element_type=F32).astype(BF16)
    v = jnp.dot(mk, wv_ref[...], preferred_element_type=F32).astype(BF16)
    kq = jax.lax.dot_general(wq_ref[...], k, (((1,), (1,)), ((), ())),
                             preferred_element_type=F32)
    kq_ref[...] = (kq * (hd ** -0.5)).astype(BF16)
    vo_ref[...] = jnp.dot(v, wo_ref[...], preferred_element_type=F32).astype(BF16)


def _mem_fold(mem, g, w_k, w_v, w_q, w_o):
    n_mem, d = mem.shape
    hd = d // N_MEM_HEADS
    cols = pl.BlockSpec((d, hd), lambda h: (0, h))
    return pl.pallas_call(
        _mem_fold_kernel,
        grid=(N_MEM_HEADS,),
        in_specs=[pl.BlockSpec((n_mem, d), lambda h: (0, 0)),
                  pl.BlockSpec((1, d), lambda h: (0, 0)),
                  cols, cols, cols,
                  pl.BlockSpec((hd, d), lambda h: (h, 0))],
        out_specs=[pl.BlockSpec((d, n_mem), lambda h: (0, h)),
                   pl.BlockSpec((n_mem, d), lambda h: (h, 0))],
        out_shape=[jax.ShapeDtypeStruct((d, N_MEM_HEADS * n_mem), BF16),
                   jax.ShapeDtypeStruct((N_MEM_HEADS * n_mem, d), BF16)],
        compiler_params=_params(1),
        name="mem_fold",
    )(mem, g.reshape(1, d), w_k, w_v, w_q, w_o)


def _mem_cross_kernel(h_ref, xq_ref, kq_ref, vo_ref, ng_ref, out_ref, xn_ref, p_ref):
    n_mem = kq_ref.shape[1] // N_MEM_HEADS
    for i in range(N_MEM_HEADS):
        sl = slice(i * n_mem, (i + 1) * n_mem)
        s = jnp.dot(xq_ref[...], kq_ref[:, sl], preferred_element_type=F32)
        m = jnp.max(s, axis=-1, keepdims=True)
        e = jnp.exp(s - m)
        p_ref[:, sl] = (e / jnp.sum(e, axis=-1, keepdims=True)).astype(BF16)
    h = h_ref[...] + jnp.dot(p_ref[...], vo_ref[...], preferred_element_type=F32)
    out_ref[...] = h
    xn_ref[...] = _rms(h, ng_ref[...]).astype(BF16)


def _mem_cross(h, xq, kq, vo, next_g, *, tm=512):
    seq, d = h.shape
    row = pl.BlockSpec((tm, d), lambda m: (m, 0))
    return pl.pallas_call(
        _mem_cross_kernel,
        grid=(seq // tm,),
        in_specs=[row, row, _resident(kq.shape), _resident(vo.shape),
                  pl.BlockSpec((1, d), lambda m: (0, 0))],
        out_specs=[row, row],
        out_shape=[jax.ShapeDtypeStruct((seq, d), F32), jax.ShapeDtypeStruct((seq, d), BF16)],
        scratch_shapes=[pltpu.VMEM((tm, kq.shape[1]), BF16)],
        compiler_params=_params(1),
        name="mem_cross",
    )(h, xq, kq, vo, next_g.reshape(1, d))


def kernel(x, mem, positions, ffn1_norm, ffn1_w_gate, ffn1_w_up, ffn1_w_down, mix_norm, w_in,
           sgu_ln_gain, sgu_ln_bias, sgu_w_s, sgu_b_s, attn_out_gain, sgu_out_gain, w_out,
           mem_q_norm, mem_kv_norm, mem_w_q, mem_w_k, mem_w_v, mem_w_o, ffn2_norm,
           ffn2_w_gate, ffn2_w_up, ffn2_w_down, final_norm):
    batch, seq, d = x.shape
    depth = ffn1_norm.shape[0]
    width = attn_out_gain.shape[1]
    assert batch == 1 and mem.shape[0] == 1 and depth == 1
    layer = 0

    h = x.reshape(seq, d)
    mem2 = mem.reshape(mem.shape[1], d)

    cos, sin, xn = _prep(positions, h, ffn1_norm[layer])
    a, wd_bf = _ffn_up(xn, ffn1_w_gate[layer], ffn1_w_up[layer], ffn1_w_down[layer])
    h, xn, w_out_bf, w_in_bf, wk_bf, wv_bf, wq_bf, wo_bf = _ffn_down(
        a, h, wd_bf, mix_norm[layer],
        (w_out[layer], w_in[layer], mem_w_k[layer], mem_w_v[layer], mem_w_q[layer],
         mem_w_o[layer]), final=False)
    proj = _proj(xn, w_in_bf, cos, sin, sgu_ln_gain[layer], sgu_ln_bias[layer], width=width)
    attn, wg2_bf, wu2_bf, wd2_bf = _attention(
        proj, (ffn2_w_gate[layer], ffn2_w_up[layer], ffn2_w_down[layer]), width=width)
    h, xn = _mix(h, attn, proj, sgu_w_s[layer], sgu_b_s[layer], attn_out_gain[layer],
                 sgu_out_gain[layer], w_out_bf, mem_q_norm[layer], width=width)
    kq, vo = _mem_fold(mem2, mem_kv_norm[layer], wk_bf, wv_bf, wq_bf, wo_bf)
    h, xn = _mem_cross(h, xn, kq, vo, ffn2_norm[layer])
    a, = _ffn_up(xn, wg2_bf, wu2_bf)
    out = _ffn_down(a, h, wd2_bf, final_norm, final=True)[0]
    return out.reshape(batch, seq, d)
```

```python
import functools

import jax
import jax.numpy as jnp
import numpy as np
from jax.experimental import pallas as pl
from jax.experimental.pallas import tpu as pltpu

F32 = jnp.float32
BF16 = jnp.bfloat16

EPS = 1e-6
ROPE_THETA = 10000.0
FFN_RES_SCALE = 0.5
HEAD_DIM = 64
BLOCK = 128
DILATED_BRANCHES = ((128, 1), (512, 4), (2048, 16))
SGU_CHUNK = 128
SGU_GROUP_DIM = 128
N_MEM_HEADS = 4
LANES = 128
PROJ_COLS = 1024
CAST_ROWS = 256
FFN_UP_ROWS = 1024

VMEM_LIMIT = 56 * 1024 * 1024


def _params(n_axes, vmem=VMEM_LIMIT):
    return pltpu.CompilerParams(
        dimension_semantics=("arbitrary",) * n_axes, vmem_limit_bytes=vmem)


def _resident(shape):
    return pl.BlockSpec(shape, lambda *_: (0,) * len(shape), pipeline_mode=pl.Buffered(1))


def _rms(x, g):
    return x * jax.lax.rsqrt(jnp.mean(x * x, axis=-1, keepdims=True) + EPS) * g


def _gelu(x):
    c = np.sqrt(2.0 / np.pi).astype(np.float32)
    return 0.5 * x * (1.0 + jnp.tanh(c * (x + 0.044715 * (x * x * x))))


def _cast_to_bf16(src_ref, dst_ref):
    rows = src_ref.shape[0]
    chunk = min(CAST_ROWS, rows)

    def body(i, carry):
        sl = pl.ds(pl.multiple_of(i * chunk, chunk), chunk)
        dst_ref[sl, :] = src_ref[sl, :].astype(BF16)
        return carry

    jax.lax.fori_loop(0, rows // chunk, body, 0)


def _cast_jobs(weights, steps, index_map):
    specs, shapes = [], []
    for w in weights:
        assert w.shape[0] % steps == 0
        specs.append(pl.BlockSpec((w.shape[0] // steps, w.shape[1]), index_map))
        shapes.append(jax.ShapeDtypeStruct(w.shape, BF16))
    return specs, shapes


def _run_casts(srcs, dsts):
    for src, dst in zip(srcs, dsts):
        dst[...] = src[...].astype(BF16)


def _prep_kernel(pos_ref, freq_ref, x_ref, g_ref, cos_ref, sin_ref, xn_ref):
    ang = pos_ref[...].astype(F32) * freq_ref[...]
    lane = jax.lax.broadcasted_iota(jnp.int32, ang.shape, 1)
    sign = jnp.where(lane % HEAD_DIM < HEAD_DIM // 2, -1.0, 1.0)
    cos_ref[...] = jnp.cos(ang)
    sin_ref[...] = jnp.sin(ang) * sign
    xn_ref[...] = _rms(x_ref[...], g_ref[...]).astype(BF16)


def _prep(positions, x, g, *, tm=1024):
    seq, d = x.shape
    half = HEAD_DIM // 2
    inv_freq = ROPE_THETA ** (-jnp.arange(0, HEAD_DIM, 2, dtype=F32) / HEAD_DIM)
    freq_lane = jnp.tile(inv_freq, LANES // half).reshape(1, LANES)
    pos = positions.reshape(seq, 1)
    row = lambda w: pl.BlockSpec((tm, w), lambda i: (i, 0))
    const = lambda w: pl.BlockSpec((1, w), lambda i: (0, 0))
    return pl.pallas_call(
        _prep_kernel,
        grid=(seq // tm,),
        in_specs=[row(1), const(LANES), row(d), const(d)],
        out_specs=[row(LANES), row(LANES), row(d)],
        out_shape=[jax.ShapeDtypeStruct((seq, LANES), F32)] * 2
        + [jax.ShapeDtypeStruct((seq, d), BF16)],
        compiler_params=_params(1),
        name="prep",
    )(pos, freq_lane, x, g.reshape(1, d))


def _ffn_up_kernel(*refs, precast):
    if precast:
        xn_ref, wgb, wub, a_ref = refs
    else:
        xn_ref, wg_ref, wu_ref, wd_ref, a_ref, wdb_ref, wgb, wub = refs
        wdb_ref[...] = wd_ref[...].astype(BF16)

        @pl.when(pl.program_id(1) == 0)
        def _():
            _cast_to_bf16(wg_ref, wgb)
            _cast_to_bf16(wu_ref, wub)

    for r0 in range(0, xn_ref.shape[0], FFN_UP_ROWS):
        rows = slice(r0, r0 + FFN_UP_ROWS)
        xn = xn_ref[rows, :]
        gate = jnp.dot(xn, wgb[...], preferred_element_type=F32)
        up = jnp.dot(xn, wub[...], preferred_element_type=F32)
        a_ref[rows, :] = ((gate * (FFN_RES_SCALE / (1.0 + jnp.exp(-gate)))) * up).astype(BF16)


def _ffn_up(xn, wg, wu, wd=None, *, tf=512):
    seq, d = xn.shape
    dff = wg.shape[1]
    precast = wd is None
    tm = 2 * FFN_UP_ROWS
    m_steps = seq // tm
    in_specs = [pl.BlockSpec((tm, d), lambda f, m: (m, 0)),
                pl.BlockSpec((d, tf), lambda f, m: (0, f)),
                pl.BlockSpec((d, tf), lambda f, m: (0, f))]
    out_specs = [pl.BlockSpec((tm, tf), lambda f, m: (m, f))]
    out_shape = [jax.ShapeDtypeStruct((seq, dff), BF16)]
    scratch, args = [], [xn, wg, wu]
    if not precast:
        assert dff % ((dff // tf) * m_steps) == 0
        wd_rows = dff // ((dff // tf) * m_steps)
        wd_spec = pl.BlockSpec((wd_rows, d), lambda f, m: (f * m_steps + m, 0))
        in_specs.append(wd_spec)
        out_specs.append(wd_spec)
        out_shape.append(jax.ShapeDtypeStruct((dff, d), BF16))
        scratch = [pltpu.VMEM((d, tf), BF16), pltpu.VMEM((d, tf), BF16)]
        args.append(wd)
    return pl.pallas_call(
        functools.partial(_ffn_up_kernel, precast=precast),
        grid=(dff // tf, seq // tm),
        in_specs=in_specs,
        out_specs=out_specs,
        out_shape=out_shape,
        scratch_shapes=scratch,
        compiler_params=_params(2),
        name="ffn_up_bf16" if precast else "ffn_up",
    )(*args)


def _ffn_down_kernel(*refs, final, n_cast):
    a_ref, x_ref, wd_ref, g_ref = refs[:4]
    cast_in = refs[4:4 + n_cast]
    out_refs = refs[4 + n_cast:]
    h = x_ref[...] + jnp.dot(a_ref[...], wd_ref[...], preferred_element_type=F32)
    if final:
        out_refs[0][...] = _rms(h, g_ref[...])
    else:
        out_refs[0][...] = h
        out_refs[1][...] = _rms(h, g_ref[...]).astype(BF16)
        _run_casts(cast_in, out_refs[2:])


def _ffn_down(a, x, wd_bf, g, cast_weights=(), *, final, tm=256):
    seq, d = x.shape
    dff = a.shape[1]
    steps = seq // tm
    assert not (final and cast_weights)
    row = pl.BlockSpec((tm, d), lambda m: (m, 0))
    cast_specs, cast_shapes = _cast_jobs(cast_weights, steps, lambda m: (m, 0))
    out_specs = ([row] if final else [row, row]) + cast_specs
    out_shape = [jax.ShapeDtypeStruct((seq, d), F32)]
    if not final:
        out_shape.append(jax.ShapeDtypeStruct((seq, d), BF16))
    return pl.pallas_call(
        functools.partial(_ffn_down_kernel, final=final, n_cast=len(cast_weights)),
        grid=(steps,),
        in_specs=[pl.BlockSpec((tm, dff), lambda m: (m, 0)), row,
                  _resident((dff, d)), pl.BlockSpec((1, d), lambda m: (0, 0))] + cast_specs,
        out_specs=out_specs,
        out_shape=out_shape + cast_shapes,
        compiler_params=_params(1),
        name="ffn_down_final" if final else "ffn_down",
    )(a, x, wd_bf, g.reshape(1, d), *cast_weights)


def _proj_kernel(xn_ref, w_ref, cos_ref, sin_ref, lng_ref, lnb_ref, o_ref, t_ref, *, q_scale):
    width = t_ref.shape[1]

    def segment(seg, epilogue):
        for j in range(width // PROJ_COLS):
            c0 = seg * width + j * PROJ_COLS
            y = jnp.dot(xn_ref[...], w_ref[:, c0:c0 + PROJ_COLS], preferred_element_type=F32)
            epilogue(y, c0)

    def rope(scale):
        def epilogue(y, c0):
            cos = cos_ref[...]
            sin = sin_ref[...]
            lane = jax.lax.broadcasted_iota(jnp.int32, cos.shape, 1)
            first = lane % HEAD_DIM < HEAD_DIM // 2
            for j in range(PROJ_COLS // LANES):
                blk = y[:, j * LANES:(j + 1) * LANES]
                partner = jnp.where(first,
                                    pltpu.roll(blk, LANES - HEAD_DIM // 2, 1),
                                    pltpu.roll(blk, HEAD_DIM // 2, 1))
                out = blk * cos + partner * sin
                if scale != 1.0:
                    out = out * scale
                o_ref[:, c0 + j * LANES:c0 + (j + 1) * LANES] = out.astype(o_ref.dtype)
        return epilogue

    def store(fn):
        def epilogue(y, c0):
            o_ref[:, c0:c0 + PROJ_COLS] = fn(y).astype(o_ref.dtype)
        return epilogue

    sums = []

    def gelu_partial(y, c0):
        t = _gelu(y)
        t_ref[:, c0 - 4 * width:c0 - 4 * width + PROJ_COLS] = t
        sums.append(jnp.sum(t, axis=-1, keepdims=True))

    segment(4, gelu_partial)
    mu = sum(sums) * (1.0 / width)
    tc = t_ref[...] - mu
    t = tc * jax.lax.rsqrt(jnp.mean(tc * tc, axis=-1, keepdims=True) + EPS)
    o_ref[:, 4 * width:] = (t * lng_ref[...] + lnb_ref[...]).astype(o_ref.dtype)
    segment(3, store(_gelu))
    segment(0, rope(q_scale))
    segment(1, rope(1.0))
    segment(2, store(lambda y: y))


def _proj(xn, w_in_bf, cos, sin, ln_g, ln_b, *, width, tm=512):
    seq, d = xn.shape
    d_in = w_in_bf.shape[1]
    assert d_in == 5 * width
    return pl.pallas_call(
        functools.partial(_proj_kernel, q_scale=HEAD_DIM ** -0.5 * np.log2(np.e)),
        grid=(seq // tm,),
        in_specs=[
            pl.BlockSpec((tm, d), lambda m: (m, 0)),
            _resident((d, d_in)),
            pl.BlockSpec((tm, LANES), lambda m: (m, 0)),
            pl.BlockSpec((tm, LANES), lambda m: (m, 0)),
            pl.BlockSpec((1, width), lambda m: (0, 0)),
            pl.BlockSpec((1, width), lambda m: (0, 0)),
        ],
        out_specs=pl.BlockSpec((tm, d_in), lambda m: (m, 0)),
        out_shape=jax.ShapeDtypeStruct((seq, d_in), BF16),
        scratch_shapes=[pltpu.VMEM((tm, width), F32)],
        compiler_params=_params(1),
        name="proj",
    )(xn, w_in_bf, cos, sin, ln_g.reshape(1, width), ln_b.reshape(1, width))


def _attn_kernel(*refs, dil, n_cast):
    q_ref, k_ref, v_ref = refs[:3]
    cast_in = refs[3:3 + n_cast]
    o_ref = refs[3 + n_cast]
    cast_out = refs[4 + n_cast:4 + 2 * n_cast]
    stage, q4, k4, v4, kb, vb, m4, l4, acc4, m_s, l_s, acc_s, bias = refs[4 + 2 * n_cast:]
    _run_casts(cast_in, cast_out)
    t = pl.program_id(1)
    sup = q_ref.shape[0]
    sub = sup // dil
    n_units = sup // BLOCK

    @pl.when(t == 0)
    def _():
        for dst in (k4, v4):
            for b in range(dil):
                dst[b * 2 * sub:b * 2 * sub + sub, :] = jnp.zeros((sub, LANES), F32)
        kb[:BLOCK] = jnp.zeros((BLOCK, LANES), BF16)
        vb[:BLOCK] = jnp.zeros((BLOCK, LANES), BF16)

    @pl.when(t > 0)
    def _():
        for dst in (k4, v4):
            for b in range(dil):
                dst[b * 2 * sub:b * 2 * sub + sub, :] = dst[b * 2 * sub + sub:(b + 1) * 2 * sub, :]
        kb[:BLOCK] = kb[sup:, :]
        vb[:BLOCK] = vb[sup:, :]

    def split(dst, rows_per_class, offset):
        for b in range(dil):
            dst[b * rows_per_class + offset:b * rows_per_class + offset + sub, :] = (
                stage[pl.ds(b, sub, stride=dil), :])

    stage[...] = k_ref[...].astype(F32)
    split(k4, 2 * sub, sub)
    stage[...] = v_ref[...].astype(F32)
    split(v4, 2 * sub, sub)
    stage[...] = q_ref[...].astype(F32)
    split(q4, sub, 0)
    kb[BLOCK:] = k_ref[...]
    vb[BLOCK:] = v_ref[...]

    qi = jax.lax.broadcasted_iota(jnp.int32, (BLOCK, 2 * BLOCK), 0)
    kj = jax.lax.broadcasted_iota(jnp.int32, (BLOCK, 2 * BLOCK), 1)
    diff = qi + BLOCK - kj
    band = (diff >= 0) & (diff <= BLOCK)
    bias[0] = jnp.where(band & (kj >= BLOCK), 0.0, -jnp.inf)
    bias[1] = jnp.where(band, 0.0, -jnp.inf)
    low = jax.lax.broadcasted_iota(jnp.int32, (BLOCK, LANES), 1) < HEAD_DIM

    def head_pair(q, k, v, has_prev):
        mask = bias[has_prev.astype(jnp.int32)]
        ms, ls, os = [], [], []
        for hh in range(2):
            qm = jnp.where(low if hh == 0 else ~low, q, jnp.zeros_like(q))
            s = jax.lax.dot_general(qm, k, (((1,), (1,)), ((), ())),
                                    preferred_element_type=F32)
            s = s + mask
            m = jnp.max(s, axis=-1, keepdims=True)
            e = jnp.exp2(s - m)
            ms.append(m)
            ls.append(jnp.sum(e, axis=-1, keepdims=True))
            os.append(jnp.dot(e.astype(BF16), v, preferred_element_type=F32))
        return (jnp.where(low, ms[0], ms[1]), jnp.where(low, ls[0], ls[1]),
                jnp.where(low, os[0], os[1]))

    def merge(refs, rows, stats):
        m_ref, l_ref, acc_ref = refs
        m_u, l_u, o_u = stats
        m_old = m_ref[rows, :]
        m_new = jnp.maximum(m_old, m_u)
        a_old = jnp.exp2(m_old - m_new)
        a_u = jnp.exp2(m_u - m_new)
        m_ref[rows, :] = m_new
        l_ref[rows, :] = a_old * l_ref[rows, :] + a_u * l_u
        acc_ref[rows, :] = a_old * acc_ref[rows, :] + a_u * o_u

    def mid_unit(u, carry):
        r = u // (sub // BLOCK)
        n = u % (sub // BLOCK)
        rows = pl.ds(pl.multiple_of(r * sub + n * BLOCK, BLOCK), BLOCK)
        kv_rows = pl.ds(pl.multiple_of(r * 2 * sub + sub + (n - 1) * BLOCK, BLOCK), 2 * BLOCK)
        m_u, l_u, o_u = head_pair(q4[rows, :].astype(BF16), k4[kv_rows, :].astype(BF16),
                                  v4[kv_rows, :].astype(BF16), (t > 0) | (n > 0))
        m4[rows, :] = m_u
        l4[rows, :] = l_u
        acc4[rows, :] = o_u
        return carry

    def wide_unit(r, carry):
        a = r // dil
        b = r % dil
        rows = pl.ds(b * sub + a, BLOCK, stride=dil)
        kv_rows = pl.ds(b * 2 * sub + a, 2 * BLOCK, stride=dil)
        stats = head_pair(q4[rows, :].astype(BF16), k4[kv_rows, :].astype(BF16),
                          v4[kv_rows, :].astype(BF16), t > 0)
        merge((m4, l4, acc4), rows, stats)
        return carry

    def near_unit(n, carry):
        rows = pl.ds(pl.multiple_of(n * BLOCK, BLOCK), BLOCK)
        kv_rows = pl.ds(pl.multiple_of(n * BLOCK, BLOCK), 2 * BLOCK)
        stats = head_pair(q_ref[rows, :], kb[kv_rows, :], vb[kv_rows, :], (t > 0) | (n > 0))
        merge((m_s, l_s, acc_s), rows, stats)
        return carry

    jax.lax.fori_loop(0, n_units, mid_unit, 0, unroll=True)
    jax.lax.fori_loop(0, n_units, wide_unit, 0, unroll=True)
    for b in range(dil):
        src = slice(b * sub, (b + 1) * sub)
        dst = pl.ds(b, sub, stride=dil)
        m_s[dst, :] = m4[src, :]
        l_s[dst, :] = l4[src, :]
        acc_s[dst, :] = acc4[src, :]
    jax.lax.fori_loop(0, n_units, near_unit, 0, unroll=True)

    o_ref[...] = (acc_s[...] / l_s[...]).astype(o_ref.dtype)


def _attention(proj, cast_weights=(), *, width):
    seq, d_in = proj.shape
    dilations = tuple(d for _, d in DILATED_BRANCHES)
    steps = {w // d for w, d in DILATED_BRANCHES}
    assert steps == {BLOCK}, "every branch must span exactly one previous block"
    dil = dilations[1]
    assert dilations == (1, dil, dil * dil)
    sup = BLOCK * dil * dil
    seg = width // LANES
    n_sup = seq // sup
    col = lambda i: pl.BlockSpec((sup, LANES), lambda p, t: (t, i * seg + p))
    f32_rows = lambda n: pltpu.VMEM((n, LANES), F32)
    cast_specs, cast_shapes = _cast_jobs(cast_weights, n_sup * seg,
                                         lambda p, t: (p * n_sup + t, 0))
    return pl.pallas_call(
        functools.partial(_attn_kernel, dil=dil, n_cast=len(cast_weights)),
        grid=(seg, n_sup),
        in_specs=[col(0), col(1), col(2)] + cast_specs,
        out_specs=[pl.BlockSpec((sup, LANES), lambda p, t: (t, p))] + cast_specs,
        out_shape=[jax.ShapeDtypeStruct((seq, width), BF16)] + cast_shapes,
        scratch_shapes=[f32_rows(sup),
                        f32_rows(sup), f32_rows(2 * sup), f32_rows(2 * sup),
                        pltpu.VMEM((sup + BLOCK, LANES), BF16),
                        pltpu.VMEM((sup + BLOCK, LANES), BF16),
                        f32_rows(sup), f32_rows(sup), f32_rows(sup),
                        f32_rows(sup), f32_rows(sup), f32_rows(sup),
                        pltpu.VMEM((2, BLOCK, 2 * BLOCK), F32)],
        compiler_params=_params(2),
        name="dilated_attn",
    )(proj, proj, proj, *cast_weights)


def _mix_kernel(h_ref, attn_ref, u_ref, gv_ref, ws_ref, bs_ref, ag_ref, sg_ref, wo_ref, ng_ref,
                out_ref, xn_ref, mixed_ref):
    tm = h_ref.shape[0]
    width = u_ref.shape[1]
    mixed_ref[:, :width] = _rms(attn_ref[...].astype(F32), ag_ref[...]).astype(BF16)

    ci = jax.lax.broadcasted_iota(jnp.int32, (SGU_CHUNK, SGU_CHUNK), 0)
    cj = jax.lax.broadcasted_iota(jnp.int32, (SGU_CHUNK, SGU_CHUNK), 1)
    causal = cj <= ci
    for g in range(width // SGU_GROUP_DIM):
        gs = slice(g * SGU_GROUP_DIM, (g + 1) * SGU_GROUP_DIM)
        w = jnp.where(causal, ws_ref[g], 0.0).astype(BF16)
        b = bs_ref[:, g:g + 1]
        for c in range(tm // SGU_CHUNK):
            rs = slice(c * SGU_CHUNK, (c + 1) * SGU_CHUNK)
            sv = jnp.dot(w, gv_ref[rs, gs], preferred_element_type=F32) + b
            out_ref[rs, gs] = u_ref[rs, gs].astype(F32) * sv
    sgu = out_ref[:, :width]
    mixed_ref[:, width:] = _rms(sgu, sg_ref[...]).astype(BF16)

    h = h_ref[...] + jnp.dot(mixed_ref[...], wo_ref[...], preferred_element_type=F32)
    out_ref[...] = h
    xn_ref[...] = _rms(h, ng_ref[...]).astype(BF16)


def _mix(h, attn, proj, w_s, b_s, attn_g, sgu_g, w_out, next_g, *, width, tm=512):
    seq, d = h.shape
    n_grp = w_s.shape[0]
    row = lambda m: (m, 0)
    const2 = lambda m: (0, 0)
    return pl.pallas_call(
        _mix_kernel,
        grid=(seq // tm,),
        in_specs=[pl.BlockSpec((tm, d), row),
                  pl.BlockSpec((tm, width), row),
                  pl.BlockSpec((tm, width), lambda m: (m, 3)),
                  pl.BlockSpec((tm, width), lambda m: (m, 4)),
                  pl.BlockSpec((n_grp, SGU_CHUNK, SGU_CHUNK), lambda m: (0, 0, 0)),
                  pl.BlockSpec((SGU_CHUNK, n_grp), const2),
                  pl.BlockSpec((1, width), const2),
                  pl.BlockSpec((1, width), const2),
                  _resident((2 * width, d)),
                  pl.BlockSpec((1, d), const2)],
        out_specs=[pl.BlockSpec((tm, d), row), pl.BlockSpec((tm, d), row)],
        out_shape=[jax.ShapeDtypeStruct((seq, d), F32), jax.ShapeDtypeStruct((seq, d), BF16)],
        scratch_shapes=[pltpu.VMEM((tm, 2 * width), BF16)],
        compiler_params=_params(1),
        name="mix_out",
    )(h, attn, proj, proj, w_s, b_s.T, attn_g.reshape(1, width), sgu_g.reshape(1, width),
      w_out, next_g.reshape(1, d))


def _mem_fold_kernel(mem_ref, g_ref, wk_ref, wv_ref, wq_ref, wo_ref, kq_ref, vo_ref):
    hd = wk_ref.shape[1]
    mk = _rms(mem_ref[...], g_ref[...]).astype(BF16)
    k = jnp.dot(mk, wk_ref[...], preferred_element_type=F32).astype(BF16)
    v = jnp.dot(mk, wv_ref[...], preferred_element_type=F32).astype(BF16)
    kq = jax.lax.dot_general(wq_ref[...], k, (((1,), (1,)), ((), ())),
                             preferred_element_type=F32)
    kq_ref[...] = (kq * (hd ** -0.5)).astype(BF16)
    vo_ref[...] = jnp.dot(v, wo_ref[...], preferred_element_type=F32).astype(BF16)


def _mem_fold(mem, g, w_k, w_v, w_q, w_o):
    n_mem, d = mem.shape
    hd = d // N_MEM_HEADS
    cols = pl.BlockSpec((d, hd), lambda h: (0, h))
    return pl.pallas_call(
        _mem_fold_kernel,
        grid=(N_MEM_HEADS,),
        in_specs=[pl.BlockSpec((n_mem, d), lambda h: (0, 0)),
                  pl.BlockSpec((1, d), lambda h: (0, 0)),
                  cols, cols, cols,
                  pl.BlockSpec((hd, d), lambda h: (h, 0))],
        out_specs=[pl.BlockSpec((d, n_mem), lambda h: (0, h)),
                   pl.BlockSpec((n_mem, d), lambda h: (h, 0))],
        out_shape=[jax.ShapeDtypeStruct((d, N_MEM_HEADS * n_mem), BF16),
                   jax.ShapeDtypeStruct((N_MEM_HEADS * n_mem, d), BF16)],
        compiler_params=_params(1),
        name="mem_fold",
    )(mem, g.reshape(1, d), w_k, w_v, w_q, w_o)


def _mem_cross_kernel(h_ref, xq_ref, kq_ref, vo_ref, ng_ref, out_ref, xn_ref, p_ref):
    n_mem = kq_ref.shape[1] // N_MEM_HEADS
    for i in range(N_MEM_HEADS):
        sl = slice(i * n_mem, (i + 1) * n_mem)
        s = jnp.dot(xq_ref[...], kq_ref[:, sl], preferred_element_type=F32)
        m = jnp.max(s, axis=-1, keepdims=True)
        e = jnp.exp(s - m)
        p_ref[:, sl] = (e / jnp.sum(e, axis=-1, keepdims=True)).astype(BF16)
    h = h_ref[...] + jnp.dot(p_ref[...], vo_ref[...], preferred_element_type=F32)
    out_ref[...] = h
    xn_ref[...] = _rms(h, ng_ref[...]).astype(BF16)


def _mem_cross(h, xq, kq, vo, next_g, *, tm=512):
    seq, d = h.shape
    row = pl.BlockSpec((tm, d), lambda m: (m, 0))
    return pl.pallas_call(
        _mem_cross_kernel,
        grid=(seq // tm,),
        in_specs=[row, row, _resident(kq.shape), _resident(vo.shape),
                  pl.BlockSpec((1, d), lambda m: (0, 0))],
        out_specs=[row, row],
        out_shape=[jax.ShapeDtypeStruct((seq, d), F32), jax.ShapeDtypeStruct((seq, d), BF16)],
        scratch_shapes=[pltpu.VMEM((tm, kq.shape[1]), BF16)],
        compiler_params=_params(1),
        name="mem_cross",
    )(h, xq, kq, vo, next_g.reshape(1, d))


def kernel(x, mem, positions, ffn1_norm, ffn1_w_gate, ffn1_w_up, ffn1_w_down, mix_norm, w_in,
           sgu_ln_gain, sgu_ln_bias, sgu_w_s, sgu_b_s, attn_out_gain, sgu_out_gain, w_out,
           mem_q_norm, mem_kv_norm, mem_w_q, mem_w_k, mem_w_v, mem_w_o, ffn2_norm,
           ffn2_w_gate, ffn2_w_up, ffn2_w_down, final_norm):
    batch, seq, d = x.shape
    depth = ffn1_norm.shape[0]
    width = attn_out_gain.shape[1]
    assert batch == 1 and mem.shape[0] == 1 and depth == 1
    layer = 0

    h = x.reshape(seq, d)
    mem2 = mem.reshape(mem.shape[1], d)

    cos, sin, xn = _prep(positions, h, ffn1_norm[layer])
    a, wd_bf = _ffn_up(xn, ffn1_w_gate[layer], ffn1_w_up[layer], ffn1_w_down[layer])
    h, xn, w_out_bf, w_in_bf, wk_bf, wv_bf, wq_bf, wo_bf = _ffn_down(
        a, h, wd_bf, mix_norm[layer],
        (w_out[layer], w_in[layer], mem_w_k[layer], mem_w_v[layer], mem_w_q[layer],
         mem_w_o[layer]), final=False)
    proj = _proj(xn, w_in_bf, cos, sin, sgu_ln_gain[layer], sgu_ln_bias[layer], width=width)
    attn, wg2_bf, wu2_bf, wd2_bf = _attention(
        proj, (ffn2_w_gate[layer], ffn2_w_up[layer], ffn2_w_down[layer]), width=width)
    h, xn = _mix(h, attn, proj, sgu_w_s[layer], sgu_b_s[layer], attn_out_gain[layer],
                 sgu_out_gain[layer], w_out_bf, mem_q_norm[layer], width=width)
    kq, vo = _mem_fold(mem2, mem_kv_norm[layer], wk_bf, wv_bf, wq_bf, wo_bf)
    h, xn = _mem_cross(h, xn, kq, vo, ffn2_norm[layer])
    a, = _ffn_up(xn, wg2_bf, wu2_bf)
    out = _ffn_down(a, h, wd2_bf, final_norm, final=True)[0]
    return out.reshape(batch, seq, d)
```

```python
import functools

import jax
import jax.numpy as jnp
import numpy as np
from jax.experimental import pallas as pl
from jax.experimental.pallas import tpu as pltpu

F32 = jnp.float32
BF16 = jnp.bfloat16

EPS = 1e-6
ROPE_THETA = 10000.0
FFN_RES_SCALE = 0.5
HEAD_DIM = 64
BLOCK = 128
DILATED_BRANCHES = ((128, 1), (512, 4), (2048, 16))
SGU_CHUNK = 128
SGU_GROUP_DIM = 128
N_MEM_HEADS = 4
LANES = 128
PROJ_COLS = 1024
CAST_ROWS = 256
FFN_UP_ROWS = 1024

VMEM_LIMIT = 56 * 1024 * 1024


def _params(n_axes, vmem=VMEM_LIMIT):
    return pltpu.CompilerParams(
        dimension_semantics=("arbitrary",) * n_axes, vmem_limit_bytes=vmem)


def _resident(shape):
    return pl.BlockSpec(shape, lambda *_: (0,) * len(shape), pipeline_mode=pl.Buffered(1))


def _rms(x, g):
    return x * jax.lax.rsqrt(jnp.mean(x * x, axis=-1, keepdims=True) + EPS) * g


def _gelu(x):
    c = np.sqrt(2.0 / np.pi).astype(np.float32)
    return 0.5 * x * (1.0 + jnp.tanh(c * (x + 0.044715 * (x * x * x))))


def _cast_to_bf16(src_ref, dst_ref):
    rows = src_ref.shape[0]
    chunk = min(CAST_ROWS, rows)

    def body(i, carry):
        sl = pl.ds(pl.multiple_of(i * chunk, chunk), chunk)
        dst_ref[sl, :] = src_ref[sl, :].astype(BF16)
        return carry

    jax.lax.fori_loop(0, rows // chunk, body, 0)


def _cast_jobs(weights, steps, index_map):
    specs, shapes = [], []
    for w in weights:
        assert w.shape[0] % steps == 0
        specs.append(pl.BlockSpec((w.shape[0] // steps, w.shape[1]), index_map))
        shapes.append(jax.ShapeDtypeStruct(w.shape, BF16))
    return specs, shapes


def _run_casts(srcs, dsts):
    for src, dst in zip(srcs, dsts):
        dst[...] = src[...].astype(BF16)


def _prep_kernel(pos_ref, freq_ref, x_ref, g_ref, cos_ref, sin_ref, xn_ref):
    ang = pos_ref[...].astype(F32) * freq_ref[...]
    lane = jax.lax.broadcasted_iota(jnp.int32, ang.shape, 1)
    sign = jnp.where(lane % HEAD_DIM < HEAD_DIM // 2, -1.0, 1.0)
    cos_ref[...] = jnp.cos(ang)
    sin_ref[...] = jnp.sin(ang) * sign
    xn_ref[...] = _rms(x_ref[...], g_ref[...]).astype(BF16)


def _prep(positions, x, g, *, tm=2048):
    seq, d = x.shape
    half = HEAD_DIM // 2
    inv_freq = ROPE_THETA ** (-jnp.arange(0, HEAD_DIM, 2, dtype=F32) / HEAD_DIM)
    freq_lane = jnp.tile(inv_freq, LANES // half).reshape(1, LANES)
    pos = positions.reshape(seq, 1)
    row = lambda w: pl.BlockSpec((tm, w), lambda i: (i, 0))
    const = lambda w: pl.BlockSpec((1, w), lambda i: (0, 0))
    return pl.pallas_call(
        _prep_kernel,
        grid=(seq // tm,),
        in_specs=[row(1), const(LANES), row(d), const(d)],
        out_specs=[row(LANES), row(LANES), row(d)],
        out_shape=[jax.ShapeDtypeStruct((seq, LANES), F32)] * 2
        + [jax.ShapeDtypeStruct((seq, d), BF16)],
        compiler_params=_params(1),
        name="prep",
    )(pos, freq_lane, x, g.reshape(1, d))


def _ffn_up_kernel(*refs, precast):
    if precast:
        xn_ref, wgb, wub, a_ref = refs
    else:
        xn_ref, wg_ref, wu_ref, wd_ref, a_ref, wdb_ref, wgb, wub = refs
        wdb_ref[...] = wd_ref[...].astype(BF16)

        @pl.when(pl.program_id(1) == 0)
        def _():
            _cast_to_bf16(wg_ref, wgb)
            _cast_to_bf16(wu_ref, wub)

    for r0 in range(0, xn_ref.shape[0], FFN_UP_ROWS):
        rows = slice(r0, r0 + FFN_UP_ROWS)
        xn = xn_ref[rows, :]
        gate = jnp.dot(xn, wgb[...], preferred_element_type=F32)
        up = jnp.dot(xn, wub[...], preferred_element_type=F32)
        a_ref[rows, :] = ((gate * (FFN_RES_SCALE / (1.0 + jnp.exp(-gate)))) * up).astype(BF16)


def _ffn_up(xn, wg, wu, wd=None, *, tf=512):
    seq, d = xn.shape
    dff = wg.shape[1]
    precast = wd is None
    tm = 2 * FFN_UP_ROWS
    m_steps = seq // tm
    in_specs = [pl.BlockSpec((tm, d), lambda f, m: (m, 0)),
                pl.BlockSpec((d, tf), lambda f, m: (0, f)),
                pl.BlockSpec((d, tf), lambda f, m: (0, f))]
    out_specs = [pl.BlockSpec((tm, tf), lambda f, m: (m, f))]
    out_shape = [jax.ShapeDtypeStruct((seq, dff), BF16)]
    scratch, args = [], [xn, wg, wu]
    if not precast:
        assert dff % ((dff // tf) * m_steps) == 0
        wd_rows = dff // ((dff // tf) * m_steps)
        wd_spec = pl.BlockSpec((wd_rows, d), lambda f, m: (f * m_steps + m, 0))
        in_specs.append(wd_spec)
        out_specs.append(wd_spec)
        out_shape.append(jax.ShapeDtypeStruct((dff, d), BF16))
        scratch = [pltpu.VMEM((d, tf), BF16), pltpu.VMEM((d, tf), BF16)]
        args.append(wd)
    return pl.pallas_call(
        functools.partial(_ffn_up_kernel, precast=precast),
        grid=(dff // tf, seq // tm),
        in_specs=in_specs,
        out_specs=out_specs,
        out_shape=out_shape,
        scratch_shapes=scratch,
        compiler_params=_params(2),
        name="ffn_up_bf16" if precast else "ffn_up",
    )(*args)


def _ffn_down_kernel(*refs, final, n_cast):
    a_ref, x_ref, wd_ref, g_ref = refs[:4]
    cast_in = refs[4:4 + n_cast]
    out_refs = refs[4 + n_cast:]
    h = x_ref[...] + jnp.dot(a_ref[...], wd_ref[...], preferred_element_type=F32)
    if final:
        out_refs[0][...] = _rms(h, g_ref[...])
    else:
        out_refs[0][...] = h
        out_refs[1][...] = _rms(h, g_ref[...]).astype(BF16)
        _run_casts(cast_in, out_refs[2:])


def _ffn_down(a, x, wd_bf, g, cast_weights=(), *, final, tm=256):
    seq, d = x.shape
    dff = a.shape[1]
    steps = seq // tm
    assert not (final and cast_weights)
    row = pl.BlockSpec((tm, d), lambda m: (m, 0))
    cast_specs, cast_shapes = _cast_jobs(cast_weights, steps, lambda m: (m, 0))
    out_specs = ([row] if final else [row, row]) + cast_specs
    out_shape = [jax.ShapeDtypeStruct((seq, d), F32)]
    if not final:
        out_shape.append(jax.ShapeDtypeStruct((seq, d), BF16))
    return pl.pallas_call(
        functools.partial(_ffn_down_kernel, final=final, n_cast=len(cast_weights)),
        grid=(steps,),
        in_specs=[pl.BlockSpec((tm, dff), lambda m: (m, 0)), row,
                  _resident((dff, d)), pl.BlockSpec((1, d), lambda m: (0, 0))] + cast_specs,
        out_specs=out_specs,
        out_shape=out_shape + cast_shapes,
        compiler_params=_params(1),
        name="ffn_down_final" if final else "ffn_down",
    )(a, x, wd_bf, g.reshape(1, d), *cast_weights)


def _proj_kernel(xn_ref, w_ref, cos_ref, sin_ref, lng_ref, lnb_ref, o_ref, t_ref, *, q_scale):
    width = t_ref.shape[1]

    def segment(seg, epilogue):
        for j in range(width // PROJ_COLS):
            c0 = seg * width + j * PROJ_COLS
            y = jnp.dot(xn_ref[...], w_ref[:, c0:c0 + PROJ_COLS], preferred_element_type=F32)
            epilogue(y, c0)

    def rope(scale):
        def epilogue(y, c0):
            cos = cos_ref[...]
            sin = sin_ref[...]
            lane = jax.lax.broadcasted_iota(jnp.int32, cos.shape, 1)
            first = lane % HEAD_DIM < HEAD_DIM // 2
            for j in range(PROJ_COLS // LANES):
                blk = y[:, j * LANES:(j + 1) * LANES]
                partner = jnp.where(first,
                                    pltpu.roll(blk, LANES - HEAD_DIM // 2, 1),
                                    pltpu.roll(blk, HEAD_DIM // 2, 1))
                out = blk * cos + partner * sin
                if scale != 1.0:
                    out = out * scale
                o_ref[:, c0 + j * LANES:c0 + (j + 1) * LANES] = out.astype(o_ref.dtype)
        return epilogue

    def store(fn):
        def epilogue(y, c0):
            o_ref[:, c0:c0 + PROJ_COLS] = fn(y).astype(o_ref.dtype)
        return epilogue

    sums = []

    def gelu_partial(y, c0):
        t = _gelu(y)
        t_ref[:, c0 - 4 * width:c0 - 4 * width + PROJ_COLS] = t
        sums.append(jnp.sum(t, axis=-1, keepdims=True))

    segment(4, gelu_partial)
    mu = sum(sums) * (1.0 / width)
    tc = t_ref[...] - mu
    t = tc * jax.lax.rsqrt(jnp.mean(tc * tc, axis=-1, keepdims=True) + EPS)
    o_ref[:, 4 * width:] = (t * lng_ref[...] + lnb_ref[...]).astype(o_ref.dtype)
    segment(3, store(_gelu))
    segment(0, rope(q_scale))
    segment(1, rope(1.0))
    segment(2, store(lambda y: y))


def _proj(xn, w_in_bf, cos, sin, ln_g, ln_b, *, width, tm=512):
    seq, d = xn.shape
    d_in = w_in_bf.shape[1]
    assert d_in == 5 * width
    return pl.pallas_call(
        functools.partial(_proj_kernel, q_scale=HEAD_DIM ** -0.5 * np.log2(np.e)),
        grid=(seq // tm,),
        in_specs=[
            pl.BlockSpec((tm, d), lambda m: (m, 0)),
            _resident((d, d_in)),
            pl.BlockSpec((tm, LANES), lambda m: (m, 0)),
            pl.BlockSpec((tm, LANES), lambda m: (m, 0)),
            pl.BlockSpec((1, width), lambda m: (0, 0)),
            pl.BlockSpec((1, width), lambda m: (0, 0)),
        ],
        out_specs=pl.BlockSpec((tm, d_in), lambda m: (m, 0)),
        out_shape=jax.ShapeDtypeStruct((seq, d_in), BF16),
        scratch_shapes=[pltpu.VMEM((tm, width), F32)],
        compiler_params=_params(1),
        name="proj",
    )(xn, w_in_bf, cos, sin, ln_g.reshape(1, width), ln_b.reshape(1, width))


def _attn_kernel(*refs, dil, n_cast):
    q_ref, k_ref, v_ref = refs[:3]
    cast_in = refs[3:3 + n_cast]
    o_ref = refs[3 + n_cast]
    cast_out = refs[4 + n_cast:4 + 2 * n_cast]
    stage, q4, k4, v4, kb, vb, m4, l4, acc4, m_s, l_s, acc_s, bias = refs[4 + 2 * n_cast:]
    _run_casts(cast_in, cast_out)
    t = pl.program_id(1)
    sup = q_ref.shape[0]
    sub = sup // dil
    n_units = sup // BLOCK

    @pl.when(t == 0)
    def _():
        for dst in (k4, v4):
            for b in range(dil):
                dst[b * 2 * sub:b * 2 * sub + sub, :] = jnp.zeros((sub, LANES), F32)
        kb[:BLOCK] = jnp.zeros((BLOCK, LANES), BF16)
        vb[:BLOCK] = jnp.zeros((BLOCK, LANES), BF16)

    @pl.when(t > 0)
    def _():
        for dst in (k4, v4):
            for b in range(dil):
                dst[b * 2 * sub:b * 2 * sub + sub, :] = dst[b * 2 * sub + sub:(b + 1) * 2 * sub, :]
        kb[:BLOCK] = kb[sup:, :]
        vb[:BLOCK] = vb[sup:, :]

    def split(dst, rows_per_class, offset):
        for b in range(dil):
            dst[b * rows_per_class + offset:b * rows_per_class + offset + sub, :] = (
                stage[pl.ds(b, sub, stride=dil), :])

    stage[...] = k_ref[...].astype(F32)
    split(k4, 2 * sub, sub)
    stage[...] = v_ref[...].astype(F32)
    split(v4, 2 * sub, sub)
    stage[...] = q_ref[...].astype(F32)
    split(q4, sub, 0)
    kb[BLOCK:] = k_ref[...]
    vb[BLOCK:] = v_ref[...]

    qi = jax.lax.broadcasted_iota(jnp.int32, (BLOCK, 2 * BLOCK), 0)
    kj = jax.lax.broadcasted_iota(jnp.int32, (BLOCK, 2 * BLOCK), 1)
    diff = qi + BLOCK - kj
    band = (diff >= 0) & (diff <= BLOCK)
    bias[0] = jnp.where(band & (kj >= BLOCK), 0.0, -jnp.inf)
    bias[1] = jnp.where(band, 0.0, -jnp.inf)
    low = jax.lax.broadcasted_iota(jnp.int32, (BLOCK, LANES), 1) < HEAD_DIM

    def head_pair(q, k, v, has_prev):
        mask = bias[has_prev.astype(jnp.int32)]
        ms, ls, os = [], [], []
        for hh in range(2):
            qm = jnp.where(low if hh == 0 else ~low, q, jnp.zeros_like(q))
            s = jax.lax.dot_general(qm, k, (((1,), (1,)), ((), ())),
                                    preferred_element_type=F32)
            s = s + mask
            m = jnp.max(s, axis=-1, keepdims=True)
            e = jnp.exp2(s - m)
            ms.append(m)
            ls.append(jnp.sum(e, axis=-1, keepdims=True))
            os.append(jnp.dot(e.astype(BF16), v, preferred_element_type=F32))
        return (jnp.where(low, ms[0], ms[1]), jnp.where(low, ls[0], ls[1]),
                jnp.where(low, os[0], os[1]))

    def merge(refs, rows, stats):
        m_ref, l_ref, acc_ref = refs
        m_u, l_u, o_u = stats
        m_old = m_ref[rows, :]
        m_new = jnp.maximum(m_old, m_u)
        a_old = jnp.exp2(m_old - m_new)
        a_u = jnp.exp2(m_u - m_new)
        m_ref[rows, :] = m_new
        l_ref[rows, :] = a_old * l_ref[rows, :] + a_u * l_u
        acc_ref[rows, :] = a_old * acc_ref[rows, :] + a_u * o_u

    def mid_unit(u, carry):
        r = u // (sub // BLOCK)
        n = u % (sub // BLOCK)
        rows = pl.ds(pl.multiple_of(r * sub + n * BLOCK, BLOCK), BLOCK)
        kv_rows = pl.ds(pl.multiple_of(r * 2 * sub + sub + (n - 1) * BLOCK, BLOCK), 2 * BLOCK)
        m_u, l_u, o_u = head_pair(q4[rows, :].astype(BF16), k4[kv_rows, :].astype(BF16),
                                  v4[kv_rows, :].astype(BF16), (t > 0) | (n > 0))
        m4[rows, :] = m_u
        l4[rows, :] = l_u
        acc4[rows, :] = o_u
        return carry

    def wide_unit(r, carry):
        a = r // dil
        b = r % dil
        rows = pl.ds(b * sub + a, BLOCK, stride=dil)
        kv_rows = pl.ds(b * 2 * sub + a, 2 * BLOCK, stride=dil)
        stats = head_pair(q4[rows, :].astype(BF16), k4[kv_rows, :].astype(BF16),
                          v4[kv_rows, :].astype(BF16), t > 0)
        merge((m4, l4, acc4), rows, stats)
        return carry

    def near_unit(n, carry):
        rows = pl.ds(pl.multiple_of(n * BLOCK, BLOCK), BLOCK)
        kv_rows = pl.ds(pl.multiple_of(n * BLOCK, BLOCK), 2 * BLOCK)
        stats = head_pair(q_ref[rows, :], kb[kv_rows, :], vb[kv_rows, :], (t > 0) | (n > 0))
        merge((m_s, l_s, acc_s), rows, stats)
        return carry

    jax.lax.fori_loop(0, n_units, mid_unit, 0, unroll=True)
    jax.lax.fori_loop(0, n_units, wide_unit, 0, unroll=True)
    for b in range(dil):
        src = slice(b * sub, (b + 1) * sub)
        dst = pl.ds(b, sub, stride=dil)
        m_s[dst, :] = m4[src, :]
        l_s[dst, :] = l4[src, :]
        acc_s[dst, :] = acc4[src, :]
    jax.lax.fori_loop(0, n_units, near_unit, 0, unroll=True)

    o_ref[...] = (acc_s[...] / l_s[...]).astype(o_ref.dtype)


def _attention(proj, cast_weights=(), *, width):
    seq, d_in = proj.shape
    dilations = tuple(d for _, d in DILATED_BRANCHES)
    steps = {w // d for w, d in DILATED_BRANCHES}
    assert steps == {BLOCK}, "every branch must span exactly one previous block"
    dil = dilations[1]
    assert dilations == (1, dil, dil * dil)
    sup = BLOCK * dil * dil
    seg = width // LANES
    n_sup = seq // sup
    col = lambda i: pl.BlockSpec((sup, LANES), lambda p, t: (t, i * seg + p))
    f32_rows = lambda n: pltpu.VMEM((n, LANES), F32)
    cast_specs, cast_shapes = _cast_jobs(cast_weights, n_sup * seg,
                                         lambda p, t: (p * n_sup + t, 0))
    return pl.pallas_call(
        functools.partial(_attn_kernel, dil=dil, n_cast=len(cast_weights)),
        grid=(seg, n_sup),
        in_specs=[col(0), col(1), col(2)] + cast_specs,
        out_specs=[pl.BlockSpec((sup, LANES), lambda p, t: (t, p))] + cast_specs,
        out_shape=[jax.ShapeDtypeStruct((seq, width), BF16)] + cast_shapes,
        scratch_shapes=[f32_rows(sup),
                        f32_rows(sup), f32_rows(2 * sup), f32_rows(2 * sup),
                        pltpu.VMEM((sup + BLOCK, LANES), BF16),
                        pltpu.VMEM((sup + BLOCK, LANES), BF16),
                        f32_rows(sup), f32_rows(sup), f32_rows(sup),
                        f32_rows(sup), f32_rows(sup), f32_rows(sup),
                        pltpu.VMEM((2, BLOCK, 2 * BLOCK), F32)],
        compiler_params=_params(2),
        name="dilated_attn",
    )(proj, proj, proj, *cast_weights)


def _mix_kernel(h_ref, attn_ref, u_ref, gv_ref, ws_ref, bs_ref, ag_ref, sg_ref, wo_ref, ng_ref,
                out_ref, xn_ref, mixed_ref):
    tm = h_ref.shape[0]
    width = u_ref.shape[1]
    mixed_ref[:, :width] = _rms(attn_ref[...].astype(F32), ag_ref[...]).astype(BF16)

    ci = jax.lax.broadcasted_iota(jnp.int32, (SGU_CHUNK, SGU_CHUNK), 0)
    cj = jax.lax.broadcasted_iota(jnp.int32, (SGU_CHUNK, SGU_CHUNK), 1)
    causal = cj <= ci
    for g in range(width // SGU_GROUP_DIM):
        gs = slice(g * SGU_GROUP_DIM, (g + 1) * SGU_GROUP_DIM)
        w = jnp.where(causal, ws_ref[g], 0.0).astype(BF16)
        b = bs_ref[:, g:g + 1]
        for c in range(tm // SGU_CHUNK):
            rs = slice(c * SGU_CHUNK, (c + 1) * SGU_CHUNK)
            sv = jnp.dot(w, gv_ref[rs, gs], preferred_element_type=F32) + b
            out_ref[rs, gs] = u_ref[rs, gs].astype(F32) * sv
    sgu = out_ref[:, :width]
    mixed_ref[:, width:] = _rms(sgu, sg_ref[...]).astype(BF16)

    h = h_ref[...] + jnp.dot(mixed_ref[...], wo_ref[...], preferred_element_type=F32)
    out_ref[...] = h
    xn_ref[...] = _rms(h, ng_ref[...]).astype(BF16)


def _mix(h, attn, proj, w_s, b_s, attn_g, sgu_g, w_out, next_g, *, width, tm=512):
    seq, d = h.shape
    n_grp = w_s.shape[0]
    row = lambda m: (m, 0)
    const2 = lambda m: (0, 0)
    return pl.pallas_call(
        _mix_kernel,
        grid=(seq // tm,),
        in_specs=[pl.BlockSpec((tm, d), row),
                  pl.BlockSpec((tm, width), row),
                  pl.BlockSpec((tm, width), lambda m: (m, 3)),
                  pl.BlockSpec((tm, width), lambda m: (m, 4)),
                  pl.BlockSpec((n_grp, SGU_CHUNK, SGU_CHUNK), lambda m: (0, 0, 0)),
                  pl.BlockSpec((SGU_CHUNK, n_grp), const2),
                  pl.BlockSpec((1, width), const2),
                  pl.BlockSpec((1, width), const2),
                  _resident((2 * width, d)),
                  pl.BlockSpec((1, d), const2)],
        out_specs=[pl.BlockSpec((tm, d), row), pl.BlockSpec((tm, d), row)],
        out_shape=[jax.ShapeDtypeStruct((seq, d), F32), jax.ShapeDtypeStruct((seq, d), BF16)],
        scratch_shapes=[pltpu.VMEM((tm, 2 * width), BF16)],
        compiler_params=_params(1),
        name="mix_out",
    )(h, attn, proj, proj, w_s, b_s.T, attn_g.reshape(1, width), sgu_g.reshape(1, width),
      w_out, next_g.reshape(1, d))


def _mem_fold_kernel(mem_ref, g_ref, wk_ref, wv_ref, wq_ref, wo_ref, kq_ref, vo_ref):
    hd = wk_ref.shape[1]
    mk = _rms(mem_ref[...], g_ref[...]).astype(BF16)
    k = jnp.dot(mk, wk_ref[...], preferred_element_type=F32).astype(BF16)
    v = jnp.dot(mk, wv_ref[...], preferred_element_type=F32).astype(BF16)
    kq = jax.lax.dot_general(wq_ref[...], k, (((1,), (1,)), ((), ())),
                             preferred_element_type=F32)
    kq_ref[...] = (kq * (hd ** -0.5)).astype(BF16)
    vo_ref[...] = jnp.dot(v, wo_ref[...], preferred_element_type=F32).astype(BF16)


def _mem_fold(mem, g, w_k, w_v, w_q, w_o):
    n_mem, d = mem.shape
    hd = d // N_MEM_HEADS
    cols = pl.BlockSpec((d, hd), lambda h: (0, h))
    return pl.pallas_call(
        _mem_fold_kernel,
        grid=(N_MEM_HEADS,),
        in_specs=[pl.BlockSpec((n_mem, d), lambda h: (0, 0)),
                  pl.BlockSpec((1, d), lambda h: (0, 0)),
                  cols, cols, cols,
                  pl.BlockSpec((hd, d), lambda h: (h, 0))],
        out_specs=[pl.BlockSpec((d, n_mem), lambda h: (0, h)),
                   pl.BlockSpec((n_mem, d), lambda h: (h, 0))],
        out_shape=[jax.ShapeDtypeStruct((d, N_MEM_HEADS * n_mem), BF16),
                   jax.ShapeDtypeStruct((N_MEM_HEADS * n_mem, d), BF16)],
        compiler_params=_params(1),
        name="mem_fold",
    )(mem, g.reshape(1, d), w_k, w_v, w_q, w_o)


def _mem_cross_kernel(h_ref, xq_ref, kq_ref, vo_ref, ng_ref, out_ref, xn_ref, p_ref):
    n_mem = kq_ref.shape[1] // N_MEM_HEADS
    for i in range(N_MEM_HEADS):
        sl = slice(i * n_mem, (i + 1) * n_mem)
        s = jnp.dot(xq_ref[...], kq_ref[:, sl], preferred_element_type=F32)
        m = jnp.max(s, axis=-1, keepdims=True)
        e = jnp.exp(s - m)
        p_ref[:, sl] = (e / jnp.sum(e, axis=-1, keepdims=True)).astype(BF16)
    h = h_ref[...] + jnp.dot(p_ref[...], vo_ref[...], preferred_element_type=F32)
    out_ref[...] = h
    xn_ref[...] = _rms(h, ng_ref[...]).astype(BF16)


def _mem_cross(h, xq, kq, vo, next_g, *, tm=512):
    seq, d = h.shape
    row = pl.BlockSpec((tm, d), lambda m: (m, 0))
    return pl.pallas_call(
        _mem_cross_kernel,
        grid=(seq // tm,),
        in_specs=[row, row, _resident(kq.shape), _resident(vo.shape),
                  pl.BlockSpec((1, d), lambda m: (0, 0))],
        out_specs=[row, row],
        out_shape=[jax.ShapeDtypeStruct((seq, d), F32), jax.ShapeDtypeStruct((seq, d), BF16)],
        scratch_shapes=[pltpu.VMEM((tm, kq.shape[1]), BF16)],
        compiler_params=_params(1),
        name="mem_cross",
    )(h, xq, kq, vo, next_g.reshape(1, d))


def kernel(x, mem, positions, ffn1_norm, ffn1_w_gate, ffn1_w_up, ffn1_w_down, mix_norm, w_in,
           sgu_ln_gain, sgu_ln_bias, sgu_w_s, sgu_b_s, attn_out_gain, sgu_out_gain, w_out,
           mem_q_norm, mem_kv_norm, mem_w_q, mem_w_k, mem_w_v, mem_w_o, ffn2_norm,
           ffn2_w_gate, ffn2_w_up, ffn2_w_down, final_norm):
    batch, seq, d = x.shape
    depth = ffn1_norm.shape[0]
    width = attn_out_gain.shape[1]
    assert batch == 1 and mem.shape[0] == 1 and depth == 1
    layer = 0

    h = x.reshape(seq, d)
    mem2 = mem.reshape(mem.shape[1], d)

    cos, sin, xn = _prep(positions, h, ffn1_norm[layer])
    a, wd_bf = _ffn_up(xn, ffn1_w_gate[layer], ffn1_w_up[layer], ffn1_w_down[layer])
    h, xn, w_out_bf, w_in_bf, wk_bf, wv_bf, wq_bf, wo_bf = _ffn_down(
        a, h, wd_bf, mix_norm[layer],
        (w_out[layer], w_in[layer], mem_w_k[layer], mem_w_v[layer], mem_w_q[layer],
         mem_w_o[layer]), final=False)
    proj = _proj(xn, w_in_bf, cos, sin, sgu_ln_gain[layer], sgu_ln_bias[layer], width=width)
    attn, wg2_bf, wu2_bf, wd2_bf = _attention(
        proj, (ffn2_w_gate[layer], ffn2_w_up[layer], ffn2_w_down[layer]), width=width)
    h, xn = _mix(h, attn, proj, sgu_w_s[layer], sgu_b_s[layer], attn_out_gain[layer],
                 sgu_out_gain[layer], w_out_bf, mem_q_norm[layer], width=width)
    kq, vo = _mem_fold(mem2, mem_kv_norm[layer], wk_bf, wv_bf, wq_bf, wo_bf)
    h, xn = _mem_cross(h, xn, kq, vo, ffn2_norm[layer])
    a, = _ffn_up(xn, wg2_bf, wu2_bf)
    out = _ffn_down(a, h, wd2_bf, final_norm, final=True)[0]
    return out.reshape(batch, seq, d)
```

```python
import functools

import jax
import jax.numpy as jnp
import numpy as np
from jax.experimental import pallas as pl
from jax.experimental.pallas import tpu as pltpu

F32 = jnp.float32
BF16 = jnp.bfloat16

EPS = 1e-6
ROPE_THETA = 10000.0
FFN_RES_SCALE = 0.5
HEAD_DIM = 64
BLOCK = 128
DILATED_BRANCHES = ((128, 1), (512, 4), (2048, 16))
SGU_CHUNK = 128
SGU_GROUP_DIM = 128
N_MEM_HEADS = 4
LANES = 128
PROJ_COLS = 1024
CAST_ROWS = 256
FFN_UP_ROWS = 1024

VMEM_LIMIT = 56 * 1024 * 1024


def _params(n_axes, vmem=VMEM_LIMIT):
    return pltpu.CompilerParams(
        dimension_semantics=("arbitrary",) * n_axes, vmem_limit_bytes=vmem)


def _resident(shape):
    return pl.BlockSpec(shape, lambda *_: (0,) * len(shape), pipeline_mode=pl.Buffered(1))


def _rms(x, g):
    return x * jax.lax.rsqrt(jnp.mean(x * x, axis=-1, keepdims=True) + EPS) * g


def _gelu(x):
    c = np.sqrt(2.0 / np.pi).astype(np.float32)
    return 0.5 * x * (1.0 + jnp.tanh(c * (x + 0.044715 * (x * x * x))))


def _cast_to_bf16(src_ref, dst_ref):
    rows = src_ref.shape[0]
    chunk = min(CAST_ROWS, rows)

    def body(i, carry):
        sl = pl.ds(pl.multiple_of(i * chunk, chunk), chunk)
        dst_ref[sl, :] = src_ref[sl, :].astype(BF16)
        return carry

    jax.lax.fori_loop(0, rows // chunk, body, 0)


def _cast_jobs(weights, steps, index_map):
    specs, shapes = [], []
    for w in weights:
        assert w.shape[0] % steps == 0
        specs.append(pl.BlockSpec((w.shape[0] // steps, w.shape[1]), index_map))
        shapes.append(jax.ShapeDtypeStruct(w.shape, BF16))
    return specs, shapes


def _run_casts(srcs, dsts):
    for src, dst in zip(srcs, dsts):
        dst[...] = src[...].astype(BF16)


def _prep_kernel(pos_ref, freq_ref, x_ref, g_ref, cos_ref, sin_ref, xn_ref):
    ang = pos_ref[...].astype(F32) * freq_ref[...]
    lane = jax.lax.broadcasted_iota(jnp.int32, ang.shape, 1)
    sign = jnp.where(lane % HEAD_DIM < HEAD_DIM // 2, -1.0, 1.0)
    cos_ref[...] = jnp.cos(ang)
    sin_ref[...] = jnp.sin(ang) * sign
    xn_ref[...] = _rms(x_ref[...], g_ref[...]).astype(BF16)


def _prep(positions, x, g, *, tm=2048):
    seq, d = x.shape
    half = HEAD_DIM // 2
    inv_freq = ROPE_THETA ** (-jnp.arange(0, HEAD_DIM, 2, dtype=F32) / HEAD_DIM)
    freq_lane = jnp.tile(inv_freq, LANES // half).reshape(1, LANES)
    pos = positions.reshape(seq, 1)
    row = lambda w: pl.BlockSpec((tm, w), lambda i: (i, 0))
    const = lambda w: pl.BlockSpec((1, w), lambda i: (0, 0))
    return pl.pallas_call(
        _prep_kernel,
        grid=(seq // tm,),
        in_specs=[row(1), const(LANES), row(d), const(d)],
        out_specs=[row(LANES), row(LANES), row(d)],
        out_shape=[jax.ShapeDtypeStruct((seq, LANES), F32)] * 2
        + [jax.ShapeDtypeStruct((seq, d), BF16)],
        compiler_params=_params(1),
        name="prep",
    )(pos, freq_lane, x, g.reshape(1, d))


def _ffn_up_kernel(*refs, precast):
    if precast:
        xn_ref, wgb, wub, a_ref = refs
    else:
        xn_ref, wg_ref, wu_ref, wd_ref, a_ref, wdb_ref, wgb, wub = refs
        wdb_ref[...] = wd_ref[...].astype(BF16)

        @pl.when(pl.program_id(1) == 0)
        def _():
            _cast_to_bf16(wg_ref, wgb)
            _cast_to_bf16(wu_ref, wub)

    pass_rows = xn_ref.shape[0] if precast else FFN_UP_ROWS
    for r0 in range(0, xn_ref.shape[0], pass_rows):
        rows = slice(r0, r0 + pass_rows)
        xn = xn_ref[rows, :]
        gate = jnp.dot(xn, wgb[...], preferred_element_type=F32)
        up = jnp.dot(xn, wub[...], preferred_element_type=F32)
        a_ref[rows, :] = ((gate * (FFN_RES_SCALE / (1.0 + jnp.exp(-gate)))) * up).astype(BF16)


def _ffn_up(xn, wg, wu, wd=None, *, tf=512):
    seq, d = xn.shape
    dff = wg.shape[1]
    precast = wd is None
    tm = 2 * FFN_UP_ROWS
    m_steps = seq // tm
    in_specs = [pl.BlockSpec((tm, d), lambda f, m: (m, 0)),
                pl.BlockSpec((d, tf), lambda f, m: (0, f)),
                pl.BlockSpec((d, tf), lambda f, m: (0, f))]
    out_specs = [pl.BlockSpec((tm, tf), lambda f, m: (m, f))]
    out_shape = [jax.ShapeDtypeStruct((seq, dff), BF16)]
    scratch, args = [], [xn, wg, wu]
    if not precast:
        assert dff % ((dff // tf) * m_steps) == 0
        wd_rows = dff // ((dff // tf) * m_steps)
        wd_spec = pl.BlockSpec((wd_rows, d), lambda f, m: (f * m_steps + m, 0))
        in_specs.append(wd_spec)
        out_specs.append(wd_spec)
        out_shape.append(jax.ShapeDtypeStruct((dff, d), BF16))
        scratch = [pltpu.VMEM((d, tf), BF16), pltpu.VMEM((d, tf), BF16)]
        args.append(wd)
    return pl.pallas_call(
        functools.partial(_ffn_up_kernel, precast=precast),
        grid=(dff // tf, seq // tm),
        in_specs=in_specs,
        out_specs=out_specs,
        out_shape=out_shape,
        scratch_shapes=scratch,
        compiler_params=_params(2),
        name="ffn_up_bf16" if precast else "ffn_up",
    )(*args)


def _ffn_down_kernel(*refs, final, n_cast):
    a_ref, x_ref, wd_ref, g_ref = refs[:4]
    cast_in = refs[4:4 + n_cast]
    out_refs = refs[4 + n_cast:]
    h = x_ref[...] + jnp.dot(a_ref[...], wd_ref[...], preferred_element_type=F32)
    if final:
        out_refs[0][...] = _rms(h, g_ref[...])
    else:
        out_refs[0][...] = h
        out_refs[1][...] = _rms(h, g_ref[...]).astype(BF16)
        _run_casts(cast_in, out_refs[2:])


def _ffn_down(a, x, wd_bf, g, cast_weights=(), *, final, tm=256):
    seq, d = x.shape
    dff = a.shape[1]
    steps = seq // tm
    assert not (final and cast_weights)
    row = pl.BlockSpec((tm, d), lambda m: (m, 0))
    cast_specs, cast_shapes = _cast_jobs(cast_weights, steps, lambda m: (m, 0))
    out_specs = ([row] if final else [row, row]) + cast_specs
    out_shape = [jax.ShapeDtypeStruct((seq, d), F32)]
    if not final:
        out_shape.append(jax.ShapeDtypeStruct((seq, d), BF16))
    return pl.pallas_call(
        functools.partial(_ffn_down_kernel, final=final, n_cast=len(cast_weights)),
        grid=(steps,),
        in_specs=[pl.BlockSpec((tm, dff), lambda m: (m, 0)), row,
                  _resident((dff, d)), pl.BlockSpec((1, d), lambda m: (0, 0))] + cast_specs,
        out_specs=out_specs,
        out_shape=out_shape + cast_shapes,
        compiler_params=_params(1),
        name="ffn_down_final" if final else "ffn_down",
    )(a, x, wd_bf, g.reshape(1, d), *cast_weights)


def _proj_kernel(xn_ref, w_ref, cos_ref, sin_ref, lng_ref, lnb_ref, o_ref, t_ref, *, q_scale):
    width = t_ref.shape[1]

    def segment(seg, epilogue):
        for j in range(width // PROJ_COLS):
            c0 = seg * width + j * PROJ_COLS
            y = jnp.dot(xn_ref[...], w_ref[:, c0:c0 + PROJ_COLS], preferred_element_type=F32)
            epilogue(y, c0)

    def rope(scale):
        def epilogue(y, c0):
            cos = cos_ref[...]
            sin = sin_ref[...]
            lane = jax.lax.broadcasted_iota(jnp.int32, cos.shape, 1)
            first = lane % HEAD_DIM < HEAD_DIM // 2
            for j in range(PROJ_COLS // LANES):
                blk = y[:, j * LANES:(j + 1) * LANES]
                partner = jnp.where(first,
                                    pltpu.roll(blk, LANES - HEAD_DIM // 2, 1),
                                    pltpu.roll(blk, HEAD_DIM // 2, 1))
                out = blk * cos + partner * sin
                if scale != 1.0:
                    out = out * scale
                o_ref[:, c0 + j * LANES:c0 + (j + 1) * LANES] = out.astype(o_ref.dtype)
        return epilogue

    def store(fn):
        def epilogue(y, c0):
            o_ref[:, c0:c0 + PROJ_COLS] = fn(y).astype(o_ref.dtype)
        return epilogue

    sums = []

    def gelu_partial(y, c0):
        t = _gelu(y)
        t_ref[:, c0 - 4 * width:c0 - 4 * width + PROJ_COLS] = t
        sums.append(jnp.sum(t, axis=-1, keepdims=True))

    segment(4, gelu_partial)
    mu = sum(sums) * (1.0 / width)
    tc = t_ref[...] - mu
    t = tc * jax.lax.rsqrt(jnp.mean(tc * tc, axis=-1, keepdims=True) + EPS)
    o_ref[:, 4 * width:] = (t * lng_ref[...] + lnb_ref[...]).astype(o_ref.dtype)
    segment(3, store(_gelu))
    segment(0, rope(q_scale))
    segment(1, rope(1.0))
    segment(2, store(lambda y: y))


def _proj(xn, w_in_bf, cos, sin, ln_g, ln_b, *, width, tm=512):
    seq, d = xn.shape
    d_in = w_in_bf.shape[1]
    assert d_in == 5 * width
    return pl.pallas_call(
        functools.partial(_proj_kernel, q_scale=HEAD_DIM ** -0.5 * np.log2(np.e)),
        grid=(seq // tm,),
        in_specs=[
            pl.BlockSpec((tm, d), lambda m: (m, 0)),
            _resident((d, d_in)),
            pl.BlockSpec((tm, LANES), lambda m: (m, 0)),
            pl.BlockSpec((tm, LANES), lambda m: (m, 0)),
            pl.BlockSpec((1, width), lambda m: (0, 0)),
            pl.BlockSpec((1, width), lambda m: (0, 0)),
        ],
        out_specs=pl.BlockSpec((tm, d_in), lambda m: (m, 0)),
        out_shape=jax.ShapeDtypeStruct((seq, d_in), BF16),
        scratch_shapes=[pltpu.VMEM((tm, width), F32)],
        compiler_params=_params(1),
        name="proj",
    )(xn, w_in_bf, cos, sin, ln_g.reshape(1, width), ln_b.reshape(1, width))


def _attn_kernel(*refs, dil, n_cast):
    q_ref, k_ref, v_ref = refs[:3]
    cast_in = refs[3:3 + n_cast]
    o_ref = refs[3 + n_cast]
    cast_out = refs[4 + n_cast:4 + 2 * n_cast]
    stage, q4, k4, v4, kb, vb, m4, l4, acc4, m_s, l_s, acc_s, bias = refs[4 + 2 * n_cast:]
    _run_casts(cast_in, cast_out)
    t = pl.program_id(1)
    sup = q_ref.shape[0]
    sub = sup // dil
    n_units = sup // BLOCK

    @pl.when(t == 0)
    def _():
        for dst in (k4, v4):
            for b in range(dil):
                dst[b * 2 * sub:b * 2 * sub + sub, :] = jnp.zeros((sub, LANES), F32)
        kb[:BLOCK] = jnp.zeros((BLOCK, LANES), BF16)
        vb[:BLOCK] = jnp.zeros((BLOCK, LANES), BF16)

    @pl.when(t > 0)
    def _():
        for dst in (k4, v4):
            for b in range(dil):
                dst[b * 2 * sub:b * 2 * sub + sub, :] = dst[b * 2 * sub + sub:(b + 1) * 2 * sub, :]
        kb[:BLOCK] = kb[sup:, :]
        vb[:BLOCK] = vb[sup:, :]

    def split(dst, rows_per_class, offset):
        for b in range(dil):
            dst[b * rows_per_class + offset:b * rows_per_class + offset + sub, :] = (
                stage[pl.ds(b, sub, stride=dil), :])

    stage[...] = k_ref[...].astype(F32)
    split(k4, 2 * sub, sub)
    stage[...] = v_ref[...].astype(F32)
    split(v4, 2 * sub, sub)
    stage[...] = q_ref[...].astype(F32)
    split(q4, sub, 0)
    kb[BLOCK:] = k_ref[...]
    vb[BLOCK:] = v_ref[...]

    qi = jax.lax.broadcasted_iota(jnp.int32, (BLOCK, 2 * BLOCK), 0)
    kj = jax.lax.broadcasted_iota(jnp.int32, (BLOCK, 2 * BLOCK), 1)
    diff = qi + BLOCK - kj
    band = (diff >= 0) & (diff <= BLOCK)
    bias[0] = jnp.where(band & (kj >= BLOCK), 0.0, -jnp.inf)
    bias[1] = jnp.where(band, 0.0, -jnp.inf)
    low = jax.lax.broadcasted_iota(jnp.int32, (BLOCK, LANES), 1) < HEAD_DIM

    def head_pair(q, k, v, has_prev):
        mask = bias[has_prev.astype(jnp.int32)]
        ms, ls, os = [], [], []
        for hh in range(2):
            qm = jnp.where(low if hh == 0 else ~low, q, jnp.zeros_like(q))
            s = jax.lax.dot_general(qm, k, (((1,), (1,)), ((), ())),
                                    preferred_element_type=F32)
            s = s + mask
            m = jnp.max(s, axis=-1, keepdims=True)
            e = jnp.exp2(s - m)
            ms.append(m)
            ls.append(jnp.sum(e, axis=-1, keepdims=True))
            os.append(jnp.dot(e.astype(BF16), v, preferred_element_type=F32))
        return (jnp.where(low, ms[0], ms[1]), jnp.where(low, ls[0], ls[1]),
                jnp.where(low, os[0], os[1]))

    def merge(refs, rows, stats):
        m_ref, l_ref, acc_ref = refs
        m_u, l_u, o_u = stats
        m_old = m_ref[rows, :]
        m_new = jnp.maximum(m_old, m_u)
        a_old = jnp.exp2(m_old - m_new)
        a_u = jnp.exp2(m_u - m_new)
        m_ref[rows, :] = m_new
        l_ref[rows, :] = a_old * l_ref[rows, :] + a_u * l_u
        acc_ref[rows, :] = a_old * acc_ref[rows, :] + a_u * o_u

    def mid_unit(u, carry):
        r = u // (sub // BLOCK)
        n = u % (sub // BLOCK)
        rows = pl.ds(pl.multiple_of(r * sub + n * BLOCK, BLOCK), BLOCK)
        kv_rows = pl.ds(pl.multiple_of(r * 2 * sub + sub + (n - 1) * BLOCK, BLOCK), 2 * BLOCK)
        m_u, l_u, o_u = head_pair(q4[rows, :].astype(BF16), k4[kv_rows, :].astype(BF16),
                                  v4[kv_rows, :].astype(BF16), (t > 0) | (n > 0))
        m4[rows, :] = m_u
        l4[rows, :] = l_u
        acc4[rows, :] = o_u
        return carry

    def wide_unit(r, carry):
        a = r // dil
        b = r % dil
        rows = pl.ds(b * sub + a, BLOCK, stride=dil)
        kv_rows = pl.ds(b * 2 * sub + a, 2 * BLOCK, stride=dil)
        stats = head_pair(q4[rows, :].astype(BF16), k4[kv_rows, :].astype(BF16),
                          v4[kv_rows, :].astype(BF16), t > 0)
        merge((m4, l4, acc4), rows, stats)
        return carry

    def near_unit(n, carry):
        rows = pl.ds(pl.multiple_of(n * BLOCK, BLOCK), BLOCK)
        kv_rows = pl.ds(pl.multiple_of(n * BLOCK, BLOCK), 2 * BLOCK)
        stats = head_pair(q_ref[rows, :], kb[kv_rows, :], vb[kv_rows, :], (t > 0) | (n > 0))
        merge((m_s, l_s, acc_s), rows, stats)
        return carry

    jax.lax.fori_loop(0, n_units, mid_unit, 0, unroll=True)
    jax.lax.fori_loop(0, n_units, wide_unit, 0, unroll=True)
    for b in range(dil):
        src = slice(b * sub, (b + 1) * sub)
        dst = pl.ds(b, sub, stride=dil)
        m_s[dst, :] = m4[src, :]
        l_s[dst, :] = l4[src, :]
        acc_s[dst, :] = acc4[src, :]
    jax.lax.fori_loop(0, n_units, near_unit, 0, unroll=True)

    o_ref[...] = (acc_s[...] / l_s[...]).astype(o_ref.dtype)


def _attention(proj, cast_weights=(), *, width):
    seq, d_in = proj.shape
    dilations = tuple(d for _, d in DILATED_BRANCHES)
    steps = {w // d for w, d in DILATED_BRANCHES}
    assert steps == {BLOCK}, "every branch must span exactly one previous block"
    dil = dilations[1]
    assert dilations == (1, dil, dil * dil)
    sup = BLOCK * dil * dil
    seg = width // LANES
    n_sup = seq // sup
    col = lambda i: pl.BlockSpec((sup, LANES), lambda p, t: (t, i * seg + p))
    f32_rows = lambda n: pltpu.VMEM((n, LANES), F32)
    cast_specs, cast_shapes = _cast_jobs(cast_weights, n_sup * seg,
                                         lambda p, t: (p * n_sup + t, 0))
    return pl.pallas_call(
        functools.partial(_attn_kernel, dil=dil, n_cast=len(cast_weights)),
        grid=(seg, n_sup),
        in_specs=[col(0), col(1), col(2)] + cast_specs,
        out_specs=[pl.BlockSpec((sup, LANES), lambda p, t: (t, p))] + cast_specs,
        out_shape=[jax.ShapeDtypeStruct((seq, width), BF16)] + cast_shapes,
        scratch_shapes=[f32_rows(sup),
                        f32_rows(sup), f32_rows(2 * sup), f32_rows(2 * sup),
                        pltpu.VMEM((sup + BLOCK, LANES), BF16),
                        pltpu.VMEM((sup + BLOCK, LANES), BF16),
                        f32_rows(sup), f32_rows(sup), f32_rows(sup),
                        f32_rows(sup), f32_rows(sup), f32_rows(sup),
                        pltpu.VMEM((2, BLOCK, 2 * BLOCK), F32)],
        compiler_params=_params(2),
        name="dilated_attn",
    )(proj, proj, proj, *cast_weights)


def _mix_kernel(h_ref, attn_ref, u_ref, gv_ref, ws_ref, bs_ref, ag_ref, sg_ref, wo_ref, ng_ref,
                out_ref, xn_ref, mixed_ref):
    tm = h_ref.shape[0]
    width = u_ref.shape[1]
    mixed_ref[:, :width] = _rms(attn_ref[...].astype(F32), ag_ref[...]).astype(BF16)

    ci = jax.lax.broadcasted_iota(jnp.int32, (SGU_CHUNK, SGU_CHUNK), 0)
    cj = jax.lax.broadcasted_iota(jnp.int32, (SGU_CHUNK, SGU_CHUNK), 1)
    causal = cj <= ci
    for g in range(width // SGU_GROUP_DIM):
        gs = slice(g * SGU_GROUP_DIM, (g + 1) * SGU_GROUP_DIM)
        w = jnp.where(causal, ws_ref[g], 0.0).astype(BF16)
        b = bs_ref[:, g:g + 1]
        for c in range(tm // SGU_CHUNK):
            rs = slice(c * SGU_CHUNK, (c + 1) * SGU_CHUNK)
            sv = jnp.dot(w, gv_ref[rs, gs], preferred_element_type=F32) + b
            out_ref[rs, gs] = u_ref[rs, gs].astype(F32) * sv
    sgu = out_ref[:, :width]
    mixed_ref[:, width:] = _rms(sgu, sg_ref[...]).astype(BF16)

    h = h_ref[...] + jnp.dot(mixed_ref[...], wo_ref[...], preferred_element_type=F32)
    out_ref[...] = h
    xn_ref[...] = _rms(h, ng_ref[...]).astype(BF16)


def _mix(h, attn, proj, w_s, b_s, attn_g, sgu_g, w_out, next_g, *, width, tm=512):
    seq, d = h.shape
    n_grp = w_s.shape[0]
    row = lambda m: (m, 0)
    const2 = lambda m: (0, 0)
    return pl.pallas_call(
        _mix_kernel,
        grid=(seq // tm,),
        in_specs=[pl.BlockSpec((tm, d), row),
                  pl.BlockSpec((tm, width), row),
                  pl.BlockSpec((tm, width), lambda m: (m, 3)),
                  pl.BlockSpec((tm, width), lambda m: (m, 4)),
                  pl.BlockSpec((n_grp, SGU_CHUNK, SGU_CHUNK), lambda m: (0, 0, 0)),
                  pl.BlockSpec((SGU_CHUNK, n_grp), const2),
                  pl.BlockSpec((1, width), const2),
                  pl.BlockSpec((1, width), const2),
                  _resident((2 * width, d)),
                  pl.BlockSpec((1, d), const2)],
        out_specs=[pl.BlockSpec((tm, d), row), pl.BlockSpec((tm, d), row)],
        out_shape=[jax.ShapeDtypeStruct((seq, d), F32), jax.ShapeDtypeStruct((seq, d), BF16)],
        scratch_shapes=[pltpu.VMEM((tm, 2 * width), BF16)],
        compiler_params=_params(1),
        name="mix_out",
    )(h, attn, proj, proj, w_s, b_s.T, attn_g.reshape(1, width), sgu_g.reshape(1, width),
      w_out, next_g.reshape(1, d))


def _mem_fold_kernel(mem_ref, g_ref, wk_ref, wv_ref, wq_ref, wo_ref, kq_ref, vo_ref):
    hd = wk_ref.shape[1]
    mk = _rms(mem_ref[...], g_ref[...]).astype(BF16)
    k = jnp.dot(mk, wk_ref[...], preferred_element_type=F32).astype(BF16)
    v = jnp.dot(mk, wv_ref[...], preferred_element_type=F32).astype(BF16)
    kq = jax.lax.dot_general(wq_ref[...], k, (((1,), (1,)), ((), ())),
                             preferred_element_type=F32)
    kq_ref[...] = (kq * (hd ** -0.5)).astype(BF16)
    vo_ref[...] = jnp.dot(v, wo_ref[...], preferred_element_type=F32).astype(BF16)


def _mem_fold(mem, g, w_k, w_v, w_q, w_o):
    n_mem, d = mem.shape
    hd = d // N_MEM_HEADS
    cols = pl.BlockSpec((d, hd), lambda h: (0, h))
    return pl.pallas_call(
        _mem_fold_kernel,
        grid=(N_MEM_HEADS,),
        in_specs=[pl.BlockSpec((n_mem, d), lambda h: (0, 0)),
                  pl.BlockSpec((1, d), lambda h: (0, 0)),
                  cols, cols, cols,
                  pl.BlockSpec((hd, d), lambda h: (h, 0))],
        out_specs=[pl.BlockSpec((d, n_mem), lambda h: (0, h)),
                   pl.BlockSpec((n_mem, d), lambda h: (h, 0))],
        out_shape=[jax.ShapeDtypeStruct((d, N_MEM_HEADS * n_mem), BF16),
                   jax.ShapeDtypeStruct((N_MEM_HEADS * n_mem, d), BF16)],
        compiler_params=_params(1),
        name="mem_fold",
    )(mem, g.reshape(1, d), w_k, w_v, w_q, w_o)


def _mem_cross_kernel(h_ref, xq_ref, kq_ref, vo_ref, ng_ref, out_ref, xn_ref, p_ref):
    n_mem = kq_ref.shape[1] // N_MEM_HEADS
    for i in range(N_MEM_HEADS):
        sl = slice(i * n_mem, (i + 1) * n_mem)
        s = jnp.dot(xq_ref[...], kq_ref[:, sl], preferred_element_type=F32)
        m = jnp.max(s, axis=-1, keepdims=True)
        e = jnp.exp(s - m)
        p_ref[:, sl] = (e / jnp.sum(e, axis=-1, keepdims=True)).astype(BF16)
    h = h_ref[...] + jnp.dot(p_ref[...], vo_ref[...], preferred_element_type=F32)
    out_ref[...] = h
    xn_ref[...] = _rms(h, ng_ref[...]).astype(BF16)


def _mem_cross(h, xq, kq, vo, next_g, *, tm=512):
    seq, d = h.shape
    row = pl.BlockSpec((tm, d), lambda m: (m, 0))
    return pl.pallas_call(
        _mem_cross_kernel,
        grid=(seq // tm,),
        in_specs=[row, row, _resident(kq.shape), _resident(vo.shape),
                  pl.BlockSpec((1, d), lambda m: (0, 0))],
        out_specs=[row, row],
        out_shape=[jax.ShapeDtypeStruct((seq, d), F32), jax.ShapeDtypeStruct((seq, d), BF16)],
        scratch_shapes=[pltpu.VMEM((tm, kq.shape[1]), BF16)],
        compiler_params=_params(1),
        name="mem_cross",
    )(h, xq, kq, vo, next_g.reshape(1, d))


def kernel(x, mem, positions, ffn1_norm, ffn1_w_gate, ffn1_w_up, ffn1_w_down, mix_norm, w_in,
           sgu_ln_gain, sgu_ln_bias, sgu_w_s, sgu_b_s, attn_out_gain, sgu_out_gain, w_out,
           mem_q_norm, mem_kv_norm, mem_w_q, mem_w_k, mem_w_v, mem_w_o, ffn2_norm,
           ffn2_w_gate, ffn2_w_up, ffn2_w_down, final_norm):
    batch, seq, d = x.shape
    depth = ffn1_norm.shape[0]
    width = attn_out_gain.shape[1]
    assert batch == 1 and mem.shape[0] == 1 and depth == 1
    layer = 0

    h = x.reshape(seq, d)
    mem2 = mem.reshape(mem.shape[1], d)

    cos, sin, xn = _prep(positions, h, ffn1_norm[layer])
    a, wd_bf = _ffn_up(xn, ffn1_w_gate[layer], ffn1_w_up[layer], ffn1_w_down[layer])
    h, xn, w_out_bf, w_in_bf, wk_bf, wv_bf, wq_bf, wo_bf = _ffn_down(
        a, h, wd_bf, mix_norm[layer],
        (w_out[layer], w_in[layer], mem_w_k[layer], mem_w_v[layer], mem_w_q[layer],
         mem_w_o[layer]), final=False)
    proj = _proj(xn, w_in_bf, cos, sin, sgu_ln_gain[layer], sgu_ln_bias[layer], width=width)
    attn, wg2_bf, wu2_bf, wd2_bf = _attention(
        proj, (ffn2_w_gate[layer], ffn2_w_up[layer], ffn2_w_down[layer]), width=width)
    h, xn = _mix(h, attn, proj, sgu_w_s[layer], sgu_b_s[layer], attn_out_gain[layer],
                 sgu_out_gain[layer], w_out_bf, mem_q_norm[layer], width=width)
    kq, vo = _mem_fold(mem2, mem_kv_norm[layer], wk_bf, wv_bf, wq_bf, wo_bf)
    h, xn = _mem_cross(h, xn, kq, vo, ffn2_norm[layer])
    a, = _ffn_up(xn, wg2_bf, wu2_bf)
    out = _ffn_down(a, h, wd2_bf, final_norm, final=True)[0]
    return out.reshape(batch, seq, d)
```

```python
import functools

import jax
import jax.numpy as jnp
import numpy as np
from jax.experimental import pallas as pl
from jax.experimental.pallas import tpu as pltpu

F32 = jnp.float32
BF16 = jnp.bfloat16

EPS = 1e-6
ROPE_THETA = 10000.0
FFN_RES_SCALE = 0.5
HEAD_DIM = 64
BLOCK = 128
DILATED_BRANCHES = ((128, 1), (512, 4), (2048, 16))
SGU_CHUNK = 128
SGU_GROUP_DIM = 128
N_MEM_HEADS = 4
LANES = 128
PROJ_COLS = 1024
CAST_ROWS = 256
FFN_UP_TILE = 2048
FFN_UP_ROWS = 512

VMEM_LIMIT = 56 * 1024 * 1024


def _params(n_axes, vmem=VMEM_LIMIT):
    return pltpu.CompilerParams(
        dimension_semantics=("arbitrary",) * n_axes, vmem_limit_bytes=vmem)


def _resident(shape):
    return pl.BlockSpec(shape, lambda *_: (0,) * len(shape), pipeline_mode=pl.Buffered(1))


def _rms(x, g):
    return x * jax.lax.rsqrt(jnp.mean(x * x, axis=-1, keepdims=True) + EPS) * g


def _gelu(x):
    c = np.sqrt(2.0 / np.pi).astype(np.float32)
    return 0.5 * x * (1.0 + jnp.tanh(c * (x + 0.044715 * (x * x * x))))


def _cast_to_bf16(src_ref, dst_ref):
    rows = src_ref.shape[0]
    chunk = min(CAST_ROWS, rows)

    def body(i, carry):
        sl = pl.ds(pl.multiple_of(i * chunk, chunk), chunk)
        dst_ref[sl, :] = src_ref[sl, :].astype(BF16)
        return carry

    jax.lax.fori_loop(0, rows // chunk, body, 0)


def _cast_jobs(weights, steps, index_map):
    specs, shapes = [], []
    for w in weights:
        assert w.shape[0] % steps == 0
        specs.append(pl.BlockSpec((w.shape[0] // steps, w.shape[1]), index_map))
        shapes.append(jax.ShapeDtypeStruct(w.shape, BF16))
    return specs, shapes


def _run_casts(srcs, dsts):
    for src, dst in zip(srcs, dsts):
        dst[...] = src[...].astype(BF16)


def _prep_kernel(pos_ref, freq_ref, x_ref, g_ref, cos_ref, sin_ref, xn_ref):
    ang = pos_ref[...].astype(F32) * freq_ref[...]
    lane = jax.lax.broadcasted_iota(jnp.int32, ang.shape, 1)
    sign = jnp.where(lane % HEAD_DIM < HEAD_DIM // 2, -1.0, 1.0)
    cos_ref[...] = jnp.cos(ang)
    sin_ref[...] = jnp.sin(ang) * sign
    xn_ref[...] = _rms(x_ref[...], g_ref[...]).astype(BF16)


def _prep(positions, x, g, *, tm=2048):
    seq, d = x.shape
    half = HEAD_DIM // 2
    inv_freq = ROPE_THETA ** (-jnp.arange(0, HEAD_DIM, 2, dtype=F32) / HEAD_DIM)
    freq_lane = jnp.tile(inv_freq, LANES // half).reshape(1, LANES)
    pos = positions.reshape(seq, 1)
    row = lambda w: pl.BlockSpec((tm, w), lambda i: (i, 0))
    const = lambda w: pl.BlockSpec((1, w), lambda i: (0, 0))
    return pl.pallas_call(
        _prep_kernel,
        grid=(seq // tm,),
        in_specs=[row(1), const(LANES), row(d), const(d)],
        out_specs=[row(LANES), row(LANES), row(d)],
        out_shape=[jax.ShapeDtypeStruct((seq, LANES), F32)] * 2
        + [jax.ShapeDtypeStruct((seq, d), BF16)],
        compiler_params=_params(1),
        name="prep",
    )(pos, freq_lane, x, g.reshape(1, d))


def _ffn_up_kernel(*refs, precast):
    if precast:
        xn_ref, wgb, wub, a_ref = refs
    else:
        xn_ref, wg_ref, wu_ref, wd_ref, a_ref, wdb_ref, wgb, wub = refs
        wdb_ref[...] = wd_ref[...].astype(BF16)

        @pl.when(pl.program_id(1) == 0)
        def _():
            _cast_to_bf16(wg_ref, wgb)
            _cast_to_bf16(wu_ref, wub)

    for r0 in range(0, xn_ref.shape[0], FFN_UP_ROWS):
        rows = slice(r0, r0 + FFN_UP_ROWS)
        xn = xn_ref[rows, :]
        gate = jnp.dot(xn, wgb[...], preferred_element_type=F32)
        up = jnp.dot(xn, wub[...], preferred_element_type=F32)
        a_ref[rows, :] = ((gate * (FFN_RES_SCALE / (1.0 + jnp.exp(-gate)))) * up).astype(BF16)


def _ffn_up(xn, wg, wu, wd=None, *, tf=512):
    seq, d = xn.shape
    dff = wg.shape[1]
    precast = wd is None
    tm = FFN_UP_TILE
    m_steps = seq // tm
    in_specs = [pl.BlockSpec((tm, d), lambda f, m: (m, 0)),
                pl.BlockSpec((d, tf), lambda f, m: (0, f)),
                pl.BlockSpec((d, tf), lambda f, m: (0, f))]
    out_specs = [pl.BlockSpec((tm, tf), lambda f, m: (m, f))]
    out_shape = [jax.ShapeDtypeStruct((seq, dff), BF16)]
    scratch, args = [], [xn, wg, wu]
    if not precast:
        assert dff % ((dff // tf) * m_steps) == 0
        wd_rows = dff // ((dff // tf) * m_steps)
        wd_spec = pl.BlockSpec((wd_rows, d), lambda f, m: (f * m_steps + m, 0))
        in_specs.append(wd_spec)
        out_specs.append(wd_spec)
        out_shape.append(jax.ShapeDtypeStruct((dff, d), BF16))
        scratch = [pltpu.VMEM((d, tf), BF16), pltpu.VMEM((d, tf), BF16)]
        args.append(wd)
    return pl.pallas_call(
        functools.partial(_ffn_up_kernel, precast=precast),
        grid=(dff // tf, seq // tm),
        in_specs=in_specs,
        out_specs=out_specs,
        out_shape=out_shape,
        scratch_shapes=scratch,
        compiler_params=_params(2),
        name="ffn_up_bf16" if precast else "ffn_up",
    )(*args)


def _ffn_down_kernel(*refs, final, n_cast):
    a_ref, x_ref, wd_ref, g_ref = refs[:4]
    cast_in = refs[4:4 + n_cast]
    out_refs = refs[4 + n_cast:]
    h = x_ref[...] + jnp.dot(a_ref[...], wd_ref[...], preferred_element_type=F32)
    if final:
        out_refs[0][...] = _rms(h, g_ref[...])
    else:
        out_refs[0][...] = h
        out_refs[1][...] = _rms(h, g_ref[...]).astype(BF16)
        _run_casts(cast_in, out_refs[2:])


def _ffn_down(a, x, wd_bf, g, cast_weights=(), *, final, tm=256):
    seq, d = x.shape
    dff = a.shape[1]
    steps = seq // tm
    assert not (final and cast_weights)
    row = pl.BlockSpec((tm, d), lambda m: (m, 0))
    cast_specs, cast_shapes = _cast_jobs(cast_weights, steps, lambda m: (m, 0))
    out_specs = ([row] if final else [row, row]) + cast_specs
    out_shape = [jax.ShapeDtypeStruct((seq, d), F32)]
    if not final:
        out_shape.append(jax.ShapeDtypeStruct((seq, d), BF16))
    return pl.pallas_call(
        functools.partial(_ffn_down_kernel, final=final, n_cast=len(cast_weights)),
        grid=(steps,),
        in_specs=[pl.BlockSpec((tm, dff), lambda m: (m, 0)), row,
                  _resident((dff, d)), pl.BlockSpec((1, d), lambda m: (0, 0))] + cast_specs,
        out_specs=out_specs,
        out_shape=out_shape + cast_shapes,
        compiler_params=_params(1),
        name="ffn_down_final" if final else "ffn_down",
    )(a, x, wd_bf, g.reshape(1, d), *cast_weights)


def _proj_kernel(xn_ref, w_ref, cos_ref, sin_ref, lng_ref, lnb_ref, o_ref, t_ref, *, q_scale):
    width = t_ref.shape[1]

    def segment(seg, epilogue):
        for j in range(width // PROJ_COLS):
            c0 = seg * width + j * PROJ_COLS
            y = jnp.dot(xn_ref[...], w_ref[:, c0:c0 + PROJ_COLS], preferred_element_type=F32)
            epilogue(y, c0)

    def rope(scale):
        def epilogue(y, c0):
            cos = cos_ref[...]
            sin = sin_ref[...]
            lane = jax.lax.broadcasted_iota(jnp.int32, cos.shape, 1)
            first = lane % HEAD_DIM < HEAD_DIM // 2
            for j in range(PROJ_COLS // LANES):
                blk = y[:, j * LANES:(j + 1) * LANES]
                partner = jnp.where(first,
                                    pltpu.roll(blk, LANES - HEAD_DIM // 2, 1),
                                    pltpu.roll(blk, HEAD_DIM // 2, 1))
                out = blk * cos + partner * sin
                if scale != 1.0:
                    out = out * scale
                o_ref[:, c0 + j * LANES:c0 + (j + 1) * LANES] = out.astype(o_ref.dtype)
        return epilogue

    def store(fn):
        def epilogue(y, c0):
            o_ref[:, c0:c0 + PROJ_COLS] = fn(y).astype(o_ref.dtype)
        return epilogue

    sums = []

    def gelu_partial(y, c0):
        t = _gelu(y)
        t_ref[:, c0 - 4 * width:c0 - 4 * width + PROJ_COLS] = t
        sums.append(jnp.sum(t, axis=-1, keepdims=True))

    segment(4, gelu_partial)
    mu = sum(sums) * (1.0 / width)
    tc = t_ref[...] - mu
    t = tc * jax.lax.rsqrt(jnp.mean(tc * tc, axis=-1, keepdims=True) + EPS)
    o_ref[:, 4 * width:] = (t * lng_ref[...] + lnb_ref[...]).astype(o_ref.dtype)
    segment(3, store(_gelu))
    segment(0, rope(q_scale))
    segment(1, rope(1.0))
    segment(2, store(lambda y: y))


def _proj(xn, w_in_bf, cos, sin, ln_g, ln_b, *, width, tm=512):
    seq, d = xn.shape
    d_in = w_in_bf.shape[1]
    assert d_in == 5 * width
    return pl.pallas_call(
        functools.partial(_proj_kernel, q_scale=HEAD_DIM ** -0.5 * np.log2(np.e)),
        grid=(seq // tm,),
        in_specs=[
            pl.BlockSpec((tm, d), lambda m: (m, 0)),
            _resident((d, d_in)),
            pl.BlockSpec((tm, LANES), lambda m: (m, 0)),
            pl.BlockSpec((tm, LANES), lambda m: (m, 0)),
            pl.BlockSpec((1, width), lambda m: (0, 0)),
            pl.BlockSpec((1, width), lambda m: (0, 0)),
        ],
        out_specs=pl.BlockSpec((tm, d_in), lambda m: (m, 0)),
        out_shape=jax.ShapeDtypeStruct((seq, d_in), BF16),
        scratch_shapes=[pltpu.VMEM((tm, width), F32)],
        compiler_params=_params(1),
        name="proj",
    )(xn, w_in_bf, cos, sin, ln_g.reshape(1, width), ln_b.reshape(1, width))


def _attn_kernel(*refs, dil, n_cast):
    q_ref, k_ref, v_ref = refs[:3]
    cast_in = refs[3:3 + n_cast]
    o_ref = refs[3 + n_cast]
    cast_out = refs[4 + n_cast:4 + 2 * n_cast]
    stage, q4, k4, v4, kb, vb, m4, l4, acc4, m_s, l_s, acc_s, bias = refs[4 + 2 * n_cast:]
    _run_casts(cast_in, cast_out)
    t = pl.program_id(1)
    sup = q_ref.shape[0]
    sub = sup // dil
    n_units = sup // BLOCK

    @pl.when(t == 0)
    def _():
        for dst in (k4, v4):
            for b in range(dil):
                dst[b * 2 * sub:b * 2 * sub + sub, :] = jnp.zeros((sub, LANES), F32)
        kb[:BLOCK] = jnp.zeros((BLOCK, LANES), BF16)
        vb[:BLOCK] = jnp.zeros((BLOCK, LANES), BF16)

    @pl.when(t > 0)
    def _():
        for dst in (k4, v4):
            for b in range(dil):
                dst[b * 2 * sub:b * 2 * sub + sub, :] = dst[b * 2 * sub + sub:(b + 1) * 2 * sub, :]
        kb[:BLOCK] = kb[sup:, :]
        vb[:BLOCK] = vb[sup:, :]

    def split(dst, rows_per_class, offset):
        for b in range(dil):
            dst[b * rows_per_class + offset:b * rows_per_class + offset + sub, :] = (
                stage[pl.ds(b, sub, stride=dil), :])

    stage[...] = k_ref[...].astype(F32)
    split(k4, 2 * sub, sub)
    stage[...] = v_ref[...].astype(F32)
    split(v4, 2 * sub, sub)
    stage[...] = q_ref[...].astype(F32)
    split(q4, sub, 0)
    kb[BLOCK:] = k_ref[...]
    vb[BLOCK:] = v_ref[...]

    qi = jax.lax.broadcasted_iota(jnp.int32, (BLOCK, 2 * BLOCK), 0)
    kj = jax.lax.broadcasted_iota(jnp.int32, (BLOCK, 2 * BLOCK), 1)
    diff = qi + BLOCK - kj
    band = (diff >= 0) & (diff <= BLOCK)
    bias[0] = jnp.where(band & (kj >= BLOCK), 0.0, -jnp.inf)
    bias[1] = jnp.where(band, 0.0, -jnp.inf)
    low = jax.lax.broadcasted_iota(jnp.int32, (BLOCK, LANES), 1) < HEAD_DIM

    def head_pair(q, k, v, has_prev):
        mask = bias[has_prev.astype(jnp.int32)]
        ms, ls, os = [], [], []
        for hh in range(2):
            qm = jnp.where(low if hh == 0 else ~low, q, jnp.zeros_like(q))
            s = jax.lax.dot_general(qm, k, (((1,), (1,)), ((), ())),
                                    preferred_element_type=F32)
            s = s + mask
            m = jnp.max(s, axis=-1, keepdims=True)
            e = jnp.exp2(s - m)
            ms.append(m)
            ls.append(jnp.sum(e, axis=-1, keepdims=True))
            os.append(jnp.dot(e.astype(BF16), v, preferred_element_type=F32))
        return (jnp.where(low, ms[0], ms[1]), jnp.where(low, ls[0], ls[1]),
                jnp.where(low, os[0], os[1]))

    def merge(refs, rows, stats):
        m_ref, l_ref, acc_ref = refs
        m_u, l_u, o_u = stats
        m_old = m_ref[rows, :]
        m_new = jnp.maximum(m_old, m_u)
        a_old = jnp.exp2(m_old - m_new)
        a_u = jnp.exp2(m_u - m_new)
        m_ref[rows, :] = m_new
        l_ref[rows, :] = a_old * l_ref[rows, :] + a_u * l_u
        acc_ref[rows, :] = a_old * acc_ref[rows, :] + a_u * o_u

    def mid_unit(u, carry):
        r = u // (sub // BLOCK)
        n = u % (sub // BLOCK)
        rows = pl.ds(pl.multiple_of(r * sub + n * BLOCK, BLOCK), BLOCK)
        kv_rows = pl.ds(pl.multiple_of(r * 2 * sub + sub + (n - 1) * BLOCK, BLOCK), 2 * BLOCK)
        m_u, l_u, o_u = head_pair(q4[rows, :].astype(BF16), k4[kv_rows, :].astype(BF16),
                                  v4[kv_rows, :].astype(BF16), (t > 0) | (n > 0))
        m4[rows, :] = m_u
        l4[rows, :] = l_u
        acc4[rows, :] = o_u
        return carry

    def wide_unit(r, carry):
        a = r // dil
        b = r % dil
        rows = pl.ds(b * sub + a, BLOCK, stride=dil)
        kv_rows = pl.ds(b * 2 * sub + a, 2 * BLOCK, stride=dil)
        stats = head_pair(q4[rows, :].astype(BF16), k4[kv_rows, :].astype(BF16),
                          v4[kv_rows, :].astype(BF16), t > 0)
        merge((m4, l4, acc4), rows, stats)
        return carry

    def near_unit(n, carry):
        rows = pl.ds(pl.multiple_of(n * BLOCK, BLOCK), BLOCK)
        kv_rows = pl.ds(pl.multiple_of(n * BLOCK, BLOCK), 2 * BLOCK)
        stats = head_pair(q_ref[rows, :], kb[kv_rows, :], vb[kv_rows, :], (t > 0) | (n > 0))
        merge((m_s, l_s, acc_s), rows, stats)
        return carry

    jax.lax.fori_loop(0, n_units, mid_unit, 0, unroll=True)
    jax.lax.fori_loop(0, n_units, wide_unit, 0, unroll=True)
    for b in range(dil):
        src = slice(b * sub, (b + 1) * sub)
        dst = pl.ds(b, sub, stride=dil)
        m_s[dst, :] = m4[src, :]
        l_s[dst, :] = l4[src, :]
        acc_s[dst, :] = acc4[src, :]
    jax.lax.fori_loop(0, n_units, near_unit, 0, unroll=True)

    o_ref[...] = (acc_s[...] / l_s[...]).astype(o_ref.dtype)


def _attention(proj, cast_weights=(), *, width):
    seq, d_in = proj.shape
    dilations = tuple(d for _, d in DILATED_BRANCHES)
    steps = {w // d for w, d in DILATED_BRANCHES}
    assert steps == {BLOCK}, "every branch must span exactly one previous block"
    dil = dilations[1]
    assert dilations == (1, dil, dil * dil)
    sup = BLOCK * dil * dil
    seg = width // LANES
    n_sup = seq // sup
    col = lambda i: pl.BlockSpec((sup, LANES), lambda p, t: (t, i * seg + p))
    f32_rows = lambda n: pltpu.VMEM((n, LANES), F32)
    cast_specs, cast_shapes = _cast_jobs(cast_weights, n_sup * seg,
                                         lambda p, t: (p * n_sup + t, 0))
    return pl.pallas_call(
        functools.partial(_attn_kernel, dil=dil, n_cast=len(cast_weights)),
        grid=(seg, n_sup),
        in_specs=[col(0), col(1), col(2)] + cast_specs,
        out_specs=[pl.BlockSpec((sup, LANES), lambda p, t: (t, p))] + cast_specs,
        out_shape=[jax.ShapeDtypeStruct((seq, width), BF16)] + cast_shapes,
        scratch_shapes=[f32_rows(sup),
                        f32_rows(sup), f32_rows(2 * sup), f32_rows(2 * sup),
                        pltpu.VMEM((sup + BLOCK, LANES), BF16),
                        pltpu.VMEM((sup + BLOCK, LANES), BF16),
                        f32_rows(sup), f32_rows(sup), f32_rows(sup),
                        f32_rows(sup), f32_rows(sup), f32_rows(sup),
                        pltpu.VMEM((2, BLOCK, 2 * BLOCK), F32)],
        compiler_params=_params(2),
        name="dilated_attn",
    )(proj, proj, proj, *cast_weights)


def _mix_kernel(h_ref, attn_ref, u_ref, gv_ref, ws_ref, bs_ref, ag_ref, sg_ref, wo_ref, ng_ref,
                out_ref, xn_ref, mixed_ref):
    tm = h_ref.shape[0]
    width = u_ref.shape[1]
    mixed_ref[:, :width] = _rms(attn_ref[...].astype(F32), ag_ref[...]).astype(BF16)

    ci = jax.lax.broadcasted_iota(jnp.int32, (SGU_CHUNK, SGU_CHUNK), 0)
    cj = jax.lax.broadcasted_iota(jnp.int32, (SGU_CHUNK, SGU_CHUNK), 1)
    causal = cj <= ci
    for g in range(width // SGU_GROUP_DIM):
        gs = slice(g * SGU_GROUP_DIM, (g + 1) * SGU_GROUP_DIM)
        w = jnp.where(causal, ws_ref[g], 0.0).astype(BF16)
        b = bs_ref[:, g:g + 1]
        for c in range(tm // SGU_CHUNK):
            rs = slice(c * SGU_CHUNK, (c + 1) * SGU_CHUNK)
            sv = jnp.dot(w, gv_ref[rs, gs], preferred_element_type=F32) + b
            out_ref[rs, gs] = u_ref[rs, gs].astype(F32) * sv
    sgu = out_ref[:, :width]
    mixed_ref[:, width:] = _rms(sgu, sg_ref[...]).astype(BF16)

    h = h_ref[...] + jnp.dot(mixed_ref[...], wo_ref[...], preferred_element_type=F32)
    out_ref[...] = h
    xn_ref[...] = _rms(h, ng_ref[...]).astype(BF16)


def _mix(h, attn, proj, w_s, b_s, attn_g, sgu_g, w_out, next_g, *, width, tm=512):
    seq, d = h.shape
    n_grp = w_s.shape[0]
    row = lambda m: (m, 0)
    const2 = lambda m: (0, 0)
    return pl.pallas_call(
        _mix_kernel,
        grid=(seq // tm,),
        in_specs=[pl.BlockSpec((tm, d), row),
                  pl.BlockSpec((tm, width), row),
                  pl.BlockSpec((tm, width), lambda m: (m, 3)),
                  pl.BlockSpec((tm, width), lambda m: (m, 4)),
                  pl.BlockSpec((n_grp, SGU_CHUNK, SGU_CHUNK), lambda m: (0, 0, 0)),
                  pl.BlockSpec((SGU_CHUNK, n_grp), const2),
                  pl.BlockSpec((1, width), const2),
                  pl.BlockSpec((1, width), const2),
                  _resident((2 * width, d)),
                  pl.BlockSpec((1, d), const2)],
        out_specs=[pl.BlockSpec((tm, d), row), pl.BlockSpec((tm, d), row)],
        out_shape=[jax.ShapeDtypeStruct((seq, d), F32), jax.ShapeDtypeStruct((seq, d), BF16)],
        scratch_shapes=[pltpu.VMEM((tm, 2 * width), BF16)],
        compiler_params=_params(1),
        name="mix_out",
    )(h, attn, proj, proj, w_s, b_s.T, attn_g.reshape(1, width), sgu_g.reshape(1, width),
      w_out, next_g.reshape(1, d))


def _mem_fold_kernel(mem_ref, g_ref, wk_ref, wv_ref, wq_ref, wo_ref, kq_ref, vo_ref):
    hd = wk_ref.shape[1]
    mk = _rms(mem_ref[...], g_ref[...]).astype(BF16)
    k = jnp.dot(mk, wk_ref[...], preferred_element_type=F32).astype(BF16)
    v = jnp.dot(mk, wv_ref[...], preferred_element_type=F32).astype(BF16)
    kq = jax.lax.dot_general(wq_ref[...], k, (((1,), (1,)), ((), ())),
                             preferred_element_type=F32)
    kq_ref[...] = (kq * (hd ** -0.5)).astype(BF16)
    vo_ref[...] = jnp.dot(v, wo_ref[...], preferred_element_type=F32).astype(BF16)


def _mem_fold(mem, g, w_k, w_v, w_q, w_o):
    n_mem, d = mem.shape
    hd = d // N_MEM_HEADS
    cols = pl.BlockSpec((d, hd), lambda h: (0, h))
    return pl.pallas_call(
        _mem_fold_kernel,
        grid=(N_MEM_HEADS,),
        in_specs=[pl.BlockSpec((n_mem, d), lambda h: (0, 0)),
                  pl.BlockSpec((1, d), lambda h: (0, 0)),
                  cols, cols, cols,
                  pl.BlockSpec((hd, d), lambda h: (h, 0))],
        out_specs=[pl.BlockSpec((d, n_mem), lambda h: (0, h)),
                   pl.BlockSpec((n_mem, d), lambda h: (h, 0))],
        out_shape=[jax.ShapeDtypeStruct((d, N_MEM_HEADS * n_mem), BF16),
                   jax.ShapeDtypeStruct((N_MEM_HEADS * n_mem, d), BF16)],
        compiler_params=_params(1),
        name="mem_fold",
    )(mem, g.reshape(1, d), w_k, w_v, w_q, w_o)


def _mem_cross_kernel(h_ref, xq_ref, kq_ref, vo_ref, ng_ref, out_ref, xn_ref, p_ref):
    n_mem = kq_ref.shape[1] // N_MEM_HEADS
    for i in range(N_MEM_HEADS):
        sl = slice(i * n_mem, (i + 1) * n_mem)
        s = jnp.dot(xq_ref[...], kq_ref[:, sl], preferred_element_type=F32)
        m = jnp.max(s, axis=-1, keepdims=True)
        e = jnp.exp(s - m)
        p_ref[:, sl] = (e / jnp.sum(e, axis=-1, keepdims=True)).astype(BF16)
    h = h_ref[...] + jnp.dot(p_ref[...], vo_ref[...], preferred_element_type=F32)
    out_ref[...] = h
    xn_ref[...] = _rms(h, ng_ref[...]).astype(BF16)


def _mem_cross(h, xq, kq, vo, next_g, *, tm=512):
    seq, d = h.shape
    row = pl.BlockSpec((tm, d), lambda m: (m, 0))
    return pl.pallas_call(
        _mem_cross_kernel,
        grid=(seq // tm,),
        in_specs=[row, row, _resident(kq.shape), _resident(vo.shape),
                  pl.BlockSpec((1, d), lambda m: (0, 0))],
        out_specs=[row, row],
        out_shape=[jax.ShapeDtypeStruct((seq, d), F32), jax.ShapeDtypeStruct((seq, d), BF16)],
        scratch_shapes=[pltpu.VMEM((tm, kq.shape[1]), BF16)],
        compiler_params=_params(1),
        name="mem_cross",
    )(h, xq, kq, vo, next_g.reshape(1, d))


def kernel(x, mem, positions, ffn1_norm, ffn1_w_gate, ffn1_w_up, ffn1_w_down, mix_norm, w_in,
           sgu_ln_gain, sgu_ln_bias, sgu_w_s, sgu_b_s, attn_out_gain, sgu_out_gain, w_out,
           mem_q_norm, mem_kv_norm, mem_w_q, mem_w_k, mem_w_v, mem_w_o, ffn2_norm,
           ffn2_w_gate, ffn2_w_up, ffn2_w_down, final_norm):
    batch, seq, d = x.shape
    depth = ffn1_norm.shape[0]
    width = attn_out_gain.shape[1]
    assert batch == 1 and mem.shape[0] == 1 and depth == 1
    layer = 0

    h = x.reshape(seq, d)
    mem2 = mem.reshape(mem.shape[1], d)

    cos, sin, xn = _prep(positions, h, ffn1_norm[layer])
    a, wd_bf = _ffn_up(xn, ffn1_w_gate[layer], ffn1_w_up[layer], ffn1_w_down[layer])
    h, xn, w_out_bf, w_in_bf, wk_bf, wv_bf, wq_bf, wo_bf = _ffn_down(
        a, h, wd_bf, mix_norm[layer],
        (w_out[layer], w_in[layer], mem_w_k[layer], mem_w_v[layer], mem_w_q[layer],
         mem_w_o[layer]), final=False)
    proj = _proj(xn, w_in_bf, cos, sin, sgu_ln_gain[layer], sgu_ln_bias[layer], width=width)
    attn, wg2_bf, wu2_bf, wd2_bf = _attention(
        proj, (ffn2_w_gate[layer], ffn2_w_up[layer], ffn2_w_down[layer]), width=width)
    h, xn = _mix(h, attn, proj, sgu_w_s[layer], sgu_b_s[layer], attn_out_gain[layer],
                 sgu_out_gain[layer], w_out_bf, mem_q_norm[layer], width=width)
    kq, vo = _mem_fold(mem2, mem_kv_norm[layer], wk_bf, wv_bf, wq_bf, wo_bf)
    h, xn = _mem_cross(h, xn, kq, vo, ffn2_norm[layer])
    a, = _ffn_up(xn, wg2_bf, wu2_bf)
    out = _ffn_down(a, h, wd2_bf, final_norm, final=True)[0]
    return out.reshape(batch, seq, d)
```
